```python
import math
import jax, jax.numpy as jnp
from jax import lax
import numpy as np

D_MODEL = 1024
BATCH = 32
SEQ = 256
DEPTH = 2
DEC_BATCH = 4
DEC_SEQ = 1024
PAST_LEN = 512

GRID_W = 64
Q_BLOCK = 128
HEAD_DIM = 64
N_HEADS_A = D_MODEL // 2 // HEAD_DIM
N_KV_A = N_HEADS_A // 4
N_HEADS_B = D_MODEL // 4 // HEAD_DIM
DIFF_DIM = HEAD_DIM // 2
GROUP_C = 64
N_GROUPS_C = D_MODEL // 4 // GROUP_C
WIDTH_A = N_HEADS_A * HEAD_DIM
WIDTH_B = N_HEADS_B * HEAD_DIM
WIDTH_C = N_GROUPS_C * GROUP_C
MIX_WIDTH = WIDTH_A + WIDTH_B + WIDTH_C
KV_WIDTH_A = N_KV_A * HEAD_DIM
PROJ_SPLITS = (WIDTH_A, KV_WIDTH_A, KV_WIDTH_A, WIDTH_A,
               WIDTH_B, WIDTH_B, WIDTH_B, WIDTH_B,
               WIDTH_C, WIDTH_C)
D_IN = sum(PROJ_SPLITS)
RMS_EPS = 1e-6
ROPE_BASE = 10000.0

kernel_name = 'hybrid_diffusion_parallel_heads_step'


def rms_norm(x, g):
    xf = x.astype(jnp.float32)
    y = xf * lax.rsqrt(jnp.mean(xf * xf, axis=-1, keepdims=True) + RMS_EPS)
    return (y * g.astype(jnp.float32)).astype(x.dtype)


def split_columns(p):
    idx, acc = [], 0
    for w in PROJ_SPLITS[:-1]:
        acc += w
        idx.append(acc)
    return jnp.split(p, idx, axis=-1)


def axial_angles(n_tok, dim):
    rows = n_tok // GRID_W
    row = jnp.repeat(jnp.arange(rows), GRID_W).astype(jnp.float32)
    col = jnp.tile(jnp.arange(GRID_W), rows).astype(jnp.float32)
    n_freq = dim // 4
    inv = 1.0 / (ROPE_BASE ** (jnp.arange(n_freq, dtype=jnp.float32) / n_freq))
    return row[:, None] * inv, col[:, None] * inv


def rope_half(x, ang):
    shp = (ang.shape[0],) + (1,) * (x.ndim - 3) + (ang.shape[1],)
    cos = jnp.cos(ang).reshape(shp).astype(x.dtype)
    sin = jnp.sin(ang).reshape(shp).astype(x.dtype)
    x1, x2 = jnp.split(x, 2, axis=-1)
    return jnp.concatenate([x1 * cos - x2 * sin, x1 * sin + x2 * cos], axis=-1)


def rope_2d(x, angs):
    xr, xc = jnp.split(x, 2, axis=-1)
    return jnp.concatenate([rope_half(xr, angs[0]), rope_half(xc, angs[1])], axis=-1)


def sweep_query_blocks(fn, *qs):
    b, s = qs[0].shape[:2]
    nb = s // Q_BLOCK
    blocks = tuple(jnp.moveaxis(q.reshape((b, nb, Q_BLOCK) + q.shape[2:]), 1, 0) for q in qs)
    out = lax.map(lambda qb: fn(*qb), blocks)
    return jnp.moveaxis(out, 0, 1).reshape((b, s) + out.shape[3:])


def gqa_attention(q, k, v):
    b, _, h, d = q.shape
    kvh = k.shape[2]
    g = h // kvh
    scale = d ** -0.5

    def block(qb):
        qb = qb.reshape(b, Q_BLOCK, kvh, g, d)
        sc = jnp.einsum('bqhgd,bkhd->bhgqk', qb, k).astype(jnp.float32) * scale
        pr = jax.nn.softmax(sc, axis=-1).astype(v.dtype)
        o = jnp.einsum('bhgqk,bkhd->bqhgd', pr, v)
        return o.reshape(b, Q_BLOCK, h, d)

    return sweep_query_blocks(block, q)


def diff_attention(q1, q2, k1, k2, v, lam):
    scale = q1.shape[-1] ** -0.5

    def block(q1b, q2b):
        s1 = jnp.einsum('bqhd,bkhd->bhqk', q1b, k1).astype(jnp.float32) * scale
        s2 = jnp.einsum('bqhd,bkhd->bhqk', q2b, k2).astype(jnp.float32) * scale
        pr = jax.nn.softmax(s1, axis=-1) - lam * jax.nn.softmax(s2, axis=-1)
        return jnp.einsum('bhqk,bkhd->bqhd', pr.astype(v.dtype), v)

    return sweep_query_blocks(block, q1, q2)


def fourier_mix(u, w_c):
    b, s, _ = u.shape
    uf = u.astype(jnp.float32).reshape(b, s, N_GROUPS_C, GROUP_C)
    f = jnp.fft.fft2(uf, axes=(1, 3), norm='ortho').real
    return f.reshape(b, s, WIDTH_C).astype(u.dtype) @ w_c


def modulation(cond, w_mod, b_mod):
    m = jax.nn.silu(cond) @ w_mod + b_mod
    return jnp.split(m, 3, axis=-1)


def mixer_layer(x, shift, scale, gate, norm_g, w_in, qn_a, kn_a, qn_b, kn_b,
                lq1, lk1, lq2, lk2, subln_g, w_c, w_out, layer_idx,
                angs_a, angs_b, ctx_a, ctx_b):
    b, s, _ = x.shape
    h = rms_norm(x, norm_g) * (1.0 + scale) + shift
    qa, ka, va, ga, qb, kb, vb, gb, uc, gc = split_columns(h @ w_in)

    qa = rms_norm(qa.reshape(b, s, N_HEADS_A, HEAD_DIM), qn_a)
    ka = rms_norm(ka.reshape(b, s, N_KV_A, HEAD_DIM), kn_a)
    va = va.reshape(b, s, N_KV_A, HEAD_DIM)
    qb = rms_norm(qb.reshape(b, s, N_HEADS_B, 2, DIFF_DIM), qn_b)
    kb = rms_norm(kb.reshape(b, s, N_HEADS_B, 2, DIFF_DIM), kn_b)
    vb = vb.reshape(b, s, N_HEADS_B, HEAD_DIM)

    if ctx_a is None:
        new_a = jnp.stack([ka, va], axis=2)
        new_b = jnp.stack([kb.reshape(b, s, N_HEADS_B, HEAD_DIM), vb], axis=2)
        ka_all, va_all, kb_all, vb_all = ka, va, kb, vb
    else:
        new_a, new_b = None, None
        qa, ka = rope_2d(qa, angs_a), rope_2d(ka, angs_a)
        qb, kb = rope_2d(qb, angs_b), rope_2d(kb, angs_b)
        n_ctx = ctx_a.shape[1]
        ka_all = jnp.concatenate([ka, ctx_a[:, :, 0]], axis=1)
        va_all = jnp.concatenate([va, ctx_a[:, :, 1]], axis=1)
        kb_ctx = ctx_b[:, :, 0].reshape(b, n_ctx, N_HEADS_B, 2, DIFF_DIM)
        kb_all = jnp.concatenate([kb, kb_ctx], axis=1)
        vb_all = jnp.concatenate([vb, ctx_b[:, :, 1]], axis=1)

    oa = gqa_attention(qa, ka_all, va_all).reshape(b, s, WIDTH_A)

    lam_init = 0.8 - 0.6 * math.exp(-0.3 * layer_idx)
    lam = (jnp.exp(jnp.sum(lq1.astype(jnp.float32) * lk1.astype(jnp.float32)))
           - jnp.exp(jnp.sum(lq2.astype(jnp.float32) * lk2.astype(jnp.float32))) + lam_init)
    ob = diff_attention(qb[..., 0, :], qb[..., 1, :], kb_all[..., 0, :], kb_all[..., 1, :], vb_all, lam)
    ob = (rms_norm(ob, subln_g) * (1.0 - lam_init)).reshape(b, s, WIDTH_B)

    oc = fourier_mix(uc, w_c)

    mix = jnp.concatenate([oa * jax.nn.silu(ga), ob * jax.nn.silu(gb), oc * jax.nn.silu(gc)], axis=-1)
    return x + gate * (mix @ w_out), new_a, new_b


def setup_inputs(seed: int = 0) -> dict:
    key = jax.random.key(seed)
    ks = jax.random.split(key, 24)
    f32 = jnp.float32
    nrm = lambda k, shp: jax.random.normal(k, shp, f32)
    return {
        'x_prompt': nrm(ks[0], (BATCH, SEQ, D_MODEL)),
        'x_sample': nrm(ks[1], (DEC_BATCH, DEC_SEQ, D_MODEL)),
        'cache_attn_a': nrm(ks[2], (DEC_BATCH, DEPTH, PAST_LEN, 2, N_KV_A, HEAD_DIM)),
        'cache_attn_b': nrm(ks[3], (DEC_BATCH, DEPTH, PAST_LEN, 2, N_HEADS_B, HEAD_DIM)),
        'c': nrm(ks[4], (DEC_BATCH, D_MODEL)),
        'c_ctx': nrm(ks[5], (D_MODEL,)),
        'norm_g': 1.0 + 0.01 * nrm(ks[6], (DEPTH, D_MODEL)),
        'w_mod': nrm(ks[7], (DEPTH, D_MODEL, 3 * D_MODEL)) * (0.5 * D_MODEL ** -0.5),
        'b_mod': 0.01 * nrm(ks[8], (DEPTH, 3 * D_MODEL)),
        'w_in': nrm(ks[9], (DEPTH, D_MODEL, D_IN)) * D_MODEL ** -0.5,
        'q_norm_a': 1.0 + 0.01 * nrm(ks[10], (DEPTH, HEAD_DIM)),
        'k_norm_a': 1.0 + 0.01 * nrm(ks[11], (DEPTH, HEAD_DIM)),
        'q_norm_b': 1.0 + 0.01 * nrm(ks[12], (DEPTH, DIFF_DIM)),
        'k_norm_b': 1.0 + 0.01 * nrm(ks[13], (DEPTH, DIFF_DIM)),
        'lambda_q1': 0.1 * nrm(ks[14], (DEPTH, DIFF_DIM)),
        'lambda_k1': 0.1 * nrm(ks[15], (DEPTH, DIFF_DIM)),
        'lambda_q2': 0.1 * nrm(ks[16], (DEPTH, DIFF_DIM)),
        'lambda_k2': 0.1 * nrm(ks[17], (DEPTH, DIFF_DIM)),
        'subln_g': 1.0 + 0.01 * nrm(ks[18], (DEPTH, HEAD_DIM)),
        'w_fourier': nrm(ks[19], (DEPTH, WIDTH_C, WIDTH_C)) * WIDTH_C ** -0.5,
        'w_out': nrm(ks[20], (DEPTH, MIX_WIDTH, D_MODEL)) * MIX_WIDTH ** -0.5,
    }


def reference(x_prompt, x_sample, cache_attn_a, cache_attn_b, c, c_ctx,
              norm_g, w_mod, b_mod, w_in, q_norm_a, k_norm_a, q_norm_b, k_norm_b,
              lambda_q1, lambda_k1, lambda_q2, lambda_k2, subln_g, w_fourier, w_out):
    y_prompt = x_prompt
    kv_a_layers, kv_b_layers = [], []
    for l in range(DEPTH):
        shift, scale, gate = modulation(c_ctx, w_mod[l], b_mod[l])
        y_prompt, kv_a, kv_b = mixer_layer(
            y_prompt, shift, scale, gate, norm_g[l], w_in[l],
            q_norm_a[l], k_norm_a[l], q_norm_b[l], k_norm_b[l],
            lambda_q1[l], lambda_k1[l], lambda_q2[l], lambda_k2[l],
            subln_g[l], w_fourier[l], w_out[l], l, None, None, None, None)
        kv_a_layers.append(kv_a)
        kv_b_layers.append(kv_b)
    new_attn_a = jnp.stack(kv_a_layers, axis=1)
    new_attn_b = jnp.stack(kv_b_layers, axis=1)

    n_lat = x_sample.shape[1]
    angs_a = axial_angles(n_lat, HEAD_DIM)
    angs_b = axial_angles(n_lat, DIFF_DIM)
    y_sample = x_sample
    for l in range(DEPTH):
        shift, scale, gate = modulation(c, w_mod[l], b_mod[l])
        y_sample, _, _ = mixer_layer(
            y_sample, shift[:, None, :], scale[:, None, :], gate[:, None, :],
            norm_g[l], w_in[l], q_norm_a[l], k_norm_a[l], q_norm_b[l], k_norm_b[l],
            lambda_q1[l], lambda_k1[l], lambda_q2[l], lambda_k2[l],
            subln_g[l], w_fourier[l], w_out[l], l, angs_a, angs_b,
            cache_attn_a[:, l], cache_attn_b[:, l])
    return (y_prompt, y_sample, new_attn_a, new_attn_b)
```

```python
import functools
import math

import jax
import jax.numpy as jnp
import numpy as np
from jax import lax
from jax.experimental import pallas as pl
from jax.experimental.pallas import tpu as pltpu

D_MODEL = 1024
DEPTH = 2
GRID_W = 64
HEAD_DIM = 64
N_HEADS_A = 8
N_KV_A = 2
N_HEADS_B = 4
DIFF_DIM = 32
GROUP_C = 64
N_GROUPS_C = 4
WIDTH_A = N_HEADS_A * HEAD_DIM
WIDTH_B = N_HEADS_B * HEAD_DIM
WIDTH_C = N_GROUPS_C * GROUP_C
KV_WIDTH_A = N_KV_A * HEAD_DIM
RMS_EPS = 1e-6
ROPE_BASE = 10000.0

TOKEN_BLOCK = 256
MOD_ROWS = 8
V7X_VMEM_LIMIT_BYTES = 56 * 1024 * 1024

_OFF = np.cumsum([0, WIDTH_A, KV_WIDTH_A, KV_WIDTH_A, WIDTH_A, WIDTH_B, WIDTH_B, WIDTH_B, WIDTH_B,
                  WIDTH_C, WIDTH_C])
F_QA, F_GA, F_QB, F_GB, F_GC, F_UC = 0, 512, 1024, 1280, 1536, 1792
F_STEP_ROWS = 1792
F_ROWS = 2048
T_KA, T_VA, T_KB, T_VB = 0, 128, 256, 512
T_COLS = 768
V_ROWS = KV_WIDTH_A + WIDTH_B

_BF16 = jnp.bfloat16
_F32 = jnp.float32


def _silu(x):
    return x * (1.0 / (1.0 + jnp.exp(-x)))


def _dot(a, b):
    return jnp.dot(a, b, preferred_element_type=_F32)


def _dot_nt(a, b):
    return lax.dot_general(a, b, (((1,), (1,)), ((), ())), preferred_element_type=_F32)


def _mod_kernel(c_ref, w_ref, b_ref, o_ref):
    c = c_ref[...]
    o_ref[0] = _dot(_silu(c).astype(_BF16), w_ref[0].astype(_BF16)) + b_ref[0]


def _modulation(cond, w_mod, b_mod):
    n_tile = D_MODEL
    return pl.pallas_call(
        _mod_kernel,
        grid=(DEPTH, 3 * D_MODEL // n_tile),
        in_specs=[
            pl.BlockSpec((MOD_ROWS, D_MODEL), lambda l, n: (0, 0)),
            pl.BlockSpec((1, D_MODEL, n_tile), lambda l, n: (l, 0, n)),
            pl.BlockSpec((1, 1, n_tile), lambda l, n: (l, 0, n)),
        ],
        out_specs=pl.BlockSpec((1, MOD_ROWS, n_tile), lambda l, n: (l, 0, n)),
        out_shape=jax.ShapeDtypeStruct((DEPTH, MOD_ROWS, 3 * D_MODEL), _F32),
        name="modulation",
    )(cond, w_mod, b_mod.reshape(DEPTH, 1, 3 * D_MODEL))


def _softmax_t(s_t):
    m = jnp.max(s_t, axis=0, keepdims=True)
    e = jnp.exp(s_t - m)
    return e.astype(_BF16), jnp.sum(e, axis=0, keepdims=True)


def _rms_rows(x):
    return lax.rsqrt(jnp.mean(x * x, axis=0, keepdims=True) + RMS_EPS)


def _swap_halves(x, m):
    return jnp.concatenate([x[m:2 * m], x[0:m], x[3 * m:4 * m], x[2 * m:3 * m]], axis=0)


def _pad_rows(x, start, total):
    parts = []
    if start:
        parts.append(jnp.zeros((start, x.shape[1]), x.dtype))
    parts.append(x)
    rest = total - start - x.shape[0]
    if rest:
        parts.append(jnp.zeros((rest, x.shape[1]), x.dtype))
    return jnp.concatenate(parts, axis=0)


def _layer_kernel(*refs, latent, seq, n_keys, lam_init):
    n_blk = seq // TOKEN_BLOCK
    it = iter(refs)
    xfull_ref, xblk_ref, mod_ref, ng_ref = next(it), next(it), next(it), next(it)
    wft_ref, wt_ref, wct_ref, woutt_ref = next(it), next(it), next(it), next(it)
    gqa_ref, gka_ref, gqb_ref, gkb_ref, gsub_ref, lam_ref = (next(it) for _ in range(6))
    bd64_ref, bd32_ref, bc_ref, bs_ref, cs_ref, nss_ref = (next(it) for _ in range(6))
    if latent:
        ctxa_ref, ctxb_ref = next(it), next(it)
        cqa_ref, sqa_ref, cqb_ref, sqb_ref = (next(it) for _ in range(4))
        cka_ref, skpa_ref, skma_ref, ckb_ref, skpb_ref, skmb_ref = (next(it) for _ in range(6))
        y_ref = next(it)
    else:
        y_ref, newa_ref, newb_ref = next(it), next(it), next(it)
    h_s, ka_s, kb_s, vt_s, t1_s, t2_s, mixt_s = (next(it) for _ in range(7))

    j = pl.program_id(1)
    shift = mod_ref[0, 0:1, :]
    scale = mod_ref[0, 1:2, :]
    gate = mod_ref[0, 2:3, :]

    def rope_k(k, cos_ref, sp_ref, sm_ref, rows, m):
        out = []
        for c0 in range(0, k.shape[1], 128):
            kc = k[:, c0:c0 + 128]
            out.append(kc * cos_ref[rows, c0:c0 + 128]
                       + pltpu.roll(kc, m, 1) * sp_ref[rows, c0:c0 + 128]
                       + pltpu.roll(kc, 128 - m, 1) * sm_ref[rows, c0:c0 + 128])
        return out[0] if len(out) == 1 else jnp.concatenate(out, axis=1)

    def prep():
        for c in range(n_blk):
            rows = slice(c * TOKEN_BLOCK, (c + 1) * TOKEN_BLOCK)
            xc = xfull_ref[0, rows, :]
            ms = jnp.mean(xc * xc, axis=-1, keepdims=True)
            hc = (xc * lax.rsqrt(ms + RMS_EPS) * ng_ref[...] * (1.0 + scale) + shift).astype(_BF16)
            h_s[rows, :] = hc
            pt = _dot(hc, wt_ref[...])
            ka = pt[:, T_KA:T_KA + KV_WIDTH_A]
            va = pt[:, T_VA:T_VA + KV_WIDTH_A]
            kb = pt[:, T_KB:T_KB + WIDTH_B]
            vb = pt[:, T_VB:T_VB + WIDTH_B]
            ka = ka * lax.rsqrt(_dot((ka * ka).astype(_BF16), bd64_ref[...]) + RMS_EPS) * gka_ref[...]
            kb = kb * lax.rsqrt(_dot((kb * kb).astype(_BF16), bd32_ref[...]) + RMS_EPS) * gkb_ref[...]
            if latent:
                ka = rope_k(ka, cka_ref, skpa_ref, skma_ref, rows, HEAD_DIM // 4)
                kb = rope_k(kb, ckb_ref, skpb_ref, skmb_ref, rows, DIFF_DIM // 4)
            else:
                newa_ref[0, rows, 0:KV_WIDTH_A] = ka
                newa_ref[0, rows, KV_WIDTH_A:2 * KV_WIDTH_A] = va
                newb_ref[0, rows, 0:WIDTH_B] = kb
                newb_ref[0, rows, WIDTH_B:2 * WIDTH_B] = vb
            ka_s[rows, :] = ka.astype(_BF16)
            kb_s[rows, :] = kb.astype(_BF16)
            vt_s[0:KV_WIDTH_A, rows] = va.T.astype(_BF16)
            vt_s[KV_WIDTH_A:V_ROWS, rows] = vb.T.astype(_BF16)
            uct = _dot_nt(wft_ref[F_UC:F_ROWS, :], hc).astype(_BF16)
            t1_s[:, rows] = _dot(bc_ref[...], uct).astype(_BF16)
            t2_s[:, rows] = _dot(bs_ref[...], uct).astype(_BF16)
        if latent:
            ca = ctxa_ref[0, 0]
            cb = ctxb_ref[0, 0]
            ka_s[seq:n_keys, :] = ca[:, 0:KV_WIDTH_A].astype(_BF16)
            kb_s[seq:n_keys, :] = cb[:, 0:WIDTH_B].astype(_BF16)
            vt_s[0:KV_WIDTH_A, seq:n_keys] = ca[:, KV_WIDTH_A:].T.astype(_BF16)
            vt_s[KV_WIDTH_A:V_ROWS, seq:n_keys] = cb[:, WIDTH_B:].T.astype(_BF16)

    if n_blk == 1:
        prep()
        h_blk = h_s[...]
    else:
        pl.when(j == 0)(prep)
        h_blk = h_s[pl.ds(pl.multiple_of(j * TOKEN_BLOCK, TOKEN_BLOCK), TOKEN_BLOCK), :]

    pf = _dot_nt(wft_ref[0:F_STEP_ROWS, :], h_blk)

    ka_all = ka_s[...]
    for h in range(N_HEADS_A):
        g = h // (N_HEADS_A // N_KV_A)
        q = pf[F_QA + h * HEAD_DIM:F_QA + (h + 1) * HEAD_DIM, :]
        q = q * _rms_rows(q) * gqa_ref[...]
        if latent:
            q = q * cqa_ref[...] + _swap_halves(q, HEAD_DIM // 4) * sqa_ref[...]
        qp = _pad_rows(q.astype(_BF16), g * HEAD_DIM, KV_WIDTH_A)
        e, den = _softmax_t(_dot(ka_all, qp))
        o = _dot(vt_s[g * HEAD_DIM:(g + 1) * HEAD_DIM, :], e) / den
        ga = pf[F_GA + h * HEAD_DIM:F_GA + (h + 1) * HEAD_DIM, :]
        mixt_s[h * HEAD_DIM:(h + 1) * HEAD_DIM, :] = (o * _silu(ga)).astype(_BF16)

    lv = lam_ref[...]
    lam = (jnp.exp(jnp.sum(lv[0:1] * lv[1:2], axis=-1, keepdims=True))
           - jnp.exp(jnp.sum(lv[2:3] * lv[3:4], axis=-1, keepdims=True)) + lam_init)
    kb_all = kb_s[...]
    for h in range(N_HEADS_B):
        parts = []
        for comp in range(2):
            r0 = F_QB + h * HEAD_DIM + comp * DIFF_DIM
            q = pf[r0:r0 + DIFF_DIM, :]
            q = q * _rms_rows(q) * gqb_ref[...]
            if latent:
                q = q * cqb_ref[...] + _swap_halves(q, DIFF_DIM // 4) * sqb_ref[...]
            qp = _pad_rows(q.astype(_BF16), h * HEAD_DIM + comp * DIFF_DIM, WIDTH_B)
            e, den = _softmax_t(_dot(kb_all, qp))
            v_rows = slice(KV_WIDTH_A + h * HEAD_DIM, KV_WIDTH_A + (h + 1) * HEAD_DIM)
            parts.append(_dot(vt_s[v_rows, :], e) / den)
        o = parts[0] - lam * parts[1]
        o = o * _rms_rows(o) * gsub_ref[...]
        gb = pf[F_GB + h * HEAD_DIM:F_GB + (h + 1) * HEAD_DIM, :]
        mixt_s[WIDTH_A + h * HEAD_DIM:WIDTH_A + (h + 1) * HEAD_DIM, :] = (o * _silu(gb)).astype(_BF16)

    ft = _dot(t1_s[...], cs_ref[...]) + _dot(t2_s[...], nss_ref[...])
    oc = _dot(wct_ref[...], ft.astype(_BF16))
    gc = pf[F_GC:F_GC + WIDTH_C, :]
    mixt_s[WIDTH_A + WIDTH_B:, :] = (oc * _silu(gc)).astype(_BF16)

    yt = _dot(woutt_ref[...], mixt_s[...])
    y_ref[0] = xblk_ref[0] + gate * yt.T


def _const_spec(shape):
    nd = len(shape)
    return pl.BlockSpec(shape, lambda b, j: (0,) * nd, pipeline_mode=pl.Buffered(1))


def _mixer_layer(x, mod, layer_idx, mod_row, wts, consts, ctx=None):
    latent = ctx is not None
    n_batch, seq, _ = x.shape
    n_blk = seq // TOKEN_BLOCK
    n_keys = seq + (ctx[0].shape[2] if latent else 0)
    lam_init = 0.8 - 0.6 * math.exp(-0.3 * layer_idx)

    args = [x, x, mod, wts["norm_g"]]
    specs = [
        pl.BlockSpec((1, seq, D_MODEL), lambda b, j: (b, 0, 0)),
        pl.BlockSpec((1, TOKEN_BLOCK, D_MODEL), lambda b, j: (b, j, 0)),
        pl.BlockSpec((1, 3, D_MODEL), lambda b, j: (layer_idx * MOD_ROWS + mod_row(b), 0, 0)),
        _const_spec((1, D_MODEL)),
    ]
    for name in ("wft", "wt", "wct", "woutt", "gqa", "gka", "gqb", "gkb", "gsub", "lam"):
        args.append(wts[name])
        specs.append(_const_spec(wts[name].shape))
    for name in ("bd64", "bd32", "bc", "bs"):
        args.append(consts[name])
        specs.append(_const_spec(consts[name].shape))
    for name in ("cs", "nss"):
        args.append(consts[name])
        specs.append(pl.BlockSpec((seq, TOKEN_BLOCK), lambda b, j: (0, j)))
    if latent:
        past = ctx[0].shape[2]
        args += [ctx[0], ctx[1]]
        specs += [
            pl.BlockSpec((1, 1, past, 2 * KV_WIDTH_A), lambda b, j: (b, layer_idx, 0, 0)),
            pl.BlockSpec((1, 1, past, 2 * WIDTH_B), lambda b, j: (b, layer_idx, 0, 0)),
        ]
        for name, rows in (("cqa", HEAD_DIM), ("sqa", HEAD_DIM), ("cqb", DIFF_DIM), ("sqb", DIFF_DIM)):
            args.append(consts[name])
            specs.append(pl.BlockSpec((rows, TOKEN_BLOCK), lambda b, j: (0, j)))
        for name in ("cka", "skpa", "skma", "ckb", "skpb", "skmb"):
            args.append(consts[name])
            specs.append(_const_spec(consts[name].shape))

    y_shape = jax.ShapeDtypeStruct(x.shape, _F32)
    y_spec = pl.BlockSpec((1, TOKEN_BLOCK, D_MODEL), lambda b, j: (b, j, 0))
    if latent:
        out_shape, out_specs = y_shape, y_spec
    else:
        out_shape = (y_shape,
                     jax.ShapeDtypeStruct((n_batch, seq, 2 * KV_WIDTH_A), _F32),
                     jax.ShapeDtypeStruct((n_batch, seq, 2 * WIDTH_B), _F32))
        out_specs = (y_spec,
                     pl.BlockSpec((1, seq, 2 * KV_WIDTH_A), lambda b, j: (b, 0, 0)),
                     pl.BlockSpec((1, seq, 2 * WIDTH_B), lambda b, j: (b, 0, 0)))

    scratch = [
        pltpu.VMEM((seq, D_MODEL), _BF16),
        pltpu.VMEM((n_keys, KV_WIDTH_A), _BF16),
        pltpu.VMEM((n_keys, WIDTH_B), _BF16),
        pltpu.VMEM((V_ROWS, n_keys), _BF16),
        pltpu.VMEM((WIDTH_C, seq), _BF16),
        pltpu.VMEM((WIDTH_C, seq), _BF16),
        pltpu.VMEM((D_MODEL, TOKEN_BLOCK), _BF16),
    ]
    kern = functools.partial(_layer_kernel, latent=latent, seq=seq, n_keys=n_keys, lam_init=lam_init)
    return pl.pallas_call(
        kern,
        grid=(n_batch, n_blk),
        in_specs=specs,
        out_specs=out_specs,
        out_shape=out_shape,
        scratch_shapes=scratch,
        compiler_params=pltpu.CompilerParams(
            dimension_semantics=("arbitrary", "arbitrary"),
            vmem_limit_bytes=V7X_VMEM_LIMIT_BYTES),
        name=("latent" if latent else "context") + f"_layer{layer_idx}",
    )(*args)


def _rope_tables(n_tok, dim):
    m = dim // 4
    t = np.arange(n_tok)
    inv = 1.0 / (ROPE_BASE ** (np.arange(m, dtype=np.float64) / m))
    ar = (t // GRID_W)[:, None] * inv
    ac = (t % GRID_W)[:, None] * inv
    cos = np.concatenate([np.cos(ar), np.cos(ar), np.cos(ac), np.cos(ac)], axis=1)
    sin = np.concatenate([-np.sin(ar), np.sin(ar), -np.sin(ac), np.sin(ac)], axis=1)
    first = np.tile(np.concatenate([np.ones(m), np.zeros(m)]), 2)[None, :]
    return cos.astype(np.float32), sin.astype(np.float32), first.astype(np.float32)


def _block_diag(block, n):
    out = np.zeros((block.shape[0] * n, block.shape[1] * n), np.float64)
    for i in range(n):
        out[i * block.shape[0]:(i + 1) * block.shape[0], i * block.shape[1]:(i + 1) * block.shape[1]] = block
    return out


def _dft_cos_sin(n):
    k = np.arange(n)
    ang = 2.0 * np.pi * ((k[:, None] * k[None, :]) % n) / n
    return np.cos(ang), np.sin(ang)


def _layer_consts(seq, latent):
    c64, s64 = _dft_cos_sin(GROUP_C)
    norm = 1.0 / math.sqrt(GROUP_C * seq)
    cs, ss = _dft_cos_sin(seq)
    f32c = lambda a: jnp.asarray(np.asarray(a, np.float32))
    consts = {
        "bd64": f32c(_block_diag(np.full((HEAD_DIM, HEAD_DIM), 1.0 / HEAD_DIM), N_KV_A)).astype(_BF16),
        "bd32": f32c(_block_diag(np.full((DIFF_DIM, DIFF_DIM), 1.0 / DIFF_DIM), 2 * N_HEADS_B)).astype(_BF16),
        "bc": f32c(_block_diag(c64, N_GROUPS_C) * norm).astype(_BF16),
        "bs": f32c(_block_diag(s64, N_GROUPS_C) * norm).astype(_BF16),
        "cs": f32c(cs).astype(_BF16),
        "nss": f32c(-ss).astype(_BF16),
    }
    if latent:
        cos_a, sin_a, first_a = _rope_tables(seq, HEAD_DIM)
        cos_b, sin_b, first_b = _rope_tables(seq, DIFF_DIM)
        consts.update({
            "cqa": f32c(cos_a.T), "sqa": f32c(sin_a.T), "cqb": f32c(cos_b.T), "sqb": f32c(sin_b.T),
            "cka": f32c(np.tile(cos_a, (1, N_KV_A))),
            "skpa": f32c(np.tile(sin_a * (1.0 - first_a), (1, N_KV_A))),
            "skma": f32c(np.tile(sin_a * first_a, (1, N_KV_A))),
            "ckb": f32c(np.tile(cos_b, (1, 2 * N_HEADS_B))),
            "skpb": f32c(np.tile(sin_b * (1.0 - first_b), (1, 2 * N_HEADS_B))),
            "skmb": f32c(np.tile(sin_b * first_b, (1, 2 * N_HEADS_B))),
        })
    return consts


def _layer_weights(l, norm_g, w_in, q_norm_a, k_norm_a, q_norm_b, k_norm_b,
                   lambda_q1, lambda_k1, lambda_q2, lambda_k2, subln_g, w_fourier, w_out):
    w = w_in[l]
    col = lambda i: w[:, _OFF[i]:_OFF[i + 1]]
    wft = jnp.concatenate([col(0), col(3), col(4), col(7), col(9), col(8)], axis=1).T.astype(_BF16)
    wt = jnp.concatenate([col(1), col(2), col(5), col(6)], axis=1).astype(_BF16)
    lam_init = 0.8 - 0.6 * math.exp(-0.3 * l)
    return {
        "norm_g": norm_g[l].reshape(1, D_MODEL),
        "wft": wft,
        "wt": wt,
        "wct": w_fourier[l].T.astype(_BF16),
        "woutt": w_out[l].T.astype(_BF16),
        "gqa": (q_norm_a[l] * HEAD_DIM ** -0.5).reshape(HEAD_DIM, 1),
        "gka": jnp.tile(k_norm_a[l], N_KV_A).reshape(1, KV_WIDTH_A),
        "gqb": (q_norm_b[l] * DIFF_DIM ** -0.5).reshape(DIFF_DIM, 1),
        "gkb": jnp.tile(k_norm_b[l], 2 * N_HEADS_B).reshape(1, WIDTH_B),
        "gsub": (subln_g[l] * (1.0 - lam_init)).reshape(HEAD_DIM, 1),
        "lam": jnp.stack([lambda_q1[l], lambda_k1[l], lambda_q2[l], lambda_k2[l]]),
    }


def kernel(x_prompt, x_sample, cache_attn_a, cache_attn_b, c, c_ctx, norm_g, w_mod, b_mod, w_in,
           q_norm_a, k_norm_a, q_norm_b, k_norm_b, lambda_q1, lambda_k1, lambda_q2, lambda_k2,
           subln_g, w_fourier, w_out):
    n_ctx_batch, ctx_seq, _ = x_prompt.shape
    n_lat_batch, lat_seq, _ = x_sample.shape
    past = cache_attn_a.shape[2]
    assert n_lat_batch + 1 <= MOD_ROWS

    cond = jnp.concatenate(
        [c_ctx[None, :], c, jnp.zeros((MOD_ROWS - 1 - n_lat_batch, D_MODEL), _F32)], axis=0)
    mod = _modulation(cond, w_mod, b_mod).reshape(DEPTH * MOD_ROWS, 3, D_MODEL)

    weights = [_layer_weights(l, norm_g, w_in, q_norm_a, k_norm_a, q_norm_b, k_norm_b,
                              lambda_q1, lambda_k1, lambda_q2, lambda_k2, subln_g, w_fourier, w_out)
               for l in range(DEPTH)]
    ctx_consts = _layer_consts(ctx_seq, latent=False)
    lat_consts = _layer_consts(lat_seq, latent=True)

    y_prompt = x_prompt
    new_a, new_b = [], []
    for l in range(DEPTH):
        y_prompt, kv_a, kv_b = _mixer_layer(y_prompt, mod, l, lambda b: 0, weights[l], ctx_consts)
        new_a.append(kv_a.reshape(n_ctx_batch, ctx_seq, 2, N_KV_A, HEAD_DIM))
        new_b.append(kv_b.reshape(n_ctx_batch, ctx_seq, 2, N_HEADS_B, HEAD_DIM))
    new_attn_a = jnp.stack(new_a, axis=1)
    new_attn_b = jnp.stack(new_b, axis=1)

    ctx_a = cache_attn_a.reshape(n_lat_batch, DEPTH, past, 2 * KV_WIDTH_A)
    ctx_b = cache_attn_b.reshape(n_lat_batch, DEPTH, past, 2 * WIDTH_B)
    y_sample = x_sample
    for l in range(DEPTH):
        y_sample = _mixer_layer(y_sample, mod, l, lambda b: b + 1, weights[l], lat_consts,
                                ctx=(ctx_a, ctx_b))
    return (y_prompt, y_sample, new_attn_a, new_attn_b)
```

```python
import functools
import math

import jax
import jax.numpy as jnp
import numpy as np
from jax import lax
from jax.experimental import pallas as pl
from jax.experimental.pallas import tpu as pltpu

D_MODEL = 1024
DEPTH = 2
GRID_W = 64
HEAD_DIM = 64
N_HEADS_A = 8
N_KV_A = 2
N_HEADS_B = 4
DIFF_DIM = 32
GROUP_C = 64
N_GROUPS_C = 4
WIDTH_A = N_HEADS_A * HEAD_DIM
WIDTH_B = N_HEADS_B * HEAD_DIM
WIDTH_C = N_GROUPS_C * GROUP_C
KV_WIDTH_A = N_KV_A * HEAD_DIM
D_IN = 2 * WIDTH_A + 2 * KV_WIDTH_A + 4 * WIDTH_B + 2 * WIDTH_C
RMS_EPS = 1e-6
ROPE_BASE = 10000.0
LOG2_E = math.log2(math.e)
LAM_INIT = tuple(0.8 - 0.6 * math.exp(-0.3 * l) for l in range(DEPTH))

LANES = 128
TOKEN_BLOCK = 512
MOD_ROWS = 8
WEIGHT_TILE = 256
V7X_VMEM_LIMIT_BYTES = 58 * 1024 * 1024

FEATURE_TILES = (0, 1, 3, 4, 5, 8, 10, 9)
TOKEN_TILES = (2, 6, 7)
F_QA, F_GA, F_QB, F_GB, F_GC, F_UC = 0, 512, 1024, 1280, 1536, 1792
F_STEP_ROWS = 1792
F_ROWS = 2048
T_KA, T_VA, T_KB, T_VB = 0, 128, 256, 512
T_COLS = 768
V_ROWS = KV_WIDTH_A + WIDTH_B

_BF16 = jnp.bfloat16
_F32 = jnp.float32


def _silu(x):
    return x * (1.0 / (1.0 + jnp.exp(-x)))


def _dot(a, b):
    return jnp.dot(a, b, preferred_element_type=_F32)


def _dot_nt(a, b):
    return lax.dot_general(a, b, (((1,), (1,)), ((), ())), preferred_element_type=_F32)


def _mod_kernel(c_ref, w_ref, b_ref, o_ref):
    c = c_ref[...]
    o_ref[0] = _dot(_silu(c).astype(_BF16), w_ref[0].astype(_BF16)) + b_ref[0]


def _modulation(cond, w_mod, b_mod):
    n_tile = D_MODEL
    return pl.pallas_call(
        _mod_kernel,
        grid=(DEPTH, 3 * D_MODEL // n_tile),
        in_specs=[
            pl.BlockSpec((MOD_ROWS, D_MODEL), lambda l, n: (0, 0)),
            pl.BlockSpec((1, D_MODEL, n_tile), lambda l, n: (l, 0, n)),
            pl.BlockSpec((1, 1, n_tile), lambda l, n: (l, 0, n)),
        ],
        out_specs=pl.BlockSpec((1, MOD_ROWS, n_tile), lambda l, n: (l, 0, n)),
        out_shape=jax.ShapeDtypeStruct((DEPTH, MOD_ROWS, 3 * D_MODEL), _F32),
        name="modulation",
    )(cond, w_mod, b_mod.reshape(DEPTH, 1, 3 * D_MODEL))


def _relayout_kernel(tiles_ref, w_ref, o_ref, *, transpose):
    del tiles_ref
    w = w_ref[0]
    o_ref[0] = (w.T if transpose else w).astype(_BF16)


def _relayout(w, tiles, transpose, name):
    depth, rows, _ = w.shape
    n = len(tiles)
    if transpose:
        out_shape = (depth, n * WEIGHT_TILE, rows)
        out_spec = pl.BlockSpec((1, WEIGHT_TILE, rows), lambda l, i, t: (l, i, 0))
    else:
        out_shape = (depth, rows, n * WEIGHT_TILE)
        out_spec = pl.BlockSpec((1, rows, WEIGHT_TILE), lambda l, i, t: (l, 0, i))
    return pl.pallas_call(
        functools.partial(_relayout_kernel, transpose=transpose),
        grid_spec=pltpu.PrefetchScalarGridSpec(
            num_scalar_prefetch=1,
            grid=(depth, n),
            in_specs=[pl.BlockSpec((1, rows, WEIGHT_TILE), lambda l, i, t: (l, 0, t[i]))],
            out_specs=out_spec),
        out_shape=jax.ShapeDtypeStruct(out_shape, _BF16),
        name=name,
    )(jnp.asarray(tiles, jnp.int32), w)


def _softmax_t(s_t):
    m = jnp.max(s_t, axis=0, keepdims=True)
    e = jnp.exp2(s_t - m)
    return e.astype(_BF16), jnp.sum(e, axis=0, keepdims=True)


def _rms_rows(x):
    return lax.rsqrt(jnp.mean(x * x, axis=0, keepdims=True) + RMS_EPS)


def _swap_halves(x, m):
    return jnp.concatenate([x[m:2 * m], x[0:m], x[3 * m:4 * m], x[2 * m:3 * m]], axis=0)


def _pad_rows(x, start, total):
    parts = []
    if start:
        parts.append(jnp.zeros((start, x.shape[1]), x.dtype))
    parts.append(x)
    rest = total - start - x.shape[0]
    if rest:
        parts.append(jnp.zeros((rest, x.shape[1]), x.dtype))
    return jnp.concatenate(parts, axis=0)


def _cat(parts, axis):
    return parts[0] if len(parts) == 1 else jnp.concatenate(parts, axis=axis)


def _mixer_kernel(*refs, latent, seq, nb, past):
    grp = nb * seq
    n_blk = grp // TOKEN_BLOCK
    tb = TOKEN_BLOCK // nb
    it = iter(refs)
    x_ref, mod_ref, ng_ref = next(it), next(it), next(it)
    wft_ref, wt_ref, wct_ref, woutt_ref = next(it), next(it), next(it), next(it)
    gqa_ref, gka_ref, gqb_ref, gkb_ref, gsub_ref, lam_ref = (next(it) for _ in range(6))
    bd64_ref, bd32_ref, bc_ref, bs_ref, cs_ref, nss_ref = (next(it) for _ in range(6))
    if latent:
        ctxa_ref, ctxb_ref = next(it), next(it)
        cqa_ref, sqa_ref, cqb_ref, sqb_ref = (next(it) for _ in range(4))
        cka_ref, skpa_ref, skma_ref, ckb_ref, skpb_ref, skmb_ref = (next(it) for _ in range(6))
        y_ref = next(it)
    else:
        y_ref, newa_ref, newb_ref = next(it), next(it), next(it)
    xres_s, h_s, ka_s, kb_s, vt_s, t1_s, t2_s, mixt_s = (next(it) for _ in range(8))

    l = pl.program_id(1)
    j = pl.program_id(2)
    shift = mod_ref[0, 0:1, :]
    scale = mod_ref[0, 1:2, :]
    gate = mod_ref[0, 2:3, :]

    @pl.when((l == 0) & (j == 0))
    def _():
        xres_s[...] = x_ref[...].reshape(grp, D_MODEL)

    def rope_k(k, cos_ref, sp_ref, sm_ref, rows, m):
        out = []
        for c0 in range(0, k.shape[1], LANES):
            kc = k[:, c0:c0 + LANES]
            out.append(kc * cos_ref[rows, :] + pltpu.roll(kc, m, 1) * sp_ref[rows, :]
                       + pltpu.roll(kc, LANES - m, 1) * sm_ref[rows, :])
        return _cat(out, 1)

    @pl.when(j == 0)
    def _prep():
        for c in range(n_blk):
            rows = slice(c * TOKEN_BLOCK, (c + 1) * TOKEN_BLOCK)
            xc = xres_s[rows, :]
            ms = jnp.mean(xc * xc, axis=-1, keepdims=True)
            hc = (xc * lax.rsqrt(ms + RMS_EPS) * ng_ref[l] * (1.0 + scale) + shift).astype(_BF16)
            h_s[rows, :] = hc
            pt = _dot(hc, wt_ref[l])
            ka = pt[:, T_KA:T_KA + KV_WIDTH_A]
            va = pt[:, T_VA:T_VA + KV_WIDTH_A]
            kb = pt[:, T_KB:T_KB + WIDTH_B]
            vb = pt[:, T_VB:T_VB + WIDTH_B]
            ka = ka * lax.rsqrt(_dot((ka * ka).astype(_BF16), bd64_ref[...]) + RMS_EPS) * gka_ref[l]
            kb = kb * lax.rsqrt(_dot((kb * kb).astype(_BF16), bd32_ref[...]) + RMS_EPS) * gkb_ref[l]
            if latent:
                ka = rope_k(ka, cka_ref, skpa_ref, skma_ref, rows, HEAD_DIM // 4)
                kb = rope_k(kb, ckb_ref, skpb_ref, skmb_ref, rows, DIFF_DIM // 4)
            else:
                for i in range(nb):
                    r = slice(i * seq, (i + 1) * seq)
                    newa_ref[i, 0, :, 0:KV_WIDTH_A] = ka[r]
                    newa_ref[i, 0, :, KV_WIDTH_A:2 * KV_WIDTH_A] = va[r]
                    newb_ref[i, 0, :, 0:WIDTH_B] = kb[r]
                    newb_ref[i, 0, :, WIDTH_B:2 * WIDTH_B] = vb[r]
            ka_s[rows, :] = ka.astype(_BF16)
            kb_s[rows, :] = kb.astype(_BF16)
            vt_s[0:KV_WIDTH_A, rows] = va.T.astype(_BF16)
            vt_s[KV_WIDTH_A:V_ROWS, rows] = vb.T.astype(_BF16)
            uct = _dot_nt(wft_ref[l, F_UC:F_ROWS, :], hc).astype(_BF16)
            t1_s[:, rows] = _dot(bc_ref[...], uct).astype(_BF16)
            t2_s[:, rows] = _dot(bs_ref[...], uct).astype(_BF16)
        if latent:
            ca = ctxa_ref[0, 0]
            cb = ctxb_ref[0, 0]
            ka_s[grp:grp + past, :] = ca[:, 0:KV_WIDTH_A].astype(_BF16)
            kb_s[grp:grp + past, :] = cb[:, 0:WIDTH_B].astype(_BF16)
            vt_s[0:KV_WIDTH_A, grp:grp + past] = ca[:, KV_WIDTH_A:].T.astype(_BF16)
            vt_s[KV_WIDTH_A:V_ROWS, grp:grp + past] = cb[:, WIDTH_B:].T.astype(_BF16)

    blk_rows = pl.ds(pl.multiple_of(j * TOKEN_BLOCK, TOKEN_BLOCK), TOKEN_BLOCK)
    pf = _dot_nt(wft_ref[l, 0:F_STEP_ROWS, :], h_s[blk_rows, :])

    lv = lam_ref[l]
    lam_init = jnp.where(l == 0, LAM_INIT[0], LAM_INIT[1])
    lam = (jnp.exp(jnp.sum(lv[0:1] * lv[1:2], axis=-1, keepdims=True))
           - jnp.exp(jnp.sum(lv[2:3] * lv[3:4], axis=-1, keepdims=True)) + lam_init)

    def q_tile(r0, dim, gain_ref, cos_ref, sin_ref, cols, pad_start, pad_total):
        q = pf[r0:r0 + dim, cols]
        q = q * _rms_rows(q) * gain_ref[l]
        if latent:
            q = q * cos_ref[...] + _swap_halves(q, dim // 4) * sin_ref[...]
        return _pad_rows(q.astype(_BF16), pad_start, pad_total)

    def q_a(h, cols):
        g = h // (N_HEADS_A // N_KV_A)
        return q_tile(F_QA + h * HEAD_DIM, HEAD_DIM, gqa_ref, cqa_ref if latent else None,
                      sqa_ref if latent else None, cols, g * HEAD_DIM, KV_WIDTH_A)

    def q_b(h, comp, cols):
        r0 = h * HEAD_DIM + comp * DIFF_DIM
        return q_tile(F_QB + r0, DIFF_DIM, gqb_ref, cqb_ref if latent else None,
                      sqb_ref if latent else None, cols, r0, WIDTH_B)

    def store_a(h, cols, o):
        ga = pf[F_GA + h * HEAD_DIM:F_GA + (h + 1) * HEAD_DIM, cols]
        mixt_s[h * HEAD_DIM:(h + 1) * HEAD_DIM, cols] = (o * _silu(ga)).astype(_BF16)

    def store_b(h, cols, o1, o2):
        ob = o1 - lam * o2
        ob = ob * _rms_rows(ob) * gsub_ref[l]
        gb = pf[F_GB + h * HEAD_DIM:F_GB + (h + 1) * HEAD_DIM, cols]
        mixt_s[WIDTH_A + h * HEAD_DIM:WIDTH_A + (h + 1) * HEAD_DIM, cols] = (ob * _silu(gb)).astype(_BF16)

    units = []
    all_cols = slice(0, TOKEN_BLOCK)
    if nb == 1:
        keys = slice(0, grp + past)
        for h in range(N_HEADS_A):
            g = h // (N_HEADS_A // N_KV_A)
            units.append(("a", keys, slice(g * HEAD_DIM, (g + 1) * HEAD_DIM),
                          [functools.partial(q_a, h, all_cols)],
                          functools.partial(store_a, h, all_cols)))
        for h in range(N_HEADS_B):
            v_rows = slice(KV_WIDTH_A + h * HEAD_DIM, KV_WIDTH_A + (h + 1) * HEAD_DIM)
            held = {}
            for comp in range(2):
                def fin(o, h=h, comp=comp, held=held):
                    held[comp] = o
                    if comp == 1:
                        store_b(h, all_cols, held[0], held[1])
                units.append(("b", keys, v_rows, [functools.partial(q_b, h, comp, all_cols)], fin))
    else:
        for i in range(nb):
            keys = slice(i * seq, (i + 1) * seq)
            cols = slice(i * tb, (i + 1) * tb)
            for u in range(N_HEADS_A // 2):
                g = (2 * u) // (N_HEADS_A // N_KV_A)
                def fin(o, u=u, cols=cols):
                    store_a(2 * u, cols, o[:, 0:tb])
                    store_a(2 * u + 1, cols, o[:, tb:2 * tb])
                units.append(("a", keys, slice(g * HEAD_DIM, (g + 1) * HEAD_DIM),
                              [functools.partial(q_a, 2 * u, cols), functools.partial(q_a, 2 * u + 1, cols)], fin))
            for h in range(N_HEADS_B):
                def fin(o, h=h, cols=cols):
                    store_b(h, cols, o[:, 0:tb], o[:, tb:2 * tb])
                units.append(("b", keys, slice(KV_WIDTH_A + h * HEAD_DIM, KV_WIDTH_A + (h + 1) * HEAD_DIM),
                              [functools.partial(q_b, h, 0, cols), functools.partial(q_b, h, 1, cols)], fin))

    def scores(unit):
        kind, keys, _, tiles, _ = unit
        k_ref = ka_s if kind == "a" else kb_s
        return _dot(k_ref[keys, :], _cat([t() for t in tiles], 1))

    s_next = scores(units[0])
    for u, unit in enumerate(units):
        s_cur = s_next
        if u + 1 < len(units):
            s_next = scores(units[u + 1])
        e, den = _softmax_t(s_cur)
        unit[4](_dot(vt_s[unit[2], unit[1]], e) / den)

    if nb == 1:
        ft = _dot(t1_s[...], cs_ref[...]) + _dot(t2_s[...], nss_ref[...])
    else:
        ft = _cat([_dot(t1_s[:, i * seq:(i + 1) * seq], cs_ref[...])
                   + _dot(t2_s[:, i * seq:(i + 1) * seq], nss_ref[...]) for i in range(nb)], 1)
    oc = _dot(wct_ref[l], ft.astype(_BF16))
    gc = pf[F_GC:F_GC + WIDTH_C, :]
    mixt_s[WIDTH_A + WIDTH_B:, :] = (oc * _silu(gc)).astype(_BF16)

    yt = _dot(woutt_ref[l], mixt_s[...])
    y = xres_s[blk_rows, :] + gate * yt.T
    xres_s[blk_rows, :] = y

    @pl.when(l == DEPTH - 1)
    def _():
        y_ref[...] = y.reshape(y_ref.shape)


def _const_spec(shape):
    nd = len(shape)
    return pl.BlockSpec(shape, lambda b, l, j: (0,) * nd, pipeline_mode=pl.Buffered(1))


def _mixer(x, mod, mod_row, wts, consts, nb, ctx=None):
    latent = ctx is not None
    n_batch, seq, _ = x.shape
    grp = nb * seq
    n_blk = grp // TOKEN_BLOCK
    assert grp % TOKEN_BLOCK == 0 and (nb == 1 or grp == TOKEN_BLOCK) and n_batch % nb == 0
    past = ctx[0].shape[2] if latent else 0
    last = DEPTH - 1

    args = [x, mod]
    specs = [
        pl.BlockSpec((nb, seq, D_MODEL), lambda b, l, j: (b, 0, 0),
                     pipeline_mode=pl.Buffered(1) if latent else None),
        pl.BlockSpec((1, 3, D_MODEL), lambda b, l, j: (l * MOD_ROWS + mod_row(b), 0, 0)),
    ]
    for name in ("norm_g", "wft", "wt", "wct", "woutt", "gqa", "gka", "gqb", "gkb", "gsub", "lam"):
        args.append(wts[name])
        specs.append(_const_spec(wts[name].shape))
    for name in ("bd64", "bd32", "bc", "bs"):
        args.append(consts[name])
        specs.append(_const_spec(consts[name].shape))
    dft_cols = TOKEN_BLOCK if nb == 1 else seq
    for name in ("cs", "nss"):
        args.append(consts[name])
        specs.append(pl.BlockSpec((seq, dft_cols), lambda b, l, j: (0, j)))
    if latent:
        args += [ctx[0], ctx[1]]
        specs += [
            pl.BlockSpec((1, 1, past, 2 * KV_WIDTH_A), lambda b, l, j: (b, l, 0, 0), pipeline_mode=pl.Buffered(1)),
            pl.BlockSpec((1, 1, past, 2 * WIDTH_B), lambda b, l, j: (b, l, 0, 0), pipeline_mode=pl.Buffered(1)),
        ]
        for name, rows in (("cqa", HEAD_DIM), ("sqa", HEAD_DIM), ("cqb", DIFF_DIM), ("sqb", DIFF_DIM)):
            args.append(consts[name])
            specs.append(pl.BlockSpec((rows, TOKEN_BLOCK), lambda b, l, j: (0, j)))
        for name in ("cka", "skpa", "skma", "ckb", "skpb", "skmb"):
            args.append(consts[name])
            specs.append(_const_spec(consts[name].shape))

    y_shape = jax.ShapeDtypeStruct(x.shape, _F32)
    if nb == 1:
        y_spec = pl.BlockSpec((1, TOKEN_BLOCK, D_MODEL), lambda b, l, j: (b, jnp.where(l == last, j, 0), 0))
    else:
        y_spec = pl.BlockSpec((nb, seq, D_MODEL), lambda b, l, j: (b, 0, 0))
    if latent:
        out_shape, out_specs = y_shape, y_spec
    else:
        out_shape = (y_shape,
                     jax.ShapeDtypeStruct((n_batch, DEPTH, seq, 2 * KV_WIDTH_A), _F32),
                     jax.ShapeDtypeStruct((n_batch, DEPTH, seq, 2 * WIDTH_B), _F32))
        out_specs = (y_spec,
                     pl.BlockSpec((nb, 1, seq, 2 * KV_WIDTH_A), lambda b, l, j: (b, l, 0, 0)),
                     pl.BlockSpec((nb, 1, seq, 2 * WIDTH_B), lambda b, l, j: (b, l, 0, 0)))

    n_keys = grp + past
    scratch = [
        pltpu.VMEM((grp, D_MODEL), _F32),
        pltpu.VMEM((grp, D_MODEL), _BF16),
        pltpu.VMEM((n_keys, KV_WIDTH_A), _BF16),
        pltpu.VMEM((n_keys, WIDTH_B), _BF16),
        pltpu.VMEM((V_ROWS, n_keys), _BF16),
        pltpu.VMEM((WIDTH_C, grp), _BF16),
        pltpu.VMEM((WIDTH_C, grp), _BF16),
        pltpu.VMEM((D_MODEL, TOKEN_BLOCK), _BF16),
    ]
    kern = functools.partial(_mixer_kernel, latent=latent, seq=seq, nb=nb, past=past)
    return pl.pallas_call(
        kern,
        grid=(n_batch // nb, DEPTH, n_blk),
        in_specs=specs,
        out_specs=out_specs,
        out_shape=out_shape,
        scratch_shapes=scratch,
        compiler_params=pltpu.CompilerParams(
            dimension_semantics=("arbitrary", "arbitrary", "arbitrary"),
            vmem_limit_bytes=V7X_VMEM_LIMIT_BYTES),
        name="latent_mixer" if latent else "context_mixer",
    )(*args)


def _rope_tables(n_tok, dim):
    m = dim // 4
    t = np.arange(n_tok)
    inv = 1.0 / (ROPE_BASE ** (np.arange(m, dtype=np.float64) / m))
    ar = (t // GRID_W)[:, None] * inv
    ac = (t % GRID_W)[:, None] * inv
    cos = np.concatenate([np.cos(ar), np.cos(ar), np.cos(ac), np.cos(ac)], axis=1)
    sin = np.concatenate([-np.sin(ar), np.sin(ar), -np.sin(ac), np.sin(ac)], axis=1)
    first = np.tile(np.concatenate([np.ones(m), np.zeros(m)]), 2)[None, :]
    return cos.astype(np.float32), sin.astype(np.float32), first.astype(np.float32)


def _block_diag(block, n):
    out = np.zeros((block.shape[0] * n, block.shape[1] * n), np.float64)
    for i in range(n):
        out[i * block.shape[0]:(i + 1) * block.shape[0], i * block.shape[1]:(i + 1) * block.shape[1]] = block
    return out


def _dft_cos_sin(n):
    k = np.arange(n)
    ang = 2.0 * np.pi * ((k[:, None] * k[None, :]) % n) / n
    return np.cos(ang), np.sin(ang)


def _mixer_consts(seq, latent):
    c64, s64 = _dft_cos_sin(GROUP_C)
    norm = 1.0 / math.sqrt(GROUP_C * seq)
    cs, ss = _dft_cos_sin(seq)
    f32c = lambda a: jnp.asarray(np.asarray(a, np.float32))
    consts = {
        "bd64": f32c(_block_diag(np.full((HEAD_DIM, HEAD_DIM), 1.0 / HEAD_DIM), N_KV_A)).astype(_BF16),
        "bd32": f32c(_block_diag(np.full((DIFF_DIM, DIFF_DIM), 1.0 / DIFF_DIM), 2 * N_HEADS_B)).astype(_BF16),
        "bc": f32c(_block_diag(c64, N_GROUPS_C) * norm).astype(_BF16),
        "bs": f32c(_block_diag(s64, N_GROUPS_C) * norm).astype(_BF16),
        "cs": f32c(cs).astype(_BF16),
        "nss": f32c(-ss).astype(_BF16),
    }
    if latent:
        cos_a, sin_a, first_a = _rope_tables(seq, HEAD_DIM)
        cos_b, sin_b, first_b = _rope_tables(seq, DIFF_DIM)
        rep_a, rep_b = LANES // HEAD_DIM, LANES // DIFF_DIM
        consts.update({
            "cqa": f32c(cos_a.T), "sqa": f32c(sin_a.T), "cqb": f32c(cos_b.T), "sqb": f32c(sin_b.T),
            "cka": f32c(np.tile(cos_a, (1, rep_a))),
            "skpa": f32c(np.tile(sin_a * (1.0 - first_a), (1, rep_a))),
            "skma": f32c(np.tile(sin_a * first_a, (1, rep_a))),
            "ckb": f32c(np.tile(cos_b, (1, rep_b))),
            "skpb": f32c(np.tile(sin_b * (1.0 - first_b), (1, rep_b))),
            "skmb": f32c(np.tile(sin_b * first_b, (1, rep_b))),
        })
    return consts


def _mixer_weights(norm_g, w_in, q_norm_a, k_norm_a, q_norm_b, k_norm_b,
                   lambda_q1, lambda_k1, lambda_q2, lambda_k2, subln_g, w_fourier, w_out):
    keep = jnp.asarray([1.0 - v for v in LAM_INIT], _F32)
    return {
        "norm_g": norm_g.reshape(DEPTH, 1, D_MODEL),
        "wft": _relayout(w_in, FEATURE_TILES, True, "w_in_feature_major"),
        "wt": _relayout(w_in, TOKEN_TILES, False, "w_in_token_major"),
        "wct": _relayout(w_fourier, (0,), True, "w_fourier_t"),
        "woutt": _relayout(w_out, tuple(range(D_MODEL // WEIGHT_TILE)), True, "w_out_t"),
        "gqa": (q_norm_a * (HEAD_DIM ** -0.5 * LOG2_E)).reshape(DEPTH, HEAD_DIM, 1),
        "gka": jnp.tile(k_norm_a, (1, N_KV_A)).reshape(DEPTH, 1, KV_WIDTH_A),
        "gqb": (q_norm_b * (DIFF_DIM ** -0.5 * LOG2_E)).reshape(DEPTH, DIFF_DIM, 1),
        "gkb": jnp.tile(k_norm_b, (1, 2 * N_HEADS_B)).reshape(DEPTH, 1, WIDTH_B),
        "gsub": (subln_g * keep[:, None]).reshape(DEPTH, HEAD_DIM, 1),
        "lam": jnp.stack([lambda_q1, lambda_k1, lambda_q2, lambda_k2], axis=1),
    }


def kernel(x_prompt, x_sample, cache_attn_a, cache_attn_b, c, c_ctx, norm_g, w_mod, b_mod, w_in,
           q_norm_a, k_norm_a, q_norm_b, k_norm_b, lambda_q1, lambda_k1, lambda_q2, lambda_k2,
           subln_g, w_fourier, w_out):
    n_ctx_batch, ctx_seq, _ = x_prompt.shape
    n_lat_batch, lat_seq, _ = x_sample.shape
    past = cache_attn_a.shape[2]
    assert n_lat_batch + 1 <= MOD_ROWS and w_in.shape == (DEPTH, D_MODEL, D_IN)

    cond = jnp.concatenate(
        [c_ctx[None, :], c, jnp.zeros((MOD_ROWS - 1 - n_lat_batch, D_MODEL), _F32)], axis=0)
    mod = _modulation(cond, w_mod, b_mod).reshape(DEPTH * MOD_ROWS, 3, D_MODEL)

    weights = _mixer_weights(norm_g, w_in, q_norm_a, k_norm_a, q_norm_b, k_norm_b,
                             lambda_q1, lambda_k1, lambda_q2, lambda_k2, subln_g, w_fourier, w_out)

    y_prompt, new_a, new_b = _mixer(x_prompt, mod, lambda b: 0, weights,
                                    _mixer_consts(ctx_seq, latent=False), nb=TOKEN_BLOCK // ctx_seq)
    new_attn_a = new_a.reshape(n_ctx_batch, DEPTH, ctx_seq, 2, N_KV_A, HEAD_DIM)
    new_attn_b = new_b.reshape(n_ctx_batch, DEPTH, ctx_seq, 2, N_HEADS_B, HEAD_DIM)

    ctx_a = cache_attn_a.reshape(n_lat_batch, DEPTH, past, 2 * KV_WIDTH_A)
    ctx_b = cache_attn_b.reshape(n_lat_batch, DEPTH, past, 2 * WIDTH_B)
    y_sample = _mixer(x_sample, mod, lambda b: b + 1, weights,
                      _mixer_consts(lat_seq, latent=True), nb=1, ctx=(ctx_a, ctx_b))
    return (y_prompt, y_sample, new_attn_a, new_attn_b)
```

```python
import functools
import math

import jax
import jax.numpy as jnp
import numpy as np
from jax import lax
from jax.experimental import pallas as pl
from jax.experimental.pallas import tpu as pltpu

D_MODEL = 1024
DEPTH = 2
GRID_W = 64
HEAD_DIM = 64
N_HEADS_A = 8
N_KV_A = 2
N_HEADS_B = 4
DIFF_DIM = 32
GROUP_C = 64
N_GROUPS_C = 4
WIDTH_A = N_HEADS_A * HEAD_DIM
WIDTH_B = N_HEADS_B * HEAD_DIM
WIDTH_C = N_GROUPS_C * GROUP_C
KV_WIDTH_A = N_KV_A * HEAD_DIM
D_IN = 2 * WIDTH_A + 2 * KV_WIDTH_A + 4 * WIDTH_B + 2 * WIDTH_C
RMS_EPS = 1e-6
ROPE_BASE = 10000.0
LOG2_E = math.log2(math.e)
LAM_INIT = tuple(0.8 - 0.6 * math.exp(-0.3 * l) for l in range(DEPTH))

LANES = 128
TOKEN_BLOCK = 512
MOD_ROWS = 8
WEIGHT_TILE = 256
V7X_VMEM_LIMIT_BYTES = 58 * 1024 * 1024

FEATURE_TILES = (0, 1, 3, 4, 5, 8, 10, 9)
TOKEN_TILES = (2, 6, 7)
F_QA, F_GA, F_QB, F_GB, F_GC, F_UC = 0, 512, 1024, 1280, 1536, 1792
F_STEP_ROWS = 1792
F_ROWS = 2048
T_KA, T_VA, T_KB, T_VB = 0, 128, 256, 512
T_COLS = 768
ONES_ROWS = 16
V_GROUP_ROWS = HEAD_DIM + ONES_ROWS
N_V_GROUPS = N_KV_A + N_HEADS_B

_BF16 = jnp.bfloat16
_F32 = jnp.float32


def _silu(x):
    return x * (1.0 / (1.0 + jnp.exp(-x)))


def _dot(a, b):
    return jnp.dot(a, b, preferred_element_type=_F32)


def _dot_nt(a, b):
    return lax.dot_general(a, b, (((1,), (1,)), ((), ())), preferred_element_type=_F32)


def _mod_kernel(c_ref, w_ref, b_ref, o_ref):
    c = c_ref[...]
    o_ref[0] = _dot(_silu(c).astype(_BF16), w_ref[0].astype(_BF16)) + b_ref[0]


def _modulation(cond, w_mod, b_mod):
    n_tile = D_MODEL
    return pl.pallas_call(
        _mod_kernel,
        grid=(DEPTH, 3 * D_MODEL // n_tile),
        in_specs=[
            pl.BlockSpec((MOD_ROWS, D_MODEL), lambda l, n: (0, 0)),
            pl.BlockSpec((1, D_MODEL, n_tile), lambda l, n: (l, 0, n)),
            pl.BlockSpec((1, 1, n_tile), lambda l, n: (l, 0, n)),
        ],
        out_specs=pl.BlockSpec((1, MOD_ROWS, n_tile), lambda l, n: (l, 0, n)),
        out_shape=jax.ShapeDtypeStruct((DEPTH, MOD_ROWS, 3 * D_MODEL), _F32),
        name="modulation",
    )(cond, w_mod, b_mod.reshape(DEPTH, 1, 3 * D_MODEL))


def _relayout_kernel(tiles_ref, w_ref, o_ref, *, transpose):
    del tiles_ref
    w = w_ref[0]
    o_ref[0] = (w.T if transpose else w).astype(_BF16)


def _relayout(w, tiles, transpose, name):
    depth, rows, _ = w.shape
    n = len(tiles)
    if transpose:
        out_shape = (depth, n * WEIGHT_TILE, rows)
        out_spec = pl.BlockSpec((1, WEIGHT_TILE, rows), lambda l, i, t: (l, i, 0))
    else:
        out_shape = (depth, rows, n * WEIGHT_TILE)
        out_spec = pl.BlockSpec((1, rows, WEIGHT_TILE), lambda l, i, t: (l, 0, i))
    return pl.pallas_call(
        functools.partial(_relayout_kernel, transpose=transpose),
        grid_spec=pltpu.PrefetchScalarGridSpec(
            num_scalar_prefetch=1,
            grid=(depth, n),
            in_specs=[pl.BlockSpec((1, rows, WEIGHT_TILE), lambda l, i, t: (l, 0, t[i]))],
            out_specs=out_spec),
        out_shape=jax.ShapeDtypeStruct(out_shape, _BF16),
        name=name,
    )(jnp.asarray(tiles, jnp.int32), w)


def _softmax_t(s_t):
    m = jnp.max(s_t, axis=0, keepdims=True)
    return jnp.exp2(s_t - m).astype(_BF16)


def _rms_rows(x):
    return lax.rsqrt(jnp.mean(x * x, axis=0, keepdims=True) + RMS_EPS)


def _swap_halves(x, m):
    return jnp.concatenate([x[m:2 * m], x[0:m], x[3 * m:4 * m], x[2 * m:3 * m]], axis=0)


def _pad_rows(x, start, total):
    parts = []
    if start:
        parts.append(jnp.zeros((start, x.shape[1]), x.dtype))
    parts.append(x)
    rest = total - start - x.shape[0]
    if rest:
        parts.append(jnp.zeros((rest, x.shape[1]), x.dtype))
    return jnp.concatenate(parts, axis=0)


def _cat(parts, axis):
    return parts[0] if len(parts) == 1 else jnp.concatenate(parts, axis=axis)


def _mixer_kernel(*refs, latent, seq, nb, past):
    grp = nb * seq
    n_blk = grp // TOKEN_BLOCK
    tb = TOKEN_BLOCK // nb
    it = iter(refs)
    x_ref, mod_ref, ng_ref = next(it), next(it), next(it)
    wft_ref, wt_ref, wct_ref, wout_ref = next(it), next(it), next(it), next(it)
    gqa_ref, gka_ref, gqb_ref, gkb_ref, gsub_ref, lam_ref = (next(it) for _ in range(6))
    bd64_ref, bd32_ref, bc_ref, bs_ref, cs_ref, nss_ref = (next(it) for _ in range(6))
    if latent:
        ctxa_ref, ctxb_ref = next(it), next(it)
        cqa_ref, sqa_ref, cqb_ref, sqb_ref = (next(it) for _ in range(4))
        cka_ref, skpa_ref, skma_ref, ckb_ref, skpb_ref, skmb_ref = (next(it) for _ in range(6))
        y_ref = next(it)
    else:
        y_ref, newa_ref, newb_ref = next(it), next(it), next(it)
    xres_s, h_s, ka_s, kb_s, vt_s, t1_s, t2_s, mix_s = (next(it) for _ in range(8))

    l = pl.program_id(1)
    j = pl.program_id(2)
    shift = mod_ref[0, 0:1, :]
    scale = mod_ref[0, 1:2, :]
    gate = mod_ref[0, 2:3, :]

    @pl.when((l == 0) & (j == 0))
    def _():
        xres_s[...] = x_ref[...].reshape(grp, D_MODEL)
        for g in range(N_V_GROUPS):
            vt_s[g * V_GROUP_ROWS + HEAD_DIM:(g + 1) * V_GROUP_ROWS, :] = jnp.ones(
                (ONES_ROWS, grp + past), _BF16)

    def rope_k(k, cos_ref, sp_ref, sm_ref, rows, m):
        out = []
        for c0 in range(0, k.shape[1], LANES):
            kc = k[:, c0:c0 + LANES]
            out.append(kc * cos_ref[rows, :] + pltpu.roll(kc, m, 1) * sp_ref[rows, :]
                       + pltpu.roll(kc, LANES - m, 1) * sm_ref[rows, :])
        return _cat(out, 1)

    def put_values(vt, cols):
        for g in range(N_V_GROUPS):
            vt_s[g * V_GROUP_ROWS:g * V_GROUP_ROWS + HEAD_DIM, cols] = vt[g * HEAD_DIM:(g + 1) * HEAD_DIM]

    @pl.when(j == 0)
    def _prep():
        for c in range(n_blk):
            rows = slice(c * TOKEN_BLOCK, (c + 1) * TOKEN_BLOCK)
            xc = xres_s[rows, :]
            ms = jnp.mean(xc * xc, axis=-1, keepdims=True)
            hc = (xc * lax.rsqrt(ms + RMS_EPS) * ng_ref[l] * (1.0 + scale) + shift).astype(_BF16)
            h_s[rows, :] = hc
            pt = _dot(hc, wt_ref[l])
            ka = pt[:, T_KA:T_KA + KV_WIDTH_A]
            va = pt[:, T_VA:T_VA + KV_WIDTH_A]
            kb = pt[:, T_KB:T_KB + WIDTH_B]
            vb = pt[:, T_VB:T_VB + WIDTH_B]
            ka = ka * lax.rsqrt(_dot((ka * ka).astype(_BF16), bd64_ref[...]) + RMS_EPS) * gka_ref[l]
            kb = kb * lax.rsqrt(_dot((kb * kb).astype(_BF16), bd32_ref[...]) + RMS_EPS) * gkb_ref[l]
            if latent:
                ka = rope_k(ka, cka_ref, skpa_ref, skma_ref, rows, HEAD_DIM // 4)
                kb = rope_k(kb, ckb_ref, skpb_ref, skmb_ref, rows, DIFF_DIM // 4)
            else:
                for i in range(nb):
                    r = slice(i * seq, (i + 1) * seq)
                    newa_ref[i, 0, :, 0:KV_WIDTH_A] = ka[r]
                    newa_ref[i, 0, :, KV_WIDTH_A:2 * KV_WIDTH_A] = va[r]
                    newb_ref[i, 0, :, 0:WIDTH_B] = kb[r]
                    newb_ref[i, 0, :, WIDTH_B:2 * WIDTH_B] = vb[r]
            ka_s[rows, :] = ka.astype(_BF16)
            kb_s[rows, :] = kb.astype(_BF16)
            put_values(jnp.concatenate([va.T, vb.T], axis=0).astype(_BF16), rows)
            uct = _dot_nt(wft_ref[l, F_UC:F_ROWS, :], hc).astype(_BF16)
            t1_s[:, rows] = _dot(bc_ref[...], uct).astype(_BF16)
            t2_s[:, rows] = _dot(bs_ref[...], uct).astype(_BF16)
        if latent:
            ca = ctxa_ref[0, 0]
            cb = ctxb_ref[0, 0]
            ka_s[grp:grp + past, :] = ca[:, 0:KV_WIDTH_A].astype(_BF16)
            kb_s[grp:grp + past, :] = cb[:, 0:WIDTH_B].astype(_BF16)
            put_values(jnp.concatenate([ca[:, KV_WIDTH_A:].T, cb[:, WIDTH_B:].T], axis=0).astype(_BF16),
                       slice(grp, grp + past))

    blk_rows = pl.ds(pl.multiple_of(j * TOKEN_BLOCK, TOKEN_BLOCK), TOKEN_BLOCK)
    pf = _dot_nt(wft_ref[l, 0:F_STEP_ROWS, :], h_s[blk_rows, :])

    lv = lam_ref[l]
    lam_init = jnp.where(l == 0, LAM_INIT[0], LAM_INIT[1])
    lam = (jnp.exp(jnp.sum(lv[0:1] * lv[1:2], axis=-1, keepdims=True))
           - jnp.exp(jnp.sum(lv[2:3] * lv[3:4], axis=-1, keepdims=True)) + lam_init)

    def q_tile(r0, dim, gain_ref, cos_ref, sin_ref, cols, pad_start, pad_total):
        q = pf[r0:r0 + dim, cols]
        q = q * _rms_rows(q) * gain_ref[l]
        if latent:
            q = q * cos_ref[...] + _swap_halves(q, dim // 4) * sin_ref[...]
        return _pad_rows(q.astype(_BF16), pad_start, pad_total)

    def q_a(h, cols):
        g = h // (N_HEADS_A // N_KV_A)
        return q_tile(F_QA + h * HEAD_DIM, HEAD_DIM, gqa_ref, cqa_ref if latent else None,
                      sqa_ref if latent else None, cols, g * HEAD_DIM, KV_WIDTH_A)

    def q_b(h, comp, cols):
        r0 = h * HEAD_DIM + comp * DIFF_DIM
        return q_tile(F_QB + r0, DIFF_DIM, gqb_ref, cqb_ref if latent else None,
                      sqb_ref if latent else None, cols, r0, WIDTH_B)

    def put_mix(gated_t, cols, c0):
        mix_s[cols, c0:c0 + gated_t.shape[0]] = gated_t.T.astype(_BF16)

    half = {}

    def put_head(slot, cols, gated_t):
        if slot % 2 == 0:
            half[cols.start] = gated_t
        else:
            put_mix(jnp.concatenate([half.pop(cols.start), gated_t], axis=0), cols, (slot - 1) * HEAD_DIM)

    def store_a(h, cols, o):
        ga = pf[F_GA + h * HEAD_DIM:F_GA + (h + 1) * HEAD_DIM, cols]
        put_head(h, cols, o * _silu(ga))

    def store_b(h, cols, o1, o2):
        ob = o1 - lam * o2
        ob = ob * _rms_rows(ob) * gsub_ref[l]
        gb = pf[F_GB + h * HEAD_DIM:F_GB + (h + 1) * HEAD_DIM, cols]
        put_head(N_HEADS_A + h, cols, ob * _silu(gb))

    def v_group(g):
        return slice(g * V_GROUP_ROWS, (g + 1) * V_GROUP_ROWS)

    units = []
    all_cols = slice(0, TOKEN_BLOCK)
    if nb == 1:
        keys = slice(0, grp + past)
        for h in range(N_HEADS_A):
            g = h // (N_HEADS_A // N_KV_A)
            units.append(("a", keys, v_group(g),
                          [functools.partial(q_a, h, all_cols)],
                          functools.partial(store_a, h, all_cols)))
        for h in range(N_HEADS_B):
            v_rows = v_group(N_KV_A + h)
            held = {}
            for comp in range(2):
                def fin(o, h=h, comp=comp, held=held):
                    held[comp] = o
                    if comp == 1:
                        store_b(h, all_cols, held[0], held[1])
                units.append(("b", keys, v_rows, [functools.partial(q_b, h, comp, all_cols)], fin))
    else:
        for i in range(nb):
            keys = slice(i * seq, (i + 1) * seq)
            cols = slice(i * tb, (i + 1) * tb)
            for u in range(N_HEADS_A // 2):
                g = (2 * u) // (N_HEADS_A // N_KV_A)
                def fin(o, u=u, cols=cols):
                    store_a(2 * u, cols, o[:, 0:tb])
                    store_a(2 * u + 1, cols, o[:, tb:2 * tb])
                units.append(("a", keys, v_group(g),
                              [functools.partial(q_a, 2 * u, cols), functools.partial(q_a, 2 * u + 1, cols)], fin))
            for h in range(N_HEADS_B):
                def fin(o, h=h, cols=cols):
                    store_b(h, cols, o[:, 0:tb], o[:, tb:2 * tb])
                units.append(("b", keys, v_group(N_KV_A + h),
                              [functools.partial(q_b, h, 0, cols), functools.partial(q_b, h, 1, cols)], fin))

    def scores(unit):
        kind, keys, _, tiles, _ = unit
        k_ref = ka_s if kind == "a" else kb_s
        return _dot(k_ref[keys, :], _cat([t() for t in tiles], 1))

    if nb == 1:
        ft = _dot(t1_s[...], cs_ref[...]) + _dot(t2_s[...], nss_ref[...])
    else:
        ft = _cat([_dot(t1_s[:, i * seq:(i + 1) * seq], cs_ref[...])
                   + _dot(t2_s[:, i * seq:(i + 1) * seq], nss_ref[...]) for i in range(nb)], 1)
    oc = _dot(wct_ref[l], ft.astype(_BF16))
    put_mix(oc * _silu(pf[F_GC:F_GC + WIDTH_C, :]), all_cols, WIDTH_A + WIDTH_B)

    s_next = scores(units[0])
    for u, unit in enumerate(units):
        s_cur = s_next
        if u + 1 < len(units):
            s_next = scores(units[u + 1])
        o = _dot(vt_s[unit[2], unit[1]], _softmax_t(s_cur))
        unit[4](o[0:HEAD_DIM] / o[HEAD_DIM:HEAD_DIM + 1])

    y = xres_s[blk_rows, :] + gate * _dot(mix_s[...], wout_ref[l])
    xres_s[blk_rows, :] = y

    @pl.when(l == DEPTH - 1)
    def _():
        y_ref[...] = y.reshape(y_ref.shape)


def _const_spec(shape):
    nd = len(shape)
    return pl.BlockSpec(shape, lambda b, l, j: (0,) * nd, pipeline_mode=pl.Buffered(1))


def _mixer(x, mod, mod_row, wts, consts, nb, ctx=None):
    latent = ctx is not None
    n_batch, seq, _ = x.shape
    grp = nb * seq
    n_blk = grp // TOKEN_BLOCK
    assert grp % TOKEN_BLOCK == 0 and (nb == 1 or grp == TOKEN_BLOCK) and n_batch % nb == 0
    past = ctx[0].shape[2] if latent else 0
    last = DEPTH - 1

    args = [x, mod]
    specs = [
        pl.BlockSpec((nb, seq, D_MODEL), lambda b, l, j: (b, 0, 0),
                     pipeline_mode=pl.Buffered(1) if latent else None),
        pl.BlockSpec((1, 3, D_MODEL), lambda b, l, j: (l * MOD_ROWS + mod_row(b), 0, 0)),
    ]
    for name in ("norm_g", "wft", "wt", "wct", "wout", "gqa", "gka", "gqb", "gkb", "gsub", "lam"):
        args.append(wts[name])
        specs.append(_const_spec(wts[name].shape))
    for name in ("bd64", "bd32", "bc", "bs"):
        args.append(consts[name])
        specs.append(_const_spec(consts[name].shape))
    dft_cols = TOKEN_BLOCK if nb == 1 else seq
    for name in ("cs", "nss"):
        args.append(consts[name])
        specs.append(pl.BlockSpec((seq, dft_cols), lambda b, l, j: (0, j)))
    if latent:
        args += [ctx[0], ctx[1]]
        specs += [
            pl.BlockSpec((1, 1, past, 2 * KV_WIDTH_A), lambda b, l, j: (b, l, 0, 0), pipeline_mode=pl.Buffered(1)),
            pl.BlockSpec((1, 1, past, 2 * WIDTH_B), lambda b, l, j: (b, l, 0, 0), pipeline_mode=pl.Buffered(1)),
        ]
        for name, rows in (("cqa", HEAD_DIM), ("sqa", HEAD_DIM), ("cqb", DIFF_DIM), ("sqb", DIFF_DIM)):
            args.append(consts[name])
            specs.append(pl.BlockSpec((rows, TOKEN_BLOCK), lambda b, l, j: (0, j)))
        for name in ("cka", "skpa", "skma", "ckb", "skpb", "skmb"):
            args.append(consts[name])
            specs.append(_const_spec(consts[name].shape))

    y_shape = jax.ShapeDtypeStruct(x.shape, _F32)
    if nb == 1:
        y_spec = pl.BlockSpec((1, TOKEN_BLOCK, D_MODEL), lambda b, l, j: (b, jnp.where(l == last, j, 0), 0))
    else:
        y_spec = pl.BlockSpec((nb, seq, D_MODEL), lambda b, l, j: (b, 0, 0))
    if latent:
        out_shape, out_specs = y_shape, y_spec
    else:
        out_shape = (y_shape,
                     jax.ShapeDtypeStruct((n_batch, DEPTH, seq, 2 * KV_WIDTH_A), _F32),
                     jax.ShapeDtypeStruct((n_batch, DEPTH, seq, 2 * WIDTH_B), _F32))
        out_specs = (y_spec,
                     pl.BlockSpec((nb, 1, seq, 2 * KV_WIDTH_A), lambda b, l, j: (b, l, 0, 0)),
                     pl.BlockSpec((nb, 1, seq, 2 * WIDTH_B), lambda b, l, j: (b, l, 0, 0)))

    n_keys = grp + past
    scratch = [
        pltpu.VMEM((grp, D_MODEL), _F32),
        pltpu.VMEM((grp, D_MODEL), _BF16),
        pltpu.VMEM((n_keys, KV_WIDTH_A), _BF16),
        pltpu.VMEM((n_keys, WIDTH_B), _BF16),
        pltpu.VMEM((N_V_GROUPS * V_GROUP_ROWS, n_keys), _BF16),
        pltpu.VMEM((WIDTH_C, grp), _BF16),
        pltpu.VMEM((WIDTH_C, grp), _BF16),
        pltpu.VMEM((TOKEN_BLOCK, D_MODEL), _BF16),
    ]
    kern = functools.partial(_mixer_kernel, latent=latent, seq=seq, nb=nb, past=past)
    return pl.pallas_call(
        kern,
        grid=(n_batch // nb, DEPTH, n_blk),
        in_specs=specs,
        out_specs=out_specs,
        out_shape=out_shape,
        scratch_shapes=scratch,
        compiler_params=pltpu.CompilerParams(
            dimension_semantics=("arbitrary", "arbitrary", "arbitrary"),
            vmem_limit_bytes=V7X_VMEM_LIMIT_BYTES),
        name="latent_mixer" if latent else "context_mixer",
    )(*args)


def _rope_tables(n_tok, dim):
    m = dim // 4
    t = np.arange(n_tok)
    inv = 1.0 / (ROPE_BASE ** (np.arange(m, dtype=np.float64) / m))
    ar = (t // GRID_W)[:, None] * inv
    ac = (t % GRID_W)[:, None] * inv
    cos = np.concatenate([np.cos(ar), np.cos(ar), np.cos(ac), np.cos(ac)], axis=1)
    sin = np.concatenate([-np.sin(ar), np.sin(ar), -np.sin(ac), np.sin(ac)], axis=1)
    first = np.tile(np.concatenate([np.ones(m), np.zeros(m)]), 2)[None, :]
    return cos.astype(np.float32), sin.astype(np.float32), first.astype(np.float32)


def _block_diag(block, n):
    out = np.zeros((block.shape[0] * n, block.shape[1] * n), np.float64)
    for i in range(n):
        out[i * block.shape[0]:(i + 1) * block.shape[0], i * block.shape[1]:(i + 1) * block.shape[1]] = block
    return out


def _dft_cos_sin(n):
    k = np.arange(n)
    ang = 2.0 * np.pi * ((k[:, None] * k[None, :]) % n) / n
    return np.cos(ang), np.sin(ang)


def _mixer_consts(seq, latent):
    c64, s64 = _dft_cos_sin(GROUP_C)
    norm = 1.0 / math.sqrt(GROUP_C * seq)
    cs, ss = _dft_cos_sin(seq)
    f32c = lambda a: jnp.asarray(np.asarray(a, np.float32))
    consts = {
        "bd64": f32c(_block_diag(np.full((HEAD_DIM, HEAD_DIM), 1.0 / HEAD_DIM), N_KV_A)).astype(_BF16),
        "bd32": f32c(_block_diag(np.full((DIFF_DIM, DIFF_DIM), 1.0 / DIFF_DIM), 2 * N_HEADS_B)).astype(_BF16),
        "bc": f32c(_block_diag(c64, N_GROUPS_C) * norm).astype(_BF16),
        "bs": f32c(_block_diag(s64, N_GROUPS_C) * norm).astype(_BF16),
        "cs": f32c(cs).astype(_BF16),
        "nss": f32c(-ss).astype(_BF16),
    }
    if latent:
        cos_a, sin_a, first_a = _rope_tables(seq, HEAD_DIM)
        cos_b, sin_b, first_b = _rope_tables(seq, DIFF_DIM)
        rep_a, rep_b = LANES // HEAD_DIM, LANES // DIFF_DIM
        consts.update({
            "cqa": f32c(cos_a.T), "sqa": f32c(sin_a.T), "cqb": f32c(cos_b.T), "sqb": f32c(sin_b.T),
            "cka": f32c(np.tile(cos_a, (1, rep_a))),
            "skpa": f32c(np.tile(sin_a * (1.0 - first_a), (1, rep_a))),
            "skma": f32c(np.tile(sin_a * first_a, (1, rep_a))),
            "ckb": f32c(np.tile(cos_b, (1, rep_b))),
            "skpb": f32c(np.tile(sin_b * (1.0 - first_b), (1, rep_b))),
            "skmb": f32c(np.tile(sin_b * first_b, (1, rep_b))),
        })
    return consts


def _mixer_weights(norm_g, w_in, q_norm_a, k_norm_a, q_norm_b, k_norm_b,
                   lambda_q1, lambda_k1, lambda_q2, lambda_k2, subln_g, w_fourier, w_out):
    keep = jnp.asarray([1.0 - v for v in LAM_INIT], _F32)
    return {
        "norm_g": norm_g.reshape(DEPTH, 1, D_MODEL),
        "wft": _relayout(w_in, FEATURE_TILES, True, "w_in_feature_major"),
        "wt": _relayout(w_in, TOKEN_TILES, False, "w_in_token_major"),
        "wct": _relayout(w_fourier, (0,), True, "w_fourier_t"),
        "wout": _relayout(w_out, tuple(range(D_MODEL // WEIGHT_TILE)), False, "w_out_bf16"),
        "gqa": (q_norm_a * (HEAD_DIM ** -0.5 * LOG2_E)).reshape(DEPTH, HEAD_DIM, 1),
        "gka": jnp.tile(k_norm_a, (1, N_KV_A)).reshape(DEPTH, 1, KV_WIDTH_A),
        "gqb": (q_norm_b * (DIFF_DIM ** -0.5 * LOG2_E)).reshape(DEPTH, DIFF_DIM, 1),
        "gkb": jnp.tile(k_norm_b, (1, 2 * N_HEADS_B)).reshape(DEPTH, 1, WIDTH_B),
        "gsub": (subln_g * keep[:, None]).reshape(DEPTH, HEAD_DIM, 1),
        "lam": jnp.stack([lambda_q1, lambda_k1, lambda_q2, lambda_k2], axis=1),
    }


def kernel(x_prompt, x_sample, cache_attn_a, cache_attn_b, c, c_ctx, norm_g, w_mod, b_mod, w_in,
           q_norm_a, k_norm_a, q_norm_b, k_norm_b, lambda_q1, lambda_k1, lambda_q2, lambda_k2,
           subln_g, w_fourier, w_out):
    n_ctx_batch, ctx_seq, _ = x_prompt.shape
    n_lat_batch, lat_seq, _ = x_sample.shape
    past = cache_attn_a.shape[2]
    assert n_lat_batch + 1 <= MOD_ROWS and w_in.shape == (DEPTH, D_MODEL, D_IN)

    cond = jnp.concatenate(
        [c_ctx[None, :], c, jnp.zeros((MOD_ROWS - 1 - n_lat_batch, D_MODEL), _F32)], axis=0)
    mod = _modulation(cond, w_mod, b_mod).reshape(DEPTH * MOD_ROWS, 3, D_MODEL)

    weights = _mixer_weights(norm_g, w_in, q_norm_a, k_norm_a, q_norm_b, k_norm_b,
                             lambda_q1, lambda_k1, lambda_q2, lambda_k2, subln_g, w_fourier, w_out)

    y_prompt, new_a, new_b = _mixer(x_prompt, mod, lambda b: 0, weights,
                                    _mixer_consts(ctx_seq, latent=False), nb=TOKEN_BLOCK // ctx_seq)
    new_attn_a = new_a.reshape(n_ctx_batch, DEPTH, ctx_seq, 2, N_KV_A, HEAD_DIM)
    new_attn_b = new_b.reshape(n_ctx_batch, DEPTH, ctx_seq, 2, N_HEADS_B, HEAD_DIM)

    ctx_a = cache_attn_a.reshape(n_lat_batch, DEPTH, past, 2 * KV_WIDTH_A)
    ctx_b = cache_attn_b.reshape(n_lat_batch, DEPTH, past, 2 * WIDTH_B)
    y_sample = _mixer(x_sample, mod, lambda b: b + 1, weights,
                      _mixer_consts(lat_seq, latent=True), nb=1, ctx=(ctx_a, ctx_b))
    return (y_prompt, y_sample, new_attn_a, new_attn_b)
```

```python
import functools
import math

import jax
import jax.numpy as jnp
import numpy as np
from jax import lax
from jax.experimental import pallas as pl
from jax.experimental.pallas import tpu as pltpu

D_MODEL = 1024
DEPTH = 2
GRID_W = 64
HEAD_DIM = 64
N_HEADS_A = 8
N_KV_A = 2
N_HEADS_B = 4
DIFF_DIM = 32
GROUP_C = 64
N_GROUPS_C = 4
WIDTH_A = N_HEADS_A * HEAD_DIM
WIDTH_B = N_HEADS_B * HEAD_DIM
WIDTH_C = N_GROUPS_C * GROUP_C
KV_WIDTH_A = N_KV_A * HEAD_DIM
D_IN = 2 * WIDTH_A + 2 * KV_WIDTH_A + 4 * WIDTH_B + 2 * WIDTH_C
RMS_EPS = 1e-6
ROPE_BASE = 10000.0
LOG2_E = math.log2(math.e)
LAM_INIT = tuple(0.8 - 0.6 * math.exp(-0.3 * l) for l in range(DEPTH))

LANES = 128
TOKEN_BLOCK = 512
MOD_ROWS = 8
WEIGHT_TILE = 256
V7X_VMEM_LIMIT_BYTES = 58 * 1024 * 1024

FEATURE_TILES = (0, 1, 3, 4, 5, 8, 10, 9)
TOKEN_TILES = (2, 6, 7)
F_QA, F_GA, F_QB, F_GB, F_GC, F_UC = 0, 512, 1024, 1280, 1536, 1792
PF_CHUNK = 256
OUT_CHUNK = 256
F_ROWS = 2048
T_KA, T_VA, T_KB, T_VB = 0, 128, 256, 512
T_COLS = 768
ONES_ROWS = 16
V_GROUP_ROWS = HEAD_DIM + ONES_ROWS
N_V_GROUPS = N_KV_A + N_HEADS_B

_BF16 = jnp.bfloat16
_F32 = jnp.float32


def _silu(x):
    return x * (1.0 / (1.0 + jnp.exp(-x)))


def _dot(a, b):
    return jnp.dot(a, b, preferred_element_type=_F32)


def _dot_nt(a, b):
    return lax.dot_general(a, b, (((1,), (1,)), ((), ())), preferred_element_type=_F32)


def _mod_kernel(c_ref, w_ref, b_ref, o_ref):
    c = c_ref[...]
    o_ref[0] = _dot(_silu(c).astype(_BF16), w_ref[0].astype(_BF16)) + b_ref[0]


def _modulation(cond, w_mod, b_mod):
    n_tile = D_MODEL
    return pl.pallas_call(
        _mod_kernel,
        grid=(DEPTH, 3 * D_MODEL // n_tile),
        in_specs=[
            pl.BlockSpec((MOD_ROWS, D_MODEL), lambda l, n: (0, 0)),
            pl.BlockSpec((1, D_MODEL, n_tile), lambda l, n: (l, 0, n)),
            pl.BlockSpec((1, 1, n_tile), lambda l, n: (l, 0, n)),
        ],
        out_specs=pl.BlockSpec((1, MOD_ROWS, n_tile), lambda l, n: (l, 0, n)),
        out_shape=jax.ShapeDtypeStruct((DEPTH, MOD_ROWS, 3 * D_MODEL), _F32),
        name="modulation",
    )(cond, w_mod, b_mod.reshape(DEPTH, 1, 3 * D_MODEL))


def _relayout_kernel(tiles_ref, w_ref, o_ref, *, transpose):
    del tiles_ref
    w = w_ref[0]
    o_ref[0] = (w.T if transpose else w).astype(_BF16)


def _relayout(w, tiles, transpose, name):
    depth, rows, _ = w.shape
    n = len(tiles)
    if transpose:
        out_shape = (depth, n * WEIGHT_TILE, rows)
        out_spec = pl.BlockSpec((1, WEIGHT_TILE, rows), lambda l, i, t: (l, i, 0))
    else:
        out_shape = (depth, rows, n * WEIGHT_TILE)
        out_spec = pl.BlockSpec((1, rows, WEIGHT_TILE), lambda l, i, t: (l, 0, i))
    return pl.pallas_call(
        functools.partial(_relayout_kernel, transpose=transpose),
        grid_spec=pltpu.PrefetchScalarGridSpec(
            num_scalar_prefetch=1,
            grid=(depth, n),
            in_specs=[pl.BlockSpec((1, rows, WEIGHT_TILE), lambda l, i, t: (l, 0, t[i]))],
            out_specs=out_spec),
        out_shape=jax.ShapeDtypeStruct(out_shape, _BF16),
        name=name,
    )(jnp.asarray(tiles, jnp.int32), w)


def _softmax_t(s_t):
    m = jnp.max(s_t, axis=0, keepdims=True)
    return jnp.exp2(s_t - m).astype(_BF16)


def _rms_rows(x):
    return lax.rsqrt(jnp.mean(x * x, axis=0, keepdims=True) + RMS_EPS)


def _swap_halves(x, m):
    return jnp.concatenate([x[m:2 * m], x[0:m], x[3 * m:4 * m], x[2 * m:3 * m]], axis=0)


def _pad_rows(x, start, total):
    parts = []
    if start:
        parts.append(jnp.zeros((start, x.shape[1]), x.dtype))
    parts.append(x)
    rest = total - start - x.shape[0]
    if rest:
        parts.append(jnp.zeros((rest, x.shape[1]), x.dtype))
    return jnp.concatenate(parts, axis=0)


def _cat(parts, axis):
    return parts[0] if len(parts) == 1 else jnp.concatenate(parts, axis=axis)


def _mixer_kernel(*refs, latent, seq, nb, past):
    grp = nb * seq
    n_blk = grp // TOKEN_BLOCK
    tb = TOKEN_BLOCK // nb
    it = iter(refs)
    x_ref, mod_ref, ng_ref = next(it), next(it), next(it)
    wft_ref, wt_ref, wct_ref, wout_ref = next(it), next(it), next(it), next(it)
    gqa_ref, gka_ref, gqb_ref, gkb_ref, gsub_ref, lam_ref = (next(it) for _ in range(6))
    bd64_ref, bd32_ref, bc_ref, bs_ref, cs_ref, nss_ref = (next(it) for _ in range(6))
    if latent:
        ctxa_ref, ctxb_ref = next(it), next(it)
        cqa_ref, sqa_ref, cqb_ref, sqb_ref = (next(it) for _ in range(4))
        cka_ref, skpa_ref, skma_ref, ckb_ref, skpb_ref, skmb_ref = (next(it) for _ in range(6))
        y_ref = next(it)
    else:
        y_ref, newa_ref, newb_ref = next(it), next(it), next(it)
    xres_s, h_s, ka_s, kb_s, vt_s, t1_s, t2_s, mix_s = (next(it) for _ in range(8))

    l = pl.program_id(1)
    j = pl.program_id(2)
    shift = mod_ref[0, 0:1, :]
    scale = mod_ref[0, 1:2, :]
    gate = mod_ref[0, 2:3, :]

    @pl.when((l == 0) & (j == 0))
    def _():
        xres_s[...] = x_ref[...].reshape(grp, D_MODEL)
        for g in range(N_V_GROUPS):
            vt_s[g * V_GROUP_ROWS + HEAD_DIM:(g + 1) * V_GROUP_ROWS, :] = jnp.ones(
                (ONES_ROWS, grp + past), _BF16)

    def rope_k(k, cos_ref, sp_ref, sm_ref, rows, m):
        out = []
        for c0 in range(0, k.shape[1], LANES):
            kc = k[:, c0:c0 + LANES]
            out.append(kc * cos_ref[rows, :] + pltpu.roll(kc, m, 1) * sp_ref[rows, :]
                       + pltpu.roll(kc, LANES - m, 1) * sm_ref[rows, :])
        return _cat(out, 1)

    def put_values(vt, cols):
        for g in range(N_V_GROUPS):
            vt_s[g * V_GROUP_ROWS:g * V_GROUP_ROWS + HEAD_DIM, cols] = vt[g * HEAD_DIM:(g + 1) * HEAD_DIM]

    @pl.when(j == 0)
    def _prep():
        for c in range(n_blk):
            rows = slice(c * TOKEN_BLOCK, (c + 1) * TOKEN_BLOCK)
            xc = xres_s[rows, :]
            ms = jnp.mean(xc * xc, axis=-1, keepdims=True)
            hc = (xc * lax.rsqrt(ms + RMS_EPS) * ng_ref[l] * (1.0 + scale) + shift).astype(_BF16)
            h_s[rows, :] = hc
            pt = _dot(hc, wt_ref[l])
            ka = pt[:, T_KA:T_KA + KV_WIDTH_A]
            va = pt[:, T_VA:T_VA + KV_WIDTH_A]
            kb = pt[:, T_KB:T_KB + WIDTH_B]
            vb = pt[:, T_VB:T_VB + WIDTH_B]
            ka = ka * lax.rsqrt(_dot((ka * ka).astype(_BF16), bd64_ref[...]) + RMS_EPS) * gka_ref[l]
            kb = kb * lax.rsqrt(_dot((kb * kb).astype(_BF16), bd32_ref[...]) + RMS_EPS) * gkb_ref[l]
            if latent:
                ka = rope_k(ka, cka_ref, skpa_ref, skma_ref, rows, HEAD_DIM // 4)
                kb = rope_k(kb, ckb_ref, skpb_ref, skmb_ref, rows, DIFF_DIM // 4)
            else:
                for i in range(nb):
                    r = slice(i * seq, (i + 1) * seq)
                    newa_ref[i, 0, :, 0:KV_WIDTH_A] = ka[r]
                    newa_ref[i, 0, :, KV_WIDTH_A:2 * KV_WIDTH_A] = va[r]
                    newb_ref[i, 0, :, 0:WIDTH_B] = kb[r]
                    newb_ref[i, 0, :, WIDTH_B:2 * WIDTH_B] = vb[r]
            ka_s[rows, :] = ka.astype(_BF16)
            kb_s[rows, :] = kb.astype(_BF16)
            put_values(jnp.concatenate([va.T, vb.T], axis=0).astype(_BF16), rows)
            uct = _dot_nt(wft_ref[l, F_UC:F_ROWS, :], hc).astype(_BF16)
            t1_s[:, rows] = _dot(bc_ref[...], uct).astype(_BF16)
            t2_s[:, rows] = _dot(bs_ref[...], uct).astype(_BF16)
        if latent:
            ca = ctxa_ref[0, 0]
            cb = ctxb_ref[0, 0]
            ka_s[grp:grp + past, :] = ca[:, 0:KV_WIDTH_A].astype(_BF16)
            kb_s[grp:grp + past, :] = cb[:, 0:WIDTH_B].astype(_BF16)
            put_values(jnp.concatenate([ca[:, KV_WIDTH_A:].T, cb[:, WIDTH_B:].T], axis=0).astype(_BF16),
                       slice(grp, grp + past))

    blk_rows = pl.ds(pl.multiple_of(j * TOKEN_BLOCK, TOKEN_BLOCK), TOKEN_BLOCK)

    pf_chunks = {}

    def pf_chunk(c):
        if c not in pf_chunks:
            pf_chunks[c] = _dot_nt(wft_ref[l, c * PF_CHUNK:(c + 1) * PF_CHUNK, :], h_s[blk_rows, :])
        return pf_chunks[c]

    def pf_rows(r0, n, cols):
        c, off = divmod(r0, PF_CHUNK)
        assert off + n <= PF_CHUNK
        return pf_chunk(c)[off:off + n, cols]

    lv = lam_ref[l]
    lam_init = jnp.where(l == 0, LAM_INIT[0], LAM_INIT[1])
    lam = (jnp.exp(jnp.sum(lv[0:1] * lv[1:2], axis=-1, keepdims=True))
           - jnp.exp(jnp.sum(lv[2:3] * lv[3:4], axis=-1, keepdims=True)) + lam_init)

    def q_tile(r0, dim, gain_ref, cos_ref, sin_ref, cols, pad_start, pad_total):
        q = pf_rows(r0, dim, cols)
        q = q * _rms_rows(q) * gain_ref[l]
        if latent:
            q = q * cos_ref[...] + _swap_halves(q, dim // 4) * sin_ref[...]
        return _pad_rows(q.astype(_BF16), pad_start, pad_total)

    def q_a(h, cols):
        g = h // (N_HEADS_A // N_KV_A)
        return q_tile(F_QA + h * HEAD_DIM, HEAD_DIM, gqa_ref, cqa_ref if latent else None,
                      sqa_ref if latent else None, cols, g * HEAD_DIM, KV_WIDTH_A)

    def q_b(h, comp, cols):
        r0 = h * HEAD_DIM + comp * DIFF_DIM
        return q_tile(F_QB + r0, DIFF_DIM, gqb_ref, cqb_ref if latent else None,
                      sqb_ref if latent else None, cols, r0, WIDTH_B)

    def put_mix(gated_t, cols, c0):
        mix_s[cols, c0:c0 + gated_t.shape[0]] = gated_t.T.astype(_BF16)

    half = {}

    def put_head(slot, cols, gated_t):
        if slot % 2 == 0:
            half[cols.start] = gated_t
        else:
            put_mix(jnp.concatenate([half.pop(cols.start), gated_t], axis=0), cols, (slot - 1) * HEAD_DIM)

    def store_a(h, cols, o):
        ga = pf_rows(F_GA + h * HEAD_DIM, HEAD_DIM, cols)
        put_head(h, cols, o * _silu(ga))

    def store_b(h, cols, o1, o2):
        ob = o1 - lam * o2
        ob = ob * _rms_rows(ob) * gsub_ref[l]
        gb = pf_rows(F_GB + h * HEAD_DIM, HEAD_DIM, cols)
        put_head(N_HEADS_A + h, cols, ob * _silu(gb))

    def v_group(g):
        return slice(g * V_GROUP_ROWS, (g + 1) * V_GROUP_ROWS)

    units = []
    all_cols = slice(0, TOKEN_BLOCK)
    if nb == 1:
        keys = slice(0, grp + past)
        for h in range(N_HEADS_A):
            g = h // (N_HEADS_A // N_KV_A)
            units.append(("a", keys, v_group(g),
                          [functools.partial(q_a, h, all_cols)],
                          functools.partial(store_a, h, all_cols)))
        for h in range(N_HEADS_B):
            v_rows = v_group(N_KV_A + h)
            held = {}
            for comp in range(2):
                def fin(o, h=h, comp=comp, held=held):
                    held[comp] = o
                    if comp == 1:
                        store_b(h, all_cols, held[0], held[1])
                units.append(("b", keys, v_rows, [functools.partial(q_b, h, comp, all_cols)], fin))
    else:
        seqs = [(slice(i * seq, (i + 1) * seq), slice(i * tb, (i + 1) * tb)) for i in range(nb)]
        for u in range(N_HEADS_A // 2):
            g = (2 * u) // (N_HEADS_A // N_KV_A)
            for keys, cols in seqs:
                def fin(o, u=u, cols=cols):
                    store_a(2 * u, cols, o[:, 0:tb])
                    store_a(2 * u + 1, cols, o[:, tb:2 * tb])
                units.append(("a", keys, v_group(g),
                              [functools.partial(q_a, 2 * u, cols), functools.partial(q_a, 2 * u + 1, cols)], fin))
        for h in range(N_HEADS_B):
            for keys, cols in seqs:
                def fin(o, h=h, cols=cols):
                    store_b(h, cols, o[:, 0:tb], o[:, tb:2 * tb])
                units.append(("b", keys, v_group(N_KV_A + h),
                              [functools.partial(q_b, h, 0, cols), functools.partial(q_b, h, 1, cols)], fin))

    def scores(unit):
        kind, keys, _, tiles, _ = unit
        k_ref = ka_s if kind == "a" else kb_s
        return _dot(k_ref[keys, :], _cat([t() for t in tiles], 1))

    def mixer_c():
        if nb == 1:
            ft = _dot(t1_s[...], cs_ref[...]) + _dot(t2_s[...], nss_ref[...])
        else:
            ft = _cat([_dot(t1_s[:, i * seq:(i + 1) * seq], cs_ref[...])
                       + _dot(t2_s[:, i * seq:(i + 1) * seq], nss_ref[...]) for i in range(nb)], 1)
        oc = _dot(wct_ref[l], ft.astype(_BF16))
        put_mix(oc * _silu(pf_rows(F_GC, WIDTH_C, all_cols)), all_cols, WIDTH_A + WIDTH_B)

    out_parts = []

    def out_chunk(k):
        cols = slice(k * OUT_CHUNK, (k + 1) * OUT_CHUNK)
        out_parts.append(_dot(mix_s[:, cols], wout_ref[l, cols, :]))

    fillers = {
        0: [functools.partial(pf_chunk, F_GA // PF_CHUNK)],
        1: [functools.partial(pf_chunk, F_QA // PF_CHUNK + 1)],
        2: [functools.partial(pf_chunk, F_GA // PF_CHUNK + 1)],
        3: [functools.partial(pf_chunk, F_QB // PF_CHUNK)],
        4: [functools.partial(pf_chunk, F_GB // PF_CHUNK)],
        5: [mixer_c],
        6: [functools.partial(out_chunk, 3), functools.partial(out_chunk, 0)],
        9: [functools.partial(out_chunk, 1)],
    }
    s_next = scores(units[0])
    for u, unit in enumerate(units):
        s_cur = s_next
        if u + 1 < len(units):
            s_next = scores(units[u + 1])
        for work in fillers.get(u, ()):
            work()
        o = _dot(vt_s[unit[2], unit[1]], _softmax_t(s_cur))
        unit[4](o[0:HEAD_DIM] / o[HEAD_DIM:HEAD_DIM + 1])
    out_chunk(2)

    y = xres_s[blk_rows, :] + gate * functools.reduce(lambda a, b: a + b, out_parts)
    xres_s[blk_rows, :] = y

    @pl.when(l == DEPTH - 1)
    def _():
        y_ref[...] = y.reshape(y_ref.shape)


def _const_spec(shape):
    nd = len(shape)
    return pl.BlockSpec(shape, lambda b, l, j: (0,) * nd, pipeline_mode=pl.Buffered(1))


def _mixer(x, mod, mod_row, wts, consts, nb, ctx=None):
    latent = ctx is not None
    n_batch, seq, _ = x.shape
    grp = nb * seq
    n_blk = grp // TOKEN_BLOCK
    assert grp % TOKEN_BLOCK == 0 and (nb == 1 or grp == TOKEN_BLOCK) and n_batch % nb == 0
    past = ctx[0].shape[2] if latent else 0
    last = DEPTH - 1

    args = [x, mod]
    specs = [
        pl.BlockSpec((nb, seq, D_MODEL), lambda b, l, j: (b, 0, 0),
                     pipeline_mode=pl.Buffered(1) if latent else None),
        pl.BlockSpec((1, 3, D_MODEL), lambda b, l, j: (l * MOD_ROWS + mod_row(b), 0, 0)),
    ]
    for name in ("norm_g", "wft", "wt", "wct", "wout", "gqa", "gka", "gqb", "gkb", "gsub", "lam"):
        args.append(wts[name])
        specs.append(_const_spec(wts[name].shape))
    for name in ("bd64", "bd32", "bc", "bs"):
        args.append(consts[name])
        specs.append(_const_spec(consts[name].shape))
    dft_cols = TOKEN_BLOCK if nb == 1 else seq
    for name in ("cs", "nss"):
        args.append(consts[name])
        specs.append(pl.BlockSpec((seq, dft_cols), lambda b, l, j: (0, j)))
    if latent:
        args += [ctx[0], ctx[1]]
        specs += [
            pl.BlockSpec((1, 1, past, 2 * KV_WIDTH_A), lambda b, l, j: (b, l, 0, 0), pipeline_mode=pl.Buffered(1)),
            pl.BlockSpec((1, 1, past, 2 * WIDTH_B), lambda b, l, j: (b, l, 0, 0), pipeline_mode=pl.Buffered(1)),
        ]
        for name, rows in (("cqa", HEAD_DIM), ("sqa", HEAD_DIM), ("cqb", DIFF_DIM), ("sqb", DIFF_DIM)):
            args.append(consts[name])
            specs.append(pl.BlockSpec((rows, TOKEN_BLOCK), lambda b, l, j: (0, j)))
        for name in ("cka", "skpa", "skma", "ckb", "skpb", "skmb"):
            args.append(consts[name])
            specs.append(_const_spec(consts[name].shape))

    y_shape = jax.ShapeDtypeStruct(x.shape, _F32)
    if nb == 1:
        y_spec = pl.BlockSpec((1, TOKEN_BLOCK, D_MODEL), lambda b, l, j: (b, jnp.where(l == last, j, 0), 0))
    else:
        y_spec = pl.BlockSpec((nb, seq, D_MODEL), lambda b, l, j: (b, 0, 0))
    if latent:
        out_shape, out_specs = y_shape, y_spec
    else:
        out_shape = (y_shape,
                     jax.ShapeDtypeStruct((n_batch, DEPTH, seq, 2 * KV_WIDTH_A), _F32),
                     jax.ShapeDtypeStruct((n_batch, DEPTH, seq, 2 * WIDTH_B), _F32))
        out_specs = (y_spec,
                     pl.BlockSpec((nb, 1, seq, 2 * KV_WIDTH_A), lambda b, l, j: (b, l, 0, 0)),
                     pl.BlockSpec((nb, 1, seq, 2 * WIDTH_B), lambda b, l, j: (b, l, 0, 0)))

    n_keys = grp + past
    scratch = [
        pltpu.VMEM((grp, D_MODEL), _F32),
        pltpu.VMEM((grp, D_MODEL), _BF16),
        pltpu.VMEM((n_keys, KV_WIDTH_A), _BF16),
        pltpu.VMEM((n_keys, WIDTH_B), _BF16),
        pltpu.VMEM((N_V_GROUPS * V_GROUP_ROWS, n_keys), _BF16),
        pltpu.VMEM((WIDTH_C, grp), _BF16),
        pltpu.VMEM((WIDTH_C, grp), _BF16),
        pltpu.VMEM((TOKEN_BLOCK, D_MODEL), _BF16),
    ]
    kern = functools.partial(_mixer_kernel, latent=latent, seq=seq, nb=nb, past=past)
    return pl.pallas_call(
        kern,
        grid=(n_batch // nb, DEPTH, n_blk),
        in_specs=specs,
        out_specs=out_specs,
        out_shape=out_shape,
        scratch_shapes=scratch,
        compiler_params=pltpu.CompilerParams(
            dimension_semantics=("arbitrary", "arbitrary", "arbitrary"),
            vmem_limit_bytes=V7X_VMEM_LIMIT_BYTES),
        name="latent_mixer" if latent else "context_mixer",
    )(*args)


def _rope_tables(n_tok, dim):
    m = dim // 4
    t = np.arange(n_tok)
    inv = 1.0 / (ROPE_BASE ** (np.arange(m, dtype=np.float64) / m))
    ar = (t // GRID_W)[:, None] * inv
    ac = (t % GRID_W)[:, None] * inv
    cos = np.concatenate([np.cos(ar), np.cos(ar), np.cos(ac), np.cos(ac)], axis=1)
    sin = np.concatenate([-np.sin(ar), np.sin(ar), -np.sin(ac), np.sin(ac)], axis=1)
    first = np.tile(np.concatenate([np.ones(m), np.zeros(m)]), 2)[None, :]
    return cos.astype(np.float32), sin.astype(np.float32), first.astype(np.float32)


def _block_diag(block, n):
    out = np.zeros((block.shape[0] * n, block.shape[1] * n), np.float64)
    for i in range(n):
        out[i * block.shape[0]:(i + 1) * block.shape[0], i * block.shape[1]:(i + 1) * block.shape[1]] = block
    return out


def _dft_cos_sin(n):
    k = np.arange(n)
    ang = 2.0 * np.pi * ((k[:, None] * k[None, :]) % n) / n
    return np.cos(ang), np.sin(ang)


def _mixer_consts(seq, latent):
    c64, s64 = _dft_cos_sin(GROUP_C)
    norm = 1.0 / math.sqrt(GROUP_C * seq)
    cs, ss = _dft_cos_sin(seq)
    f32c = lambda a: jnp.asarray(np.asarray(a, np.float32))
    consts = {
        "bd64": f32c(_block_diag(np.full((HEAD_DIM, HEAD_DIM), 1.0 / HEAD_DIM), N_KV_A)).astype(_BF16),
        "bd32": f32c(_block_diag(np.full((DIFF_DIM, DIFF_DIM), 1.0 / DIFF_DIM), 2 * N_HEADS_B)).astype(_BF16),
        "bc": f32c(_block_diag(c64, N_GROUPS_C) * norm).astype(_BF16),
        "bs": f32c(_block_diag(s64, N_GROUPS_C) * norm).astype(_BF16),
        "cs": f32c(cs).astype(_BF16),
        "nss": f32c(-ss).astype(_BF16),
    }
    if latent:
        cos_a, sin_a, first_a = _rope_tables(seq, HEAD_DIM)
        cos_b, sin_b, first_b = _rope_tables(seq, DIFF_DIM)
        rep_a, rep_b = LANES // HEAD_DIM, LANES // DIFF_DIM
        consts.update({
            "cqa": f32c(cos_a.T), "sqa": f32c(sin_a.T), "cqb": f32c(cos_b.T), "sqb": f32c(sin_b.T),
            "cka": f32c(np.tile(cos_a, (1, rep_a))),
            "skpa": f32c(np.tile(sin_a * (1.0 - first_a), (1, rep_a))),
            "skma": f32c(np.tile(sin_a * first_a, (1, rep_a))),
            "ckb": f32c(np.tile(cos_b, (1, rep_b))),
            "skpb": f32c(np.tile(sin_b * (1.0 - first_b), (1, rep_b))),
            "skmb": f32c(np.tile(sin_b * first_b, (1, rep_b))),
        })
    return consts


def _mixer_weights(norm_g, w_in, q_norm_a, k_norm_a, q_norm_b, k_norm_b,
                   lambda_q1, lambda_k1, lambda_q2, lambda_k2, subln_g, w_fourier, w_out):
    keep = jnp.asarray([1.0 - v for v in LAM_INIT], _F32)
    return {
        "norm_g": norm_g.reshape(DEPTH, 1, D_MODEL),
        "wft": _relayout(w_in, FEATURE_TILES, True, "w_in_feature_major"),
        "wt": _relayout(w_in, TOKEN_TILES, False, "w_in_token_major"),
        "wct": _relayout(w_fourier, (0,), True, "w_fourier_t"),
        "wout": _relayout(w_out, tuple(range(D_MODEL // WEIGHT_TILE)), False, "w_out_bf16"),
        "gqa": (q_norm_a * (HEAD_DIM ** -0.5 * LOG2_E)).reshape(DEPTH, HEAD_DIM, 1),
        "gka": jnp.tile(k_norm_a, (1, N_KV_A)).reshape(DEPTH, 1, KV_WIDTH_A),
        "gqb": (q_norm_b * (DIFF_DIM ** -0.5 * LOG2_E)).reshape(DEPTH, DIFF_DIM, 1),
        "gkb": jnp.tile(k_norm_b, (1, 2 * N_HEADS_B)).reshape(DEPTH, 1, WIDTH_B),
        "gsub": (subln_g * keep[:, None]).reshape(DEPTH, HEAD_DIM, 1),
        "lam": jnp.stack([lambda_q1, lambda_k1, lambda_q2, lambda_k2], axis=1),
    }


def kernel(x_prompt, x_sample, cache_attn_a, cache_attn_b, c, c_ctx, norm_g, w_mod, b_mod, w_in,
           q_norm_a, k_norm_a, q_norm_b, k_norm_b, lambda_q1, lambda_k1, lambda_q2, lambda_k2,
           subln_g, w_fourier, w_out):
    n_ctx_batch, ctx_seq, _ = x_prompt.shape
    n_lat_batch, lat_seq, _ = x_sample.shape
    past = cache_attn_a.shape[2]
    assert n_lat_batch + 1 <= MOD_ROWS and w_in.shape == (DEPTH, D_MODEL, D_IN)

    cond = jnp.concatenate(
        [c_ctx[None, :], c, jnp.zeros((MOD_ROWS - 1 - n_lat_batch, D_MODEL), _F32)], axis=0)
    mod = _modulation(cond, w_mod, b_mod).reshape(DEPTH * MOD_ROWS, 3, D_MODEL)

    weights = _mixer_weights(norm_g, w_in, q_norm_a, k_norm_a, q_norm_b, k_norm_b,
                             lambda_q1, lambda_k1, lambda_q2, lambda_k2, subln_g, w_fourier, w_out)

    y_prompt, new_a, new_b = _mixer(x_prompt, mod, lambda b: 0, weights,
                                    _mixer_consts(ctx_seq, latent=False), nb=TOKEN_BLOCK // ctx_seq)
    new_attn_a = new_a.reshape(n_ctx_batch, DEPTH, ctx_seq, 2, N_KV_A, HEAD_DIM)
    new_attn_b = new_b.reshape(n_ctx_batch, DEPTH, ctx_seq, 2, N_HEADS_B, HEAD_DIM)

    ctx_a = cache_attn_a.reshape(n_lat_batch, DEPTH, past, 2 * KV_WIDTH_A)
    ctx_b = cache_attn_b.reshape(n_lat_batch, DEPTH, past, 2 * WIDTH_B)
    y_sample = _mixer(x_sample, mod, lambda b: b + 1, weights,
                      _mixer_consts(lat_seq, latent=True), nb=1, ctx=(ctx_a, ctx_b))
    return (y_prompt, y_sample, new_attn_a, new_attn_b)
```

```python
import functools
import math

import jax
import jax.numpy as jnp
import numpy as np
from jax import lax
from jax.experimental import pallas as pl
from jax.experimental.pallas import tpu as pltpu

D_MODEL = 1024
DEPTH = 2
GRID_W = 64
HEAD_DIM = 64
N_HEADS_A = 8
N_KV_A = 2
N_HEADS_B = 4
DIFF_DIM = 32
GROUP_C = 64
N_GROUPS_C = 4
WIDTH_A = N_HEADS_A * HEAD_DIM
WIDTH_B = N_HEADS_B * HEAD_DIM
WIDTH_C = N_GROUPS_C * GROUP_C
KV_WIDTH_A = N_KV_A * HEAD_DIM
D_IN = 2 * WIDTH_A + 2 * KV_WIDTH_A + 4 * WIDTH_B + 2 * WIDTH_C
RMS_EPS = 1e-6
ROPE_BASE = 10000.0
LOG2_E = math.log2(math.e)
LAM_INIT = tuple(0.8 - 0.6 * math.exp(-0.3 * l) for l in range(DEPTH))

LANES = 128
TOKEN_BLOCK = 512
MOD_ROWS = 8
WEIGHT_TILE = 256
V7X_VMEM_LIMIT_BYTES = 62 * 1024 * 1024
BOUND_SLACK = 1.02
DEN_MIN = 2.0 ** -40
SEQ_ROWS = 8

FEATURE_TILES = (0, 1, 3, 4, 5, 8, 10, 9)
TOKEN_TILES = (2, 6, 7)
F_QA, F_GA, F_QB, F_GB, F_GC, F_UC = 0, 512, 1024, 1280, 1536, 1792
PF_CHUNK = 256
OUT_CHUNK = 256
F_ROWS = 2048
T_KA, T_VA, T_KB, T_VB = 0, 128, 256, 512
T_COLS = 768
ONES_ROWS = 16
V_GROUP_ROWS = HEAD_DIM + ONES_ROWS
N_V_GROUPS = N_KV_A + N_HEADS_B

_BF16 = jnp.bfloat16
_F32 = jnp.float32


def _silu(x):
    return x * (1.0 / (1.0 + jnp.exp(-x)))


def _dot(a, b):
    return jnp.dot(a, b, preferred_element_type=_F32)


def _dot_nt(a, b):
    return lax.dot_general(a, b, (((1,), (1,)), ((), ())), preferred_element_type=_F32)


def _mod_kernel(c_ref, w_ref, b_ref, o_ref):
    c = c_ref[...]
    o_ref[0] = _dot(_silu(c).astype(_BF16), w_ref[0].astype(_BF16)) + b_ref[0]


def _modulation(cond, w_mod, b_mod):
    n_tile = D_MODEL
    return pl.pallas_call(
        _mod_kernel,
        grid=(DEPTH, 3 * D_MODEL // n_tile),
        in_specs=[
            pl.BlockSpec((MOD_ROWS, D_MODEL), lambda l, n: (0, 0)),
            pl.BlockSpec((1, D_MODEL, n_tile), lambda l, n: (l, 0, n)),
            pl.BlockSpec((1, 1, n_tile), lambda l, n: (l, 0, n)),
        ],
        out_specs=pl.BlockSpec((1, MOD_ROWS, n_tile), lambda l, n: (l, 0, n)),
        out_shape=jax.ShapeDtypeStruct((DEPTH, MOD_ROWS, 3 * D_MODEL), _F32),
        name="modulation",
    )(cond, w_mod, b_mod.reshape(DEPTH, 1, 3 * D_MODEL))


def _relayout_kernel(tiles_ref, w_ref, o_ref, *, transpose):
    del tiles_ref
    w = w_ref[0]
    o_ref[0] = (w.T if transpose else w).astype(_BF16)


def _relayout(w, tiles, transpose, name):
    depth, rows, _ = w.shape
    n = len(tiles)
    if transpose:
        out_shape = (depth, n * WEIGHT_TILE, rows)
        out_spec = pl.BlockSpec((1, WEIGHT_TILE, rows), lambda l, i, t: (l, i, 0))
    else:
        out_shape = (depth, rows, n * WEIGHT_TILE)
        out_spec = pl.BlockSpec((1, rows, WEIGHT_TILE), lambda l, i, t: (l, 0, i))
    return pl.pallas_call(
        functools.partial(_relayout_kernel, transpose=transpose),
        grid_spec=pltpu.PrefetchScalarGridSpec(
            num_scalar_prefetch=1,
            grid=(depth, n),
            in_specs=[pl.BlockSpec((1, rows, WEIGHT_TILE), lambda l, i, t: (l, 0, t[i]))],
            out_specs=out_spec),
        out_shape=jax.ShapeDtypeStruct(out_shape, _BF16),
        name=name,
    )(jnp.asarray(tiles, jnp.int32), w)


def _softmax_t(s_t, shift=None):
    if shift is None:
        shift = jnp.max(s_t, axis=0, keepdims=True)
    return jnp.exp2(s_t - shift).astype(_BF16)


def _rms_rows(x):
    return lax.rsqrt(jnp.mean(x * x, axis=0, keepdims=True) + RMS_EPS)


def _swap_halves(x, m):
    return jnp.concatenate([x[m:2 * m], x[0:m], x[3 * m:4 * m], x[2 * m:3 * m]], axis=0)


def _pad_rows(x, start, total):
    parts = []
    if start:
        parts.append(jnp.zeros((start, x.shape[1]), x.dtype))
    parts.append(x)
    rest = total - start - x.shape[0]
    if rest:
        parts.append(jnp.zeros((rest, x.shape[1]), x.dtype))
    return jnp.concatenate(parts, axis=0)


def _cat(parts, axis):
    return parts[0] if len(parts) == 1 else jnp.concatenate(parts, axis=axis)


def _mixer_kernel(*refs, latent, seq, nb, past):
    grp = nb * seq
    n_blk = grp // TOKEN_BLOCK
    tb = TOKEN_BLOCK // nb
    it = iter(refs)
    x_ref, mod_ref, ng_ref = next(it), next(it), next(it)
    wft_ref, wt_ref, wct_ref, wout_ref = next(it), next(it), next(it), next(it)
    gqa_ref, gka_ref, gqb_ref, gkb_ref, gsub_ref, lam_ref = (next(it) for _ in range(6))
    bd64_ref, bd32_ref, bc_ref, bs_ref, cs_ref, nss_ref = (next(it) for _ in range(6))
    if latent:
        ctxa_ref, ctxb_ref = next(it), next(it)
        cqa_ref, sqa_ref, cqb_ref, sqb_ref = (next(it) for _ in range(4))
        cka_ref, skpa_ref, skma_ref, ckb_ref, skpb_ref, skmb_ref = (next(it) for _ in range(6))
        y_ref = next(it)
    else:
        y_ref, newa_ref, newb_ref = next(it), next(it), next(it)
    xres_s, h_s, ka_s, kb_s, vt_s, t1_s, t2_s, mix_s, kn2a_s, kn2b_s, xold_s = (next(it) for _ in range(11))

    l = pl.program_id(1)
    j = pl.program_id(2)
    shift = mod_ref[0, 0:1, :]
    scale = mod_ref[0, 1:2, :]
    gate = mod_ref[0, 2:3, :]

    @pl.when((l == 0) & (j == 0))
    def _():
        xres_s[...] = x_ref[...].reshape(grp, D_MODEL)
        for g in range(N_V_GROUPS):
            vt_s[g * V_GROUP_ROWS + HEAD_DIM:(g + 1) * V_GROUP_ROWS, :] = jnp.ones(
                (ONES_ROWS, grp + past), _BF16)

    def rope_k(k, cos_ref, sp_ref, sm_ref, rows, m):
        out = []
        for c0 in range(0, k.shape[1], LANES):
            kc = k[:, c0:c0 + LANES]
            out.append(kc * cos_ref[rows, :] + pltpu.roll(kc, m, 1) * sp_ref[rows, :]
                       + pltpu.roll(kc, LANES - m, 1) * sm_ref[rows, :])
        return _cat(out, 1)

    def note_key_norms(i, ka, kb, first):
        for k, bd_ref, dim, kn2_s in ((ka, bd64_ref, HEAD_DIM, kn2a_s), (kb, bd32_ref, DIFF_DIM, kn2b_s)):
            n2 = jnp.max(_dot((k * k).astype(_BF16), bd_ref[...]), axis=0, keepdims=True) * float(dim)
            kn2_s[i:i + 1, :] = n2 if first else jnp.maximum(kn2_s[i:i + 1, :], n2)

    def put_values(vt, cols):
        for g in range(N_V_GROUPS):
            vt_s[g * V_GROUP_ROWS:g * V_GROUP_ROWS + HEAD_DIM, cols] = vt[g * HEAD_DIM:(g + 1) * HEAD_DIM]

    @pl.when(j == 0)
    def _prep():
        for c in range(n_blk):
            rows = slice(c * TOKEN_BLOCK, (c + 1) * TOKEN_BLOCK)
            xc = xres_s[rows, :]
            ms = jnp.mean(xc * xc, axis=-1, keepdims=True)
            hc = (xc * lax.rsqrt(ms + RMS_EPS) * ng_ref[l] * (1.0 + scale) + shift).astype(_BF16)
            h_s[rows, :] = hc
            pt = _dot(hc, wt_ref[l])
            ka = pt[:, T_KA:T_KA + KV_WIDTH_A]
            va = pt[:, T_VA:T_VA + KV_WIDTH_A]
            kb = pt[:, T_KB:T_KB + WIDTH_B]
            vb = pt[:, T_VB:T_VB + WIDTH_B]
            ka = ka * lax.rsqrt(_dot((ka * ka).astype(_BF16), bd64_ref[...]) + RMS_EPS) * gka_ref[l]
            kb = kb * lax.rsqrt(_dot((kb * kb).astype(_BF16), bd32_ref[...]) + RMS_EPS) * gkb_ref[l]
            if latent:
                ka = rope_k(ka, cka_ref, skpa_ref, skma_ref, rows, HEAD_DIM // 4)
                kb = rope_k(kb, ckb_ref, skpb_ref, skmb_ref, rows, DIFF_DIM // 4)
            else:
                for i in range(nb):
                    r = slice(i * seq, (i + 1) * seq)
                    newa_ref[i, 0, :, 0:KV_WIDTH_A] = ka[r]
                    newa_ref[i, 0, :, KV_WIDTH_A:2 * KV_WIDTH_A] = va[r]
                    newb_ref[i, 0, :, 0:WIDTH_B] = kb[r]
                    newb_ref[i, 0, :, WIDTH_B:2 * WIDTH_B] = vb[r]
            ka_s[rows, :] = ka.astype(_BF16)
            kb_s[rows, :] = kb.astype(_BF16)
            for i in range(nb):
                r = slice(i * seq, (i + 1) * seq) if nb > 1 else slice(None)
                note_key_norms(i, ka[r], kb[r], first=(c == 0))
            put_values(jnp.concatenate([va.T, vb.T], axis=0).astype(_BF16), rows)
            uct = _dot_nt(wft_ref[l, F_UC:F_ROWS, :], hc).astype(_BF16)
            t1_s[:, rows] = _dot(bc_ref[...], uct).astype(_BF16)
            t2_s[:, rows] = _dot(bs_ref[...], uct).astype(_BF16)
        if latent:
            ca = ctxa_ref[0, 0]
            cb = ctxb_ref[0, 0]
            ka_s[grp:grp + past, :] = ca[:, 0:KV_WIDTH_A].astype(_BF16)
            kb_s[grp:grp + past, :] = cb[:, 0:WIDTH_B].astype(_BF16)
            note_key_norms(0, ca[:, 0:KV_WIDTH_A], cb[:, 0:WIDTH_B], first=False)
            put_values(jnp.concatenate([ca[:, KV_WIDTH_A:].T, cb[:, WIDTH_B:].T], axis=0).astype(_BF16),
                       slice(grp, grp + past))

    blk_rows = pl.ds(pl.multiple_of(j * TOKEN_BLOCK, TOKEN_BLOCK), TOKEN_BLOCK)
    xold_s[...] = xres_s[blk_rows, :]

    pf_chunks = {}

    def pf_chunk(c):
        if c not in pf_chunks:
            pf_chunks[c] = _dot_nt(wft_ref[l, c * PF_CHUNK:(c + 1) * PF_CHUNK, :], h_s[blk_rows, :])
        return pf_chunks[c]

    def pf_rows(r0, n, cols):
        c, off = divmod(r0, PF_CHUNK)
        assert off + n <= PF_CHUNK
        return pf_chunk(c)[off:off + n, cols]

    lv = lam_ref[l]
    lam_init = jnp.where(l == 0, LAM_INIT[0], LAM_INIT[1])
    lam = (jnp.exp(jnp.sum(lv[0:1] * lv[1:2], axis=-1, keepdims=True))
           - jnp.exp(jnp.sum(lv[2:3] * lv[3:4], axis=-1, keepdims=True)) + lam_init)

    def q_tile(r0, dim, gain_ref, cos_ref, sin_ref, cols, pad_start, pad_total, kn2_s):
        q = pf_rows(r0, dim, cols)
        q = q * _rms_rows(q) * gain_ref[l]
        if latent:
            q = q * cos_ref[...] + _swap_halves(q, dim // 4) * sin_ref[...]
        i = cols.start // tb if nb > 1 else 0
        kn2 = kn2_s[i:i + 1, pad_start:pad_start + 1]
        bound = jnp.sqrt(jnp.sum(q * q, axis=0, keepdims=True) * kn2) * BOUND_SLACK
        return _pad_rows(q.astype(_BF16), pad_start, pad_total), bound

    def q_a(h, cols):
        g = h // (N_HEADS_A // N_KV_A)
        return q_tile(F_QA + h * HEAD_DIM, HEAD_DIM, gqa_ref, cqa_ref if latent else None,
                      sqa_ref if latent else None, cols, g * HEAD_DIM, KV_WIDTH_A, kn2a_s)

    def q_b(h, comp, cols):
        r0 = h * HEAD_DIM + comp * DIFF_DIM
        return q_tile(F_QB + r0, DIFF_DIM, gqb_ref, cqb_ref if latent else None,
                      sqb_ref if latent else None, cols, r0, WIDTH_B, kn2b_s)

    def put_mix(gated_t, cols, c0):
        mix_s[cols, c0:c0 + gated_t.shape[0]] = gated_t.T.astype(_BF16)

    half = {}

    def put_head(slot, cols, gated_t):
        if slot % 2 == 0:
            half[cols.start] = gated_t
        else:
            put_mix(jnp.concatenate([half.pop(cols.start), gated_t], axis=0), cols, (slot - 1) * HEAD_DIM)

    def store_a(h, cols, o):
        ga = pf_rows(F_GA + h * HEAD_DIM, HEAD_DIM, cols)
        put_head(h, cols, o * _silu(ga))

    def store_b(h, cols, o1, o2):
        ob = o1 - lam * o2
        ob = ob * _rms_rows(ob) * gsub_ref[l]
        gb = pf_rows(F_GB + h * HEAD_DIM, HEAD_DIM, cols)
        put_head(N_HEADS_A + h, cols, ob * _silu(gb))

    def v_group(g):
        return slice(g * V_GROUP_ROWS, (g + 1) * V_GROUP_ROWS)

    units = []
    all_cols = slice(0, TOKEN_BLOCK)
    if nb == 1:
        keys = slice(0, grp + past)
        for h in range(N_HEADS_A):
            g = h // (N_HEADS_A // N_KV_A)
            units.append(("a", keys, v_group(g),
                          [functools.partial(q_a, h, all_cols)],
                          functools.partial(store_a, h, all_cols)))
        for h in range(N_HEADS_B):
            v_rows = v_group(N_KV_A + h)
            held = {}
            for comp in range(2):
                def fin(o, h=h, comp=comp, held=held):
                    held[comp] = o
                    if comp == 1:
                        store_b(h, all_cols, held[0], held[1])
                units.append(("b", keys, v_rows, [functools.partial(q_b, h, comp, all_cols)], fin))
    else:
        seqs = [(slice(i * seq, (i + 1) * seq), slice(i * tb, (i + 1) * tb)) for i in range(nb)]
        for u in range(N_HEADS_A // 2):
            g = (2 * u) // (N_HEADS_A // N_KV_A)
            for keys, cols in seqs:
                def fin(o, u=u, cols=cols):
                    store_a(2 * u, cols, o[:, 0:tb])
                    store_a(2 * u + 1, cols, o[:, tb:2 * tb])
                units.append(("a", keys, v_group(g),
                              [functools.partial(q_a, 2 * u, cols), functools.partial(q_a, 2 * u + 1, cols)], fin))
        for h in range(N_HEADS_B):
            for keys, cols in seqs:
                def fin(o, h=h, cols=cols):
                    store_b(h, cols, o[:, 0:tb], o[:, tb:2 * tb])
                units.append(("b", keys, v_group(N_KV_A + h),
                              [functools.partial(q_b, h, 0, cols), functools.partial(q_b, h, 1, cols)], fin))

    def scores(unit):
        kind, keys, _, tiles, _ = unit
        k_ref = ka_s if kind == "a" else kb_s
        qs, bounds = zip(*[t() for t in tiles])
        return _dot(k_ref[keys, :], _cat(qs, 1)), _cat(bounds, 1)

    def attend(unit, s_t, shift):
        o = _dot(vt_s[unit[2], unit[1]], _softmax_t(s_t, shift))
        den = o[HEAD_DIM:HEAD_DIM + 1]
        unit[4](o[0:HEAD_DIM] / den)
        return den

    def mixer_c():
        if nb == 1:
            ft = _dot(t1_s[...], cs_ref[...]) + _dot(t2_s[...], nss_ref[...])
        else:
            ft = _cat([_dot(t1_s[:, i * seq:(i + 1) * seq], cs_ref[...])
                       + _dot(t2_s[:, i * seq:(i + 1) * seq], nss_ref[...]) for i in range(nb)], 1)
        oc = _dot(wct_ref[l], ft.astype(_BF16))
        put_mix(oc * _silu(pf_rows(F_GC, WIDTH_C, all_cols)), all_cols, WIDTH_A + WIDTH_B)

    out_parts = []

    def out_chunk(k):
        cols = slice(k * OUT_CHUNK, (k + 1) * OUT_CHUNK)
        out_parts.append(_dot(mix_s[:, cols], wout_ref[l, cols, :]))

    fillers = {
        0: [functools.partial(pf_chunk, F_GA // PF_CHUNK)],
        1: [functools.partial(pf_chunk, F_QA // PF_CHUNK + 1)],
        2: [functools.partial(pf_chunk, F_GA // PF_CHUNK + 1)],
        3: [functools.partial(pf_chunk, F_QB // PF_CHUNK)],
        4: [functools.partial(pf_chunk, F_GB // PF_CHUNK)],
        5: [mixer_c],
        6: [functools.partial(out_chunk, 3), functools.partial(out_chunk, 0)],
        9: [functools.partial(out_chunk, 1)],
    }
    s_next = scores(units[0])
    den_floor = None
    for u, unit in enumerate(units):
        s_cur, bound = s_next
        if u + 1 < len(units):
            s_next = scores(units[u + 1])
        for work in fillers.get(u, ()):
            work()
        den = attend(unit, s_cur, bound)
        den_floor = den if den_floor is None else jnp.minimum(den_floor, den)
    out_chunk(2)

    def commit(update):
        y = xold_s[...] + gate * update
        xres_s[blk_rows, :] = y

        @pl.when(l == DEPTH - 1)
        def _():
            y_ref[...] = y.reshape(y_ref.shape)

    commit(functools.reduce(lambda a, b: a + b, out_parts))

    @pl.when(jnp.logical_not(jnp.min(den_floor) >= DEN_MIN))
    def _():
        pf_chunks.clear()
        for unit in units:
            attend(unit, scores(unit)[0], None)
        commit(_dot(mix_s[...], wout_ref[l]))


def _const_spec(shape):
    nd = len(shape)
    return pl.BlockSpec(shape, lambda b, l, j: (0,) * nd, pipeline_mode=pl.Buffered(1))


def _mixer(x, mod, mod_row, wts, consts, nb, ctx=None):
    latent = ctx is not None
    n_batch, seq, _ = x.shape
    grp = nb * seq
    n_blk = grp // TOKEN_BLOCK
    assert grp % TOKEN_BLOCK == 0 and (nb == 1 or grp == TOKEN_BLOCK) and n_batch % nb == 0 and nb <= SEQ_ROWS
    past = ctx[0].shape[2] if latent else 0
    last = DEPTH - 1

    args = [x, mod]
    specs = [
        pl.BlockSpec((nb, seq, D_MODEL), lambda b, l, j: (b, 0, 0),
                     pipeline_mode=pl.Buffered(1) if latent else None),
        pl.BlockSpec((1, 3, D_MODEL), lambda b, l, j: (l * MOD_ROWS + mod_row(b), 0, 0)),
    ]
    for name in ("norm_g", "wft", "wt", "wct", "wout", "gqa", "gka", "gqb", "gkb", "gsub", "lam"):
        args.append(wts[name])
        specs.append(_const_spec(wts[name].shape))
    for name in ("bd64", "bd32", "bc", "bs"):
        args.append(consts[name])
        specs.append(_const_spec(consts[name].shape))
    dft_cols = TOKEN_BLOCK if nb == 1 else seq
    for name in ("cs", "nss"):
        args.append(consts[name])
        specs.append(pl.BlockSpec((seq, dft_cols), lambda b, l, j: (0, j)))
    if latent:
        args += [ctx[0], ctx[1]]
        specs += [
            pl.BlockSpec((1, 1, past, 2 * KV_WIDTH_A), lambda b, l, j: (b, l, 0, 0), pipeline_mode=pl.Buffered(1)),
            pl.BlockSpec((1, 1, past, 2 * WIDTH_B), lambda b, l, j: (b, l, 0, 0), pipeline_mode=pl.Buffered(1)),
        ]
        for name, rows in (("cqa", HEAD_DIM), ("sqa", HEAD_DIM), ("cqb", DIFF_DIM), ("sqb", DIFF_DIM)):
            args.append(consts[name])
            specs.append(pl.BlockSpec((rows, TOKEN_BLOCK), lambda b, l, j: (0, j)))
        for name in ("cka", "skpa", "skma", "ckb", "skpb", "skmb"):
            args.append(consts[name])
            specs.append(_const_spec(consts[name].shape))

    y_shape = jax.ShapeDtypeStruct(x.shape, _F32)
    if nb == 1:
        y_spec = pl.BlockSpec((1, TOKEN_BLOCK, D_MODEL), lambda b, l, j: (b, jnp.where(l == last, j, 0), 0))
    else:
        y_spec = pl.BlockSpec((nb, seq, D_MODEL), lambda b, l, j: (b, 0, 0))
    if latent:
        out_shape, out_specs = y_shape, y_spec
    else:
        out_shape = (y_shape,
                     jax.ShapeDtypeStruct((n_batch, DEPTH, seq, 2 * KV_WIDTH_A), _F32),
                     jax.ShapeDtypeStruct((n_batch, DEPTH, seq, 2 * WIDTH_B), _F32))
        out_specs = (y_spec,
                     pl.BlockSpec((nb, 1, seq, 2 * KV_WIDTH_A), lambda b, l, j: (b, l, 0, 0)),
                     pl.BlockSpec((nb, 1, seq, 2 * WIDTH_B), lambda b, l, j: (b, l, 0, 0)))

    n_keys = grp + past
    scratch = [
        pltpu.VMEM((grp, D_MODEL), _F32),
        pltpu.VMEM((grp, D_MODEL), _BF16),
        pltpu.VMEM((n_keys, KV_WIDTH_A), _BF16),
        pltpu.VMEM((n_keys, WIDTH_B), _BF16),
        pltpu.VMEM((N_V_GROUPS * V_GROUP_ROWS, n_keys), _BF16),
        pltpu.VMEM((WIDTH_C, grp), _BF16),
        pltpu.VMEM((WIDTH_C, grp), _BF16),
        pltpu.VMEM((TOKEN_BLOCK, D_MODEL), _BF16),
        pltpu.VMEM((SEQ_ROWS, KV_WIDTH_A), _F32),
        pltpu.VMEM((SEQ_ROWS, WIDTH_B), _F32),
        pltpu.VMEM((TOKEN_BLOCK, D_MODEL), _F32),
    ]
    kern = functools.partial(_mixer_kernel, latent=latent, seq=seq, nb=nb, past=past)
    return pl.pallas_call(
        kern,
        grid=(n_batch // nb, DEPTH, n_blk),
        in_specs=specs,
        out_specs=out_specs,
        out_shape=out_shape,
        scratch_shapes=scratch,
        compiler_params=pltpu.CompilerParams(
            dimension_semantics=("arbitrary", "arbitrary", "arbitrary"),
            vmem_limit_bytes=V7X_VMEM_LIMIT_BYTES),
        name="latent_mixer" if latent else "context_mixer",
    )(*args)


def _rope_tables(n_tok, dim):
    m = dim // 4
    t = np.arange(n_tok)
    inv = 1.0 / (ROPE_BASE ** (np.arange(m, dtype=np.float64) / m))
    ar = (t // GRID_W)[:, None] * inv
    ac = (t % GRID_W)[:, None] * inv
    cos = np.concatenate([np.cos(ar), np.cos(ar), np.cos(ac), np.cos(ac)], axis=1)
    sin = np.concatenate([-np.sin(ar), np.sin(ar), -np.sin(ac), np.sin(ac)], axis=1)
    first = np.tile(np.concatenate([np.ones(m), np.zeros(m)]), 2)[None, :]
    return cos.astype(np.float32), sin.astype(np.float32), first.astype(np.float32)


def _block_diag(block, n):
    out = np.zeros((block.shape[0] * n, block.shape[1] * n), np.float64)
    for i in range(n):
        out[i * block.shape[0]:(i + 1) * block.shape[0], i * block.shape[1]:(i + 1) * block.shape[1]] = block
    return out


def _dft_cos_sin(n):
    k = np.arange(n)
    ang = 2.0 * np.pi * ((k[:, None] * k[None, :]) % n) / n
    return np.cos(ang), np.sin(ang)


def _mixer_consts(seq, latent):
    c64, s64 = _dft_cos_sin(GROUP_C)
    norm = 1.0 / math.sqrt(GROUP_C * seq)
    cs, ss = _dft_cos_sin(seq)
    f32c = lambda a: jnp.asarray(np.asarray(a, np.float32))
    consts = {
        "bd64": f32c(_block_diag(np.full((HEAD_DIM, HEAD_DIM), 1.0 / HEAD_DIM), N_KV_A)).astype(_BF16),
        "bd32": f32c(_block_diag(np.full((DIFF_DIM, DIFF_DIM), 1.0 / DIFF_DIM), 2 * N_HEADS_B)).astype(_BF16),
        "bc": f32c(_block_diag(c64, N_GROUPS_C) * norm).astype(_BF16),
        "bs": f32c(_block_diag(s64, N_GROUPS_C) * norm).astype(_BF16),
        "cs": f32c(cs).astype(_BF16),
        "nss": f32c(-ss).astype(_BF16),
    }
    if latent:
        cos_a, sin_a, first_a = _rope_tables(seq, HEAD_DIM)
        cos_b, sin_b, first_b = _rope_tables(seq, DIFF_DIM)
        rep_a, rep_b = LANES // HEAD_DIM, LANES // DIFF_DIM
        consts.update({
            "cqa": f32c(cos_a.T), "sqa": f32c(sin_a.T), "cqb": f32c(cos_b.T), "sqb": f32c(sin_b.T),
            "cka": f32c(np.tile(cos_a, (1, rep_a))),
            "skpa": f32c(np.tile(sin_a * (1.0 - first_a), (1, rep_a))),
            "skma": f32c(np.tile(sin_a * first_a, (1, rep_a))),
            "ckb": f32c(np.tile(cos_b, (1, rep_b))),
            "skpb": f32c(np.tile(sin_b * (1.0 - first_b), (1, rep_b))),
            "skmb": f32c(np.tile(sin_b * first_b, (1, rep_b))),
        })
    return consts


def _mixer_weights(norm_g, w_in, q_norm_a, k_norm_a, q_norm_b, k_norm_b,
                   lambda_q1, lambda_k1, lambda_q2, lambda_k2, subln_g, w_fourier, w_out):
    keep = jnp.asarray([1.0 - v for v in LAM_INIT], _F32)
    return {
        "norm_g": norm_g.reshape(DEPTH, 1, D_MODEL),
        "wft": _relayout(w_in, FEATURE_TILES, True, "w_in_feature_major"),
        "wt": _relayout(w_in, TOKEN_TILES, False, "w_in_token_major"),
        "wct": _relayout(w_fourier, (0,), True, "w_fourier_t"),
        "wout": _relayout(w_out, tuple(range(D_MODEL // WEIGHT_TILE)), False, "w_out_bf16"),
        "gqa": (q_norm_a * (HEAD_DIM ** -0.5 * LOG2_E)).reshape(DEPTH, HEAD_DIM, 1),
        "gka": jnp.tile(k_norm_a, (1, N_KV_A)).reshape(DEPTH, 1, KV_WIDTH_A),
        "gqb": (q_norm_b * (DIFF_DIM ** -0.5 * LOG2_E)).reshape(DEPTH, DIFF_DIM, 1),
        "gkb": jnp.tile(k_norm_b, (1, 2 * N_HEADS_B)).reshape(DEPTH, 1, WIDTH_B),
        "gsub": (subln_g * keep[:, None]).reshape(DEPTH, HEAD_DIM, 1),
        "lam": jnp.stack([lambda_q1, lambda_k1, lambda_q2, lambda_k2], axis=1),
    }


def kernel(x_prompt, x_sample, cache_attn_a, cache_attn_b, c, c_ctx, norm_g, w_mod, b_mod, w_in,
           q_norm_a, k_norm_a, q_norm_b, k_norm_b, lambda_q1, lambda_k1, lambda_q2, lambda_k2,
           subln_g, w_fourier, w_out):
    n_ctx_batch, ctx_seq, _ = x_prompt.shape
    n_lat_batch, lat_seq, _ = x_sample.shape
    past = cache_attn_a.shape[2]
    assert n_lat_batch + 1 <= MOD_ROWS and w_in.shape == (DEPTH, D_MODEL, D_IN)

    cond = jnp.concatenate(
        [c_ctx[None, :], c, jnp.zeros((MOD_ROWS - 1 - n_lat_batch, D_MODEL), _F32)], axis=0)
    mod = _modulation(cond, w_mod, b_mod).reshape(DEPTH * MOD_ROWS, 3, D_MODEL)

    weights = _mixer_weights(norm_g, w_in, q_norm_a, k_norm_a, q_norm_b, k_norm_b,
                             lambda_q1, lambda_k1, lambda_q2, lambda_k2, subln_g, w_fourier, w_out)

    y_prompt, new_a, new_b = _mixer(x_prompt, mod, lambda b: 0, weights,
                                    _mixer_consts(ctx_seq, latent=False), nb=TOKEN_BLOCK // ctx_seq)
    new_attn_a = new_a.reshape(n_ctx_batch, DEPTH, ctx_seq, 2, N_KV_A, HEAD_DIM)
    new_attn_b = new_b.reshape(n_ctx_batch, DEPTH, ctx_seq, 2, N_HEADS_B, HEAD_DIM)

    ctx_a = cache_attn_a.reshape(n_lat_batch, DEPTH, past, 2 * KV_WIDTH_A)
    ctx_b = cache_attn_b.reshape(n_lat_batch, DEPTH, past, 2 * WIDTH_B)
    y_sample = _mixer(x_sample, mod, lambda b: b + 1, weights,
                      _mixer_consts(lat_seq, latent=True), nb=1, ctx=(ctx_a, ctx_b))
    return (y_prompt, y_sample, new_attn_a, new_attn_b)
```

```python
import functools
import math

import jax
import jax.numpy as jnp
import numpy as np
from jax import lax
from jax.experimental import pallas as pl
from jax.experimental.pallas import tpu as pltpu

D_MODEL = 1024
DEPTH = 2
GRID_W = 64
HEAD_DIM = 64
N_HEADS_A = 8
N_KV_A = 2
N_HEADS_B = 4
DIFF_DIM = 32
GROUP_C = 64
N_GROUPS_C = 4
WIDTH_A = N_HEADS_A * HEAD_DIM
WIDTH_B = N_HEADS_B * HEAD_DIM
WIDTH_C = N_GROUPS_C * GROUP_C
KV_WIDTH_A = N_KV_A * HEAD_DIM
D_IN = 2 * WIDTH_A + 2 * KV_WIDTH_A + 4 * WIDTH_B + 2 * WIDTH_C
RMS_EPS = 1e-6
ROPE_BASE = 10000.0
LOG2_E = math.log2(math.e)
LAM_INIT = tuple(0.8 - 0.6 * math.exp(-0.3 * l) for l in range(DEPTH))

LANES = 128
TOKEN_BLOCK = 512
MOD_ROWS = 8
WEIGHT_TILE = 256
V7X_VMEM_LIMIT_BYTES = 58 * 1024 * 1024
BOUND_SLACK = 1.02
MAX_SCORE_BOUND = 40.0
GUARD_SLACK = 1.05
SEQ_ROWS = 8

FEATURE_TILES = (0, 1, 3, 4, 5, 8, 10, 9)
TOKEN_TILES = (2, 6, 7)
F_QA, F_GA, F_QB, F_GB, F_GC, F_UC = 0, 512, 1024, 1280, 1536, 1792
PF_CHUNK = 256
OUT_CHUNK = 256
F_ROWS = 2048
T_KA, T_VA, T_KB, T_VB = 0, 128, 256, 512
T_COLS = 768
ONES_ROWS = 16
V_GROUP_ROWS = HEAD_DIM + ONES_ROWS
N_V_GROUPS = N_KV_A + N_HEADS_B

_BF16 = jnp.bfloat16
_F32 = jnp.float32


def _silu(x):
    return x * (1.0 / (1.0 + jnp.exp(-x)))


def _dot(a, b):
    return jnp.dot(a, b, preferred_element_type=_F32)


def _dot_nt(a, b):
    return lax.dot_general(a, b, (((1,), (1,)), ((), ())), preferred_element_type=_F32)


def _mod_kernel(c_ref, w_ref, b_ref, o_ref):
    c = c_ref[...]
    o_ref[0] = _dot(_silu(c).astype(_BF16), w_ref[0].astype(_BF16)) + b_ref[0]


def _modulation(cond, w_mod, b_mod):
    n_tile = D_MODEL
    return pl.pallas_call(
        _mod_kernel,
        grid=(DEPTH, 3 * D_MODEL // n_tile),
        in_specs=[
            pl.BlockSpec((MOD_ROWS, D_MODEL), lambda l, n: (0, 0)),
            pl.BlockSpec((1, D_MODEL, n_tile), lambda l, n: (l, 0, n)),
            pl.BlockSpec((1, 1, n_tile), lambda l, n: (l, 0, n)),
        ],
        out_specs=pl.BlockSpec((1, MOD_ROWS, n_tile), lambda l, n: (l, 0, n)),
        out_shape=jax.ShapeDtypeStruct((DEPTH, MOD_ROWS, 3 * D_MODEL), _F32),
        name="modulation",
    )(cond, w_mod, b_mod.reshape(DEPTH, 1, 3 * D_MODEL))


def _relayout_kernel(tiles_ref, w_ref, o_ref, *, transpose):
    del tiles_ref
    w = w_ref[0]
    o_ref[0] = (w.T if transpose else w).astype(_BF16)


def _relayout(w, tiles, transpose, name):
    depth, rows, _ = w.shape
    n = len(tiles)
    if transpose:
        out_shape = (depth, n * WEIGHT_TILE, rows)
        out_spec = pl.BlockSpec((1, WEIGHT_TILE, rows), lambda l, i, t: (l, i, 0))
    else:
        out_shape = (depth, rows, n * WEIGHT_TILE)
        out_spec = pl.BlockSpec((1, rows, WEIGHT_TILE), lambda l, i, t: (l, 0, i))
    return pl.pallas_call(
        functools.partial(_relayout_kernel, transpose=transpose),
        grid_spec=pltpu.PrefetchScalarGridSpec(
            num_scalar_prefetch=1,
            grid=(depth, n),
            in_specs=[pl.BlockSpec((1, rows, WEIGHT_TILE), lambda l, i, t: (l, 0, t[i]))],
            out_specs=out_spec),
        out_shape=jax.ShapeDtypeStruct(out_shape, _BF16),
        name=name,
    )(jnp.asarray(tiles, jnp.int32), w)


def _softmax_t(s_t, shift=None):
    if shift is None:
        shift = jnp.max(s_t, axis=0, keepdims=True)
    return jnp.exp2(s_t - shift).astype(_BF16)


def _rms_rows(x):
    return lax.rsqrt(jnp.mean(x * x, axis=0, keepdims=True) + RMS_EPS)


def _swap_halves(x, m):
    return jnp.concatenate([x[m:2 * m], x[0:m], x[3 * m:4 * m], x[2 * m:3 * m]], axis=0)


def _pad_rows(x, start, total):
    parts = []
    if start:
        parts.append(jnp.zeros((start, x.shape[1]), x.dtype))
    parts.append(x)
    rest = total - start - x.shape[0]
    if rest:
        parts.append(jnp.zeros((rest, x.shape[1]), x.dtype))
    return jnp.concatenate(parts, axis=0)


def _cat(parts, axis):
    return parts[0] if len(parts) == 1 else jnp.concatenate(parts, axis=axis)


def _mixer_kernel(*refs, latent, seq, nb, past, exact):
    grp = nb * seq
    n_blk = grp // TOKEN_BLOCK
    tb = TOKEN_BLOCK // nb
    it = iter(refs)
    x_ref, mod_ref, ng_ref = next(it), next(it), next(it)
    wft_ref, wt_ref, wct_ref, wout_ref = next(it), next(it), next(it), next(it)
    gqa_ref, gka_ref, gqb_ref, gkb_ref, gsub_ref, lam_ref = (next(it) for _ in range(6))
    bd64_ref, bd32_ref, bc_ref, bs_ref, cs_ref, nss_ref = (next(it) for _ in range(6))
    if latent:
        ctxa_ref, ctxb_ref = next(it), next(it)
        cqa_ref, sqa_ref, cqb_ref, sqb_ref = (next(it) for _ in range(4))
        cka_ref, skpa_ref, skma_ref, ckb_ref, skpb_ref, skmb_ref = (next(it) for _ in range(6))
        y_ref = next(it)
    else:
        y_ref, newa_ref, newb_ref = next(it), next(it), next(it)
    xres_s, h_s, ka_s, kb_s, vt_s, t1_s, t2_s, mix_s, kn2a_s, kn2b_s = (next(it) for _ in range(10))

    l = pl.program_id(1)
    j = pl.program_id(2)
    shift = mod_ref[0, 0:1, :]
    scale = mod_ref[0, 1:2, :]
    gate = mod_ref[0, 2:3, :]

    @pl.when((l == 0) & (j == 0))
    def _():
        xres_s[...] = x_ref[...].reshape(grp, D_MODEL)
        for g in range(N_V_GROUPS):
            vt_s[g * V_GROUP_ROWS + HEAD_DIM:(g + 1) * V_GROUP_ROWS, :] = jnp.ones(
                (ONES_ROWS, grp + past), _BF16)

    def rope_k(k, cos_ref, sp_ref, sm_ref, rows, m):
        out = []
        for c0 in range(0, k.shape[1], LANES):
            kc = k[:, c0:c0 + LANES]
            out.append(kc * cos_ref[rows, :] + pltpu.roll(kc, m, 1) * sp_ref[rows, :]
                       + pltpu.roll(kc, LANES - m, 1) * sm_ref[rows, :])
        return _cat(out, 1)

    def note_key_norms(i, ka, kb, first):
        for k, bd_ref, dim, kn2_s in ((ka, bd64_ref, HEAD_DIM, kn2a_s), (kb, bd32_ref, DIFF_DIM, kn2b_s)):
            n2 = jnp.max(_dot((k * k).astype(_BF16), bd_ref[...]), axis=0, keepdims=True) * float(dim)
            kn2_s[i:i + 1, :] = n2 if first else jnp.maximum(kn2_s[i:i + 1, :], n2)

    def put_values(vt, cols):
        for g in range(N_V_GROUPS):
            vt_s[g * V_GROUP_ROWS:g * V_GROUP_ROWS + HEAD_DIM, cols] = vt[g * HEAD_DIM:(g + 1) * HEAD_DIM]

    @pl.when(j == 0)
    def _prep():
        for c in range(n_blk):
            rows = slice(c * TOKEN_BLOCK, (c + 1) * TOKEN_BLOCK)
            xc = xres_s[rows, :]
            ms = jnp.mean(xc * xc, axis=-1, keepdims=True)
            hc = (xc * lax.rsqrt(ms + RMS_EPS) * ng_ref[l] * (1.0 + scale) + shift).astype(_BF16)
            h_s[rows, :] = hc
            pt = _dot(hc, wt_ref[l])
            ka = pt[:, T_KA:T_KA + KV_WIDTH_A]
            va = pt[:, T_VA:T_VA + KV_WIDTH_A]
            kb = pt[:, T_KB:T_KB + WIDTH_B]
            vb = pt[:, T_VB:T_VB + WIDTH_B]
            ka = ka * lax.rsqrt(_dot((ka * ka).astype(_BF16), bd64_ref[...]) + RMS_EPS) * gka_ref[l]
            kb = kb * lax.rsqrt(_dot((kb * kb).astype(_BF16), bd32_ref[...]) + RMS_EPS) * gkb_ref[l]
            if latent:
                ka = rope_k(ka, cka_ref, skpa_ref, skma_ref, rows, HEAD_DIM // 4)
                kb = rope_k(kb, ckb_ref, skpb_ref, skmb_ref, rows, DIFF_DIM // 4)
            else:
                for i in range(nb):
                    r = slice(i * seq, (i + 1) * seq)
                    newa_ref[i, 0, :, 0:KV_WIDTH_A] = ka[r]
                    newa_ref[i, 0, :, KV_WIDTH_A:2 * KV_WIDTH_A] = va[r]
                    newb_ref[i, 0, :, 0:WIDTH_B] = kb[r]
                    newb_ref[i, 0, :, WIDTH_B:2 * WIDTH_B] = vb[r]
            ka_s[rows, :] = ka.astype(_BF16)
            kb_s[rows, :] = kb.astype(_BF16)
            if not exact:
                for i in range(nb):
                    r = slice(i * seq, (i + 1) * seq) if nb > 1 else slice(None)
                    note_key_norms(i, ka[r], kb[r], first=(c == 0))
            put_values(jnp.concatenate([va.T, vb.T], axis=0).astype(_BF16), rows)
            uct = _dot_nt(wft_ref[l, F_UC:F_ROWS, :], hc).astype(_BF16)
            t1_s[:, rows] = _dot(bc_ref[...], uct).astype(_BF16)
            t2_s[:, rows] = _dot(bs_ref[...], uct).astype(_BF16)
        if latent:
            ca = ctxa_ref[0, 0]
            cb = ctxb_ref[0, 0]
            ka_s[grp:grp + past, :] = ca[:, 0:KV_WIDTH_A].astype(_BF16)
            kb_s[grp:grp + past, :] = cb[:, 0:WIDTH_B].astype(_BF16)
            if not exact:
                note_key_norms(0, ca[:, 0:KV_WIDTH_A], cb[:, 0:WIDTH_B], first=False)
            put_values(jnp.concatenate([ca[:, KV_WIDTH_A:].T, cb[:, WIDTH_B:].T], axis=0).astype(_BF16),
                       slice(grp, grp + past))

    blk_rows = pl.ds(pl.multiple_of(j * TOKEN_BLOCK, TOKEN_BLOCK), TOKEN_BLOCK)

    pf_chunks = {}

    def pf_chunk(c):
        if c not in pf_chunks:
            pf_chunks[c] = _dot_nt(wft_ref[l, c * PF_CHUNK:(c + 1) * PF_CHUNK, :], h_s[blk_rows, :])
        return pf_chunks[c]

    def pf_rows(r0, n, cols):
        c, off = divmod(r0, PF_CHUNK)
        assert off + n <= PF_CHUNK
        return pf_chunk(c)[off:off + n, cols]

    lv = lam_ref[l]
    lam_init = jnp.where(l == 0, LAM_INIT[0], LAM_INIT[1])
    lam = (jnp.exp(jnp.sum(lv[0:1] * lv[1:2], axis=-1, keepdims=True))
           - jnp.exp(jnp.sum(lv[2:3] * lv[3:4], axis=-1, keepdims=True)) + lam_init)

    def q_tile(r0, dim, gain_ref, cos_ref, sin_ref, cols, pad_start, pad_total, kn2_s):
        q = pf_rows(r0, dim, cols)
        q = q * _rms_rows(q) * gain_ref[l]
        if latent:
            q = q * cos_ref[...] + _swap_halves(q, dim // 4) * sin_ref[...]
        bound = None
        if not exact:
            i = cols.start // tb if nb > 1 else 0
            kn2 = kn2_s[i:i + 1, pad_start:pad_start + 1]
            bound = jnp.sqrt(jnp.sum(q * q, axis=0, keepdims=True) * kn2) * BOUND_SLACK
        return _pad_rows(q.astype(_BF16), pad_start, pad_total), bound

    def q_a(h, cols):
        g = h // (N_HEADS_A // N_KV_A)
        return q_tile(F_QA + h * HEAD_DIM, HEAD_DIM, gqa_ref, cqa_ref if latent else None,
                      sqa_ref if latent else None, cols, g * HEAD_DIM, KV_WIDTH_A, kn2a_s)

    def q_b(h, comp, cols):
        r0 = h * HEAD_DIM + comp * DIFF_DIM
        return q_tile(F_QB + r0, DIFF_DIM, gqb_ref, cqb_ref if latent else None,
                      sqb_ref if latent else None, cols, r0, WIDTH_B, kn2b_s)

    def put_mix(gated_t, cols, c0):
        mix_s[cols, c0:c0 + gated_t.shape[0]] = gated_t.T.astype(_BF16)

    half = {}

    def put_head(slot, cols, gated_t):
        if slot % 2 == 0:
            half[cols.start] = gated_t
        else:
            put_mix(jnp.concatenate([half.pop(cols.start), gated_t], axis=0), cols, (slot - 1) * HEAD_DIM)

    def store_a(h, cols, o):
        ga = pf_rows(F_GA + h * HEAD_DIM, HEAD_DIM, cols)
        put_head(h, cols, o * _silu(ga))

    def store_b(h, cols, o1, o2):
        ob = o1 - lam * o2
        ob = ob * _rms_rows(ob) * gsub_ref[l]
        gb = pf_rows(F_GB + h * HEAD_DIM, HEAD_DIM, cols)
        put_head(N_HEADS_A + h, cols, ob * _silu(gb))

    def v_group(g):
        return slice(g * V_GROUP_ROWS, (g + 1) * V_GROUP_ROWS)

    units = []
    all_cols = slice(0, TOKEN_BLOCK)
    if nb == 1:
        keys = slice(0, grp + past)
        for h in range(N_HEADS_A):
            g = h // (N_HEADS_A // N_KV_A)
            units.append(("a", keys, v_group(g),
                          [functools.partial(q_a, h, all_cols)],
                          functools.partial(store_a, h, all_cols)))
        for h in range(N_HEADS_B):
            v_rows = v_group(N_KV_A + h)
            held = {}
            for comp in range(2):
                def fin(o, h=h, comp=comp, held=held):
                    held[comp] = o
                    if comp == 1:
                        store_b(h, all_cols, held[0], held[1])
                units.append(("b", keys, v_rows, [functools.partial(q_b, h, comp, all_cols)], fin))
    else:
        seqs = [(slice(i * seq, (i + 1) * seq), slice(i * tb, (i + 1) * tb)) for i in range(nb)]
        for u in range(N_HEADS_A // 2):
            g = (2 * u) // (N_HEADS_A // N_KV_A)
            for keys, cols in seqs:
                def fin(o, u=u, cols=cols):
                    store_a(2 * u, cols, o[:, 0:tb])
                    store_a(2 * u + 1, cols, o[:, tb:2 * tb])
                units.append(("a", keys, v_group(g),
                              [functools.partial(q_a, 2 * u, cols), functools.partial(q_a, 2 * u + 1, cols)], fin))
        for h in range(N_HEADS_B):
            for keys, cols in seqs:
                def fin(o, h=h, cols=cols):
                    store_b(h, cols, o[:, 0:tb], o[:, tb:2 * tb])
                units.append(("b", keys, v_group(N_KV_A + h),
                              [functools.partial(q_b, h, 0, cols), functools.partial(q_b, h, 1, cols)], fin))

    def scores(unit):
        kind, keys, _, tiles, _ = unit
        k_ref = ka_s if kind == "a" else kb_s
        qs, shifts = zip(*[t() for t in tiles])
        shift = None if exact else _cat(shifts, 1)
        return _dot(k_ref[keys, :], _cat(qs, 1)), shift

    def attend(unit, s_t, shift):
        o = _dot(vt_s[unit[2], unit[1]], _softmax_t(s_t, shift))
        unit[4](o[0:HEAD_DIM] / o[HEAD_DIM:HEAD_DIM + 1])

    def mixer_c():
        if nb == 1:
            ft = _dot(t1_s[...], cs_ref[...]) + _dot(t2_s[...], nss_ref[...])
        else:
            ft = _cat([_dot(t1_s[:, i * seq:(i + 1) * seq], cs_ref[...])
                       + _dot(t2_s[:, i * seq:(i + 1) * seq], nss_ref[...]) for i in range(nb)], 1)
        oc = _dot(wct_ref[l], ft.astype(_BF16))
        put_mix(oc * _silu(pf_rows(F_GC, WIDTH_C, all_cols)), all_cols, WIDTH_A + WIDTH_B)

    out_parts = []

    def out_chunk(k):
        cols = slice(k * OUT_CHUNK, (k + 1) * OUT_CHUNK)
        out_parts.append(_dot(mix_s[:, cols], wout_ref[l, cols, :]))

    fillers = {
        0: [functools.partial(pf_chunk, F_GA // PF_CHUNK)],
        1: [functools.partial(pf_chunk, F_QA // PF_CHUNK + 1)],
        2: [functools.partial(pf_chunk, F_GA // PF_CHUNK + 1)],
        3: [functools.partial(pf_chunk, F_QB // PF_CHUNK)],
        4: [functools.partial(pf_chunk, F_GB // PF_CHUNK)],
        5: [mixer_c],
        6: [functools.partial(out_chunk, 3), functools.partial(out_chunk, 0)],
        9: [functools.partial(out_chunk, 1)],
    }
    s_next = scores(units[0])
    for u, unit in enumerate(units):
        s_cur, shift = s_next
        if u + 1 < len(units):
            s_next = scores(units[u + 1])
        for work in fillers.get(u, ()):
            work()
        attend(unit, s_cur, shift)
    out_chunk(2)

    y = xres_s[blk_rows, :] + gate * functools.reduce(lambda a, b: a + b, out_parts)
    xres_s[blk_rows, :] = y

    @pl.when(l == DEPTH - 1)
    def _():
        y_ref[...] = y.reshape(y_ref.shape)


def _const_spec(shape):
    nd = len(shape)
    return pl.BlockSpec(shape, lambda b, l, j: (0,) * nd, pipeline_mode=pl.Buffered(1))


def _mixer(x, mod, mod_row, wts, consts, nb, exact, ctx=None):
    latent = ctx is not None
    n_batch, seq, _ = x.shape
    grp = nb * seq
    n_blk = grp // TOKEN_BLOCK
    assert grp % TOKEN_BLOCK == 0 and (nb == 1 or grp == TOKEN_BLOCK) and n_batch % nb == 0 and nb <= SEQ_ROWS
    past = ctx[0].shape[2] if latent else 0
    last = DEPTH - 1

    args = [x, mod]
    specs = [
        pl.BlockSpec((nb, seq, D_MODEL), lambda b, l, j: (b, 0, 0),
                     pipeline_mode=pl.Buffered(1) if latent else None),
        pl.BlockSpec((1, 3, D_MODEL), lambda b, l, j: (l * MOD_ROWS + mod_row(b), 0, 0)),
    ]
    for name in ("norm_g", "wft", "wt", "wct", "wout", "gqa", "gka", "gqb", "gkb", "gsub", "lam"):
        args.append(wts[name])
        specs.append(_const_spec(wts[name].shape))
    for name in ("bd64", "bd32", "bc", "bs"):
        args.append(consts[name])
        specs.append(_const_spec(consts[name].shape))
    dft_cols = TOKEN_BLOCK if nb == 1 else seq
    for name in ("cs", "nss"):
        args.append(consts[name])
        specs.append(pl.BlockSpec((seq, dft_cols), lambda b, l, j: (0, j)))
    if latent:
        args += [ctx[0], ctx[1]]
        specs += [
            pl.BlockSpec((1, 1, past, 2 * KV_WIDTH_A), lambda b, l, j: (b, l, 0, 0), pipeline_mode=pl.Buffered(1)),
            pl.BlockSpec((1, 1, past, 2 * WIDTH_B), lambda b, l, j: (b, l, 0, 0), pipeline_mode=pl.Buffered(1)),
        ]
        for name, rows in (("cqa", HEAD_DIM), ("sqa", HEAD_DIM), ("cqb", DIFF_DIM), ("sqb", DIFF_DIM)):
            args.append(consts[name])
            specs.append(pl.BlockSpec((rows, TOKEN_BLOCK), lambda b, l, j: (0, j)))
        for name in ("cka", "skpa", "skma", "ckb", "skpb", "skmb"):
            args.append(consts[name])
            specs.append(_const_spec(consts[name].shape))

    y_shape = jax.ShapeDtypeStruct(x.shape, _F32)
    if nb == 1:
        y_spec = pl.BlockSpec((1, TOKEN_BLOCK, D_MODEL), lambda b, l, j: (b, jnp.where(l == last, j, 0), 0))
    else:
        y_spec = pl.BlockSpec((nb, seq, D_MODEL), lambda b, l, j: (b, 0, 0))
    if latent:
        out_shape, out_specs = y_shape, y_spec
    else:
        out_shape = (y_shape,
                     jax.ShapeDtypeStruct((n_batch, DEPTH, seq, 2 * KV_WIDTH_A), _F32),
                     jax.ShapeDtypeStruct((n_batch, DEPTH, seq, 2 * WIDTH_B), _F32))
        out_specs = (y_spec,
                     pl.BlockSpec((nb, 1, seq, 2 * KV_WIDTH_A), lambda b, l, j: (b, l, 0, 0)),
                     pl.BlockSpec((nb, 1, seq, 2 * WIDTH_B), lambda b, l, j: (b, l, 0, 0)))

    n_keys = grp + past
    scratch = [
        pltpu.VMEM((grp, D_MODEL), _F32),
        pltpu.VMEM((grp, D_MODEL), _BF16),
        pltpu.VMEM((n_keys, KV_WIDTH_A), _BF16),
        pltpu.VMEM((n_keys, WIDTH_B), _BF16),
        pltpu.VMEM((N_V_GROUPS * V_GROUP_ROWS, n_keys), _BF16),
        pltpu.VMEM((WIDTH_C, grp), _BF16),
        pltpu.VMEM((WIDTH_C, grp), _BF16),
        pltpu.VMEM((TOKEN_BLOCK, D_MODEL), _BF16),
        pltpu.VMEM((SEQ_ROWS, KV_WIDTH_A), _F32),
        pltpu.VMEM((SEQ_ROWS, WIDTH_B), _F32),
    ]
    kern = functools.partial(_mixer_kernel, latent=latent, seq=seq, nb=nb, past=past, exact=exact)
    return pl.pallas_call(
        kern,
        grid=(n_batch // nb, DEPTH, n_blk),
        in_specs=specs,
        out_specs=out_specs,
        out_shape=out_shape,
        scratch_shapes=scratch,
        compiler_params=pltpu.CompilerParams(
            dimension_semantics=("arbitrary", "arbitrary", "arbitrary"),
            vmem_limit_bytes=V7X_VMEM_LIMIT_BYTES),
        name=("latent_mixer" if latent else "context_mixer") + ("_exact" if exact else ""),
    )(*args)


def _rope_tables(n_tok, dim):
    m = dim // 4
    t = np.arange(n_tok)
    inv = 1.0 / (ROPE_BASE ** (np.arange(m, dtype=np.float64) / m))
    ar = (t // GRID_W)[:, None] * inv
    ac = (t % GRID_W)[:, None] * inv
    cos = np.concatenate([np.cos(ar), np.cos(ar), np.cos(ac), np.cos(ac)], axis=1)
    sin = np.concatenate([-np.sin(ar), np.sin(ar), -np.sin(ac), np.sin(ac)], axis=1)
    first = np.tile(np.concatenate([np.ones(m), np.zeros(m)]), 2)[None, :]
    return cos.astype(np.float32), sin.astype(np.float32), first.astype(np.float32)


def _block_diag(block, n):
    out = np.zeros((block.shape[0] * n, block.shape[1] * n), np.float64)
    for i in range(n):
        out[i * block.shape[0]:(i + 1) * block.shape[0], i * block.shape[1]:(i + 1) * block.shape[1]] = block
    return out


def _dft_cos_sin(n):
    k = np.arange(n)
    ang = 2.0 * np.pi * ((k[:, None] * k[None, :]) % n) / n
    return np.cos(ang), np.sin(ang)


def _mixer_consts(seq, latent):
    c64, s64 = _dft_cos_sin(GROUP_C)
    norm = 1.0 / math.sqrt(GROUP_C * seq)
    cs, ss = _dft_cos_sin(seq)
    f32c = lambda a: jnp.asarray(np.asarray(a, np.float32))
    consts = {
        "bd64": f32c(_block_diag(np.full((HEAD_DIM, HEAD_DIM), 1.0 / HEAD_DIM), N_KV_A)).astype(_BF16),
        "bd32": f32c(_block_diag(np.full((DIFF_DIM, DIFF_DIM), 1.0 / DIFF_DIM), 2 * N_HEADS_B)).astype(_BF16),
        "bc": f32c(_block_diag(c64, N_GROUPS_C) * norm).astype(_BF16),
        "bs": f32c(_block_diag(s64, N_GROUPS_C) * norm).astype(_BF16),
        "cs": f32c(cs).astype(_BF16),
        "nss": f32c(-ss).astype(_BF16),
    }
    if latent:
        cos_a, sin_a, first_a = _rope_tables(seq, HEAD_DIM)
        cos_b, sin_b, first_b = _rope_tables(seq, DIFF_DIM)
        rep_a, rep_b = LANES // HEAD_DIM, LANES // DIFF_DIM
        consts.update({
            "cqa": f32c(cos_a.T), "sqa": f32c(sin_a.T), "cqb": f32c(cos_b.T), "sqb": f32c(sin_b.T),
            "cka": f32c(np.tile(cos_a, (1, rep_a))),
            "skpa": f32c(np.tile(sin_a * (1.0 - first_a), (1, rep_a))),
            "skma": f32c(np.tile(sin_a * first_a, (1, rep_a))),
            "ckb": f32c(np.tile(cos_b, (1, rep_b))),
            "skpb": f32c(np.tile(sin_b * (1.0 - first_b), (1, rep_b))),
            "skmb": f32c(np.tile(sin_b * first_b, (1, rep_b))),
        })
    return consts


def _mixer_weights(norm_g, w_in, q_norm_a, k_norm_a, q_norm_b, k_norm_b,
                   lambda_q1, lambda_k1, lambda_q2, lambda_k2, subln_g, w_fourier, w_out):
    keep = jnp.asarray([1.0 - v for v in LAM_INIT], _F32)
    return {
        "norm_g": norm_g.reshape(DEPTH, 1, D_MODEL),
        "wft": _relayout(w_in, FEATURE_TILES, True, "w_in_feature_major"),
        "wt": _relayout(w_in, TOKEN_TILES, False, "w_in_token_major"),
        "wct": _relayout(w_fourier, (0,), True, "w_fourier_t"),
        "wout": _relayout(w_out, tuple(range(D_MODEL // WEIGHT_TILE)), False, "w_out_bf16"),
        "gqa": (q_norm_a * (HEAD_DIM ** -0.5 * LOG2_E)).reshape(DEPTH, HEAD_DIM, 1),
        "gka": jnp.tile(k_norm_a, (1, N_KV_A)).reshape(DEPTH, 1, KV_WIDTH_A),
        "gqb": (q_norm_b * (DIFF_DIM ** -0.5 * LOG2_E)).reshape(DEPTH, DIFF_DIM, 1),
        "gkb": jnp.tile(k_norm_b, (1, 2 * N_HEADS_B)).reshape(DEPTH, 1, WIDTH_B),
        "gsub": (subln_g * keep[:, None]).reshape(DEPTH, HEAD_DIM, 1),
        "lam": jnp.stack([lambda_q1, lambda_k1, lambda_q2, lambda_k2], axis=1),
    }


def kernel(x_prompt, x_sample, cache_attn_a, cache_attn_b, c, c_ctx, norm_g, w_mod, b_mod, w_in,
           q_norm_a, k_norm_a, q_norm_b, k_norm_b, lambda_q1, lambda_k1, lambda_q2, lambda_k2,
           subln_g, w_fourier, w_out):
    n_ctx_batch, ctx_seq, _ = x_prompt.shape
    n_lat_batch, lat_seq, _ = x_sample.shape
    past = cache_attn_a.shape[2]
    assert n_lat_batch + 1 <= MOD_ROWS and w_in.shape == (DEPTH, D_MODEL, D_IN)

    cond = jnp.concatenate(
        [c_ctx[None, :], c, jnp.zeros((MOD_ROWS - 1 - n_lat_batch, D_MODEL), _F32)], axis=0)
    mod = _modulation(cond, w_mod, b_mod).reshape(DEPTH * MOD_ROWS, 3, D_MODEL)

    weights = _mixer_weights(norm_g, w_in, q_norm_a, k_norm_a, q_norm_b, k_norm_b,
                             lambda_q1, lambda_k1, lambda_q2, lambda_k2, subln_g, w_fourier, w_out)

    ctx_a = cache_attn_a.reshape(n_lat_batch, DEPTH, past, 2 * KV_WIDTH_A)
    ctx_b = cache_attn_b.reshape(n_lat_batch, DEPTH, past, 2 * WIDTH_B)

    def max_norm(gain, dim):
        return math.sqrt(dim) * jnp.max(jnp.abs(gain), axis=-1)

    def cached_norm(cache, dim):
        k = cache[:, :, :, 0].reshape(n_lat_batch, DEPTH, past, -1, dim)
        return jnp.sqrt(jnp.max(jnp.sum(k * k, axis=-1), axis=(0, 2, 3)))

    q_a, q_b = max_norm(weights["gqa"][:, :, 0], HEAD_DIM), max_norm(weights["gqb"][:, :, 0], DIFF_DIM)
    k_a, k_b = max_norm(k_norm_a, HEAD_DIM), max_norm(k_norm_b, DIFF_DIM)
    ctx_bound = GUARD_SLACK * jnp.maximum(q_a * k_a, q_b * k_b)
    lat_bound = GUARD_SLACK * jnp.maximum(q_a * jnp.maximum(k_a, cached_norm(cache_attn_a, HEAD_DIM)),
                                          q_b * jnp.maximum(k_b, cached_norm(cache_attn_b, DIFF_DIM)))

    ctx_consts = _mixer_consts(ctx_seq, latent=False)
    lat_consts = _mixer_consts(lat_seq, latent=True)

    def context(exact):
        return _mixer(x_prompt, mod, lambda b: 0, weights, ctx_consts, nb=TOKEN_BLOCK // ctx_seq, exact=exact)

    def latent(exact):
        return _mixer(x_sample, mod, lambda b: b + 1, weights, lat_consts, nb=1, exact=exact, ctx=(ctx_a, ctx_b))

    y_prompt, new_a, new_b = lax.cond(jnp.all(ctx_bound <= MAX_SCORE_BOUND),
                                      functools.partial(context, False), functools.partial(context, True))
    y_sample = lax.cond(jnp.all(lat_bound <= MAX_SCORE_BOUND),
                        functools.partial(latent, False), functools.partial(latent, True))
    new_attn_a = new_a.reshape(n_ctx_batch, DEPTH, ctx_seq, 2, N_KV_A, HEAD_DIM)
    new_attn_b = new_b.reshape(n_ctx_batch, DEPTH, ctx_seq, 2, N_HEADS_B, HEAD_DIM)
    return (y_prompt, y_sample, new_attn_a, new_attn_b)
```

```python
import functools
import math

import jax
import jax.numpy as jnp
import numpy as np
from jax import lax
from jax.experimental import pallas as pl
from jax.experimental.pallas import tpu as pltpu

D_MODEL = 1024
DEPTH = 2
GRID_W = 64
HEAD_DIM = 64
N_HEADS_A = 8
N_KV_A = 2
N_HEADS_B = 4
DIFF_DIM = 32
GROUP_C = 64
N_GROUPS_C = 4
WIDTH_A = N_HEADS_A * HEAD_DIM
WIDTH_B = N_HEADS_B * HEAD_DIM
WIDTH_C = N_GROUPS_C * GROUP_C
KV_WIDTH_A = N_KV_A * HEAD_DIM
D_IN = 2 * WIDTH_A + 2 * KV_WIDTH_A + 4 * WIDTH_B + 2 * WIDTH_C
RMS_EPS = 1e-6
ROPE_BASE = 10000.0
LOG2_E = math.log2(math.e)
LAM_INIT = tuple(0.8 - 0.6 * math.exp(-0.3 * l) for l in range(DEPTH))

LANES = 128
TOKEN_BLOCK = 512
MOD_ROWS = 8
WEIGHT_TILE = 256
V7X_VMEM_LIMIT_BYTES = 58 * 1024 * 1024
BOUND_SLACK = 1.02
MAX_SCORE_BOUND = 40.0
GUARD_SLACK = 1.05
SEQ_ROWS = 8

FEATURE_TILES = (0, 1, 3, 4, 5, 8, 10, 9)
TOKEN_TILES = (2, 6, 7)
F_QA, F_GA, F_QB, F_GB, F_GC, F_UC = 0, 512, 1024, 1280, 1536, 1792
PF_CHUNK = 256
OUT_CHUNK = 256
F_ROWS = 2048
T_KA, T_VA, T_KB, T_VB = 0, 128, 256, 512
T_COLS = 768
ONES_ROWS = 16
V_GROUP_ROWS = HEAD_DIM + ONES_ROWS
N_V_GROUPS = N_KV_A + N_HEADS_B

_BF16 = jnp.bfloat16
_F32 = jnp.float32


def _silu(x):
    return x * (1.0 / (1.0 + jnp.exp(-x)))


def _dot(a, b):
    return jnp.dot(a, b, preferred_element_type=_F32)


def _dot_nt(a, b):
    return lax.dot_general(a, b, (((1,), (1,)), ((), ())), preferred_element_type=_F32)


def _mod_kernel(c_ref, w_ref, b_ref, o_ref):
    c = c_ref[...]
    o_ref[0] = _dot(_silu(c).astype(_BF16), w_ref[0].astype(_BF16)) + b_ref[0]


def _modulation(cond, w_mod, b_mod):
    n_tile = D_MODEL
    return pl.pallas_call(
        _mod_kernel,
        grid=(DEPTH, 3 * D_MODEL // n_tile),
        in_specs=[
            pl.BlockSpec((MOD_ROWS, D_MODEL), lambda l, n: (0, 0)),
            pl.BlockSpec((1, D_MODEL, n_tile), lambda l, n: (l, 0, n)),
            pl.BlockSpec((1, 1, n_tile), lambda l, n: (l, 0, n)),
        ],
        out_specs=pl.BlockSpec((1, MOD_ROWS, n_tile), lambda l, n: (l, 0, n)),
        out_shape=jax.ShapeDtypeStruct((DEPTH, MOD_ROWS, 3 * D_MODEL), _F32),
        name="modulation",
    )(cond, w_mod, b_mod.reshape(DEPTH, 1, 3 * D_MODEL))


def _relayout_kernel(tiles_ref, w_ref, o_ref, *, transpose):
    del tiles_ref
    w = w_ref[0]
    o_ref[0] = (w.T if transpose else w).astype(_BF16)


def _relayout(w, tiles, transpose, name):
    depth, rows, _ = w.shape
    n = len(tiles)
    if transpose:
        out_shape = (depth, n * WEIGHT_TILE, rows)
        out_spec = pl.BlockSpec((1, WEIGHT_TILE, rows), lambda l, i, t: (l, i, 0))
    else:
        out_shape = (depth, rows, n * WEIGHT_TILE)
        out_spec = pl.BlockSpec((1, rows, WEIGHT_TILE), lambda l, i, t: (l, 0, i))
    return pl.pallas_call(
        functools.partial(_relayout_kernel, transpose=transpose),
        grid_spec=pltpu.PrefetchScalarGridSpec(
            num_scalar_prefetch=1,
            grid=(depth, n),
            in_specs=[pl.BlockSpec((1, rows, WEIGHT_TILE), lambda l, i, t: (l, 0, t[i]))],
            out_specs=out_spec),
        out_shape=jax.ShapeDtypeStruct(out_shape, _BF16),
        name=name,
    )(jnp.asarray(tiles, jnp.int32), w)


def _softmax_t(s_t, shift=None):
    if shift is None:
        shift = jnp.max(s_t, axis=0, keepdims=True)
    return jnp.exp2(s_t - shift).astype(_BF16)


def _rms_rows(x):
    return lax.rsqrt(jnp.mean(x * x, axis=0, keepdims=True) + RMS_EPS)


def _swap_halves(x, m):
    return jnp.concatenate([x[m:2 * m], x[0:m], x[3 * m:4 * m], x[2 * m:3 * m]], axis=0)


def _pad_rows(x, start, total):
    parts = []
    if start:
        parts.append(jnp.zeros((start, x.shape[1]), x.dtype))
    parts.append(x)
    rest = total - start - x.shape[0]
    if rest:
        parts.append(jnp.zeros((rest, x.shape[1]), x.dtype))
    return jnp.concatenate(parts, axis=0)


def _cat(parts, axis):
    return parts[0] if len(parts) == 1 else jnp.concatenate(parts, axis=axis)


def _mixer_kernel(*refs, latent, seq, nb, past, exact):
    grp = nb * seq
    n_blk = grp // TOKEN_BLOCK
    tb = TOKEN_BLOCK // nb
    it = iter(refs)
    x_ref, mod_ref, ng_ref = next(it), next(it), next(it)
    wft_ref, wt_ref, wct_ref, wout_ref = next(it), next(it), next(it), next(it)
    gqa_ref, gka_ref, gqb_ref, gkb_ref, gsub_ref, lam_ref = (next(it) for _ in range(6))
    bd64_ref, bd32_ref, bc_ref, bs_ref, cs_ref, nss_ref = (next(it) for _ in range(6))
    if latent:
        ctxa_ref, ctxb_ref = next(it), next(it)
        cqa_ref, sqa_ref, cqb_ref, sqb_ref = (next(it) for _ in range(4))
        cka_ref, skpa_ref, skma_ref, ckb_ref, skpb_ref, skmb_ref = (next(it) for _ in range(6))
        y_ref = next(it)
    else:
        y_ref, newa_ref, newb_ref = next(it), next(it), next(it)
    xres_s, h_s, ka_s, kb_s, vt_s, t1_s, t2_s, mix_s, kn2a_s, kn2b_s = (next(it) for _ in range(10))

    l = pl.program_id(1)
    j = pl.program_id(2)
    shift = mod_ref[0, 0:1, :]
    scale = mod_ref[0, 1:2, :]
    gate = mod_ref[0, 2:3, :]

    @pl.when((l == 0) & (j == 0))
    def _():
        xres_s[...] = x_ref[...].reshape(grp, D_MODEL)
        for g in range(N_V_GROUPS):
            vt_s[g * V_GROUP_ROWS + HEAD_DIM:(g + 1) * V_GROUP_ROWS, :] = jnp.ones(
                (ONES_ROWS, grp + past), _BF16)

    def rope_k(k, cos_ref, sp_ref, sm_ref, rows, m):
        out = []
        for c0 in range(0, k.shape[1], LANES):
            kc = k[:, c0:c0 + LANES]
            out.append(kc * cos_ref[rows, :] + pltpu.roll(kc, m, 1) * sp_ref[rows, :]
                       + pltpu.roll(kc, LANES - m, 1) * sm_ref[rows, :])
        return _cat(out, 1)

    def note_key_norms(i, ka, kb, first):
        for k, bd_ref, dim, kn2_s in ((ka, bd64_ref, HEAD_DIM, kn2a_s), (kb, bd32_ref, DIFF_DIM, kn2b_s)):
            n2 = jnp.max(_dot((k * k).astype(_BF16), bd_ref[...]), axis=0, keepdims=True) * float(dim)
            kn2_s[i:i + 1, :] = n2 if first else jnp.maximum(kn2_s[i:i + 1, :], n2)

    def put_values(vt, cols):
        for g in range(N_V_GROUPS):
            vt_s[g * V_GROUP_ROWS:g * V_GROUP_ROWS + HEAD_DIM, cols] = vt[g * HEAD_DIM:(g + 1) * HEAD_DIM]

    @pl.when(j == 0)
    def _prep():
        for c in range(n_blk):
            rows = slice(c * TOKEN_BLOCK, (c + 1) * TOKEN_BLOCK)
            xc = xres_s[rows, :]
            ms = jnp.mean(xc * xc, axis=-1, keepdims=True)
            hc = (xc * lax.rsqrt(ms + RMS_EPS) * ng_ref[l] * (1.0 + scale) + shift).astype(_BF16)
            h_s[rows, :] = hc
            pt = _dot(hc, wt_ref[l])
            ka = pt[:, T_KA:T_KA + KV_WIDTH_A]
            va = pt[:, T_VA:T_VA + KV_WIDTH_A]
            kb = pt[:, T_KB:T_KB + WIDTH_B]
            vb = pt[:, T_VB:T_VB + WIDTH_B]
            vat, vbt = va.T, vb.T
            ka = ka * lax.rsqrt(_dot((ka * ka).astype(_BF16), bd64_ref[...]) + RMS_EPS) * gka_ref[l]
            kb = kb * lax.rsqrt(_dot((kb * kb).astype(_BF16), bd32_ref[...]) + RMS_EPS) * gkb_ref[l]
            if latent:
                ka = rope_k(ka, cka_ref, skpa_ref, skma_ref, rows, HEAD_DIM // 4)
                kb = rope_k(kb, ckb_ref, skpb_ref, skmb_ref, rows, DIFF_DIM // 4)
            else:
                kat, kbt = ka.T, kb.T
                for i in range(nb):
                    r = slice(i * seq, (i + 1) * seq)
                    newa_ref[i, 0, 0], newa_ref[i, 0, 1] = kat[:, r], vat[:, r]
                    newb_ref[i, 0, 0], newb_ref[i, 0, 1] = kbt[:, r], vbt[:, r]
            ka_s[rows, :] = ka.astype(_BF16)
            kb_s[rows, :] = kb.astype(_BF16)
            if not exact:
                for i in range(nb):
                    r = slice(i * seq, (i + 1) * seq) if nb > 1 else slice(None)
                    note_key_norms(i, ka[r], kb[r], first=(c == 0))
            put_values(jnp.concatenate([vat, vbt], axis=0).astype(_BF16), rows)
            uct = _dot_nt(wft_ref[l, F_UC:F_ROWS, :], hc).astype(_BF16)
            t1_s[:, rows] = _dot(bc_ref[...], uct).astype(_BF16)
            t2_s[:, rows] = _dot(bs_ref[...], uct).astype(_BF16)
        if latent:
            ka_ctx, kb_ctx = ctxa_ref[0, 0, 0].T, ctxb_ref[0, 0, 0].T
            ka_s[grp:grp + past, :] = ka_ctx.astype(_BF16)
            kb_s[grp:grp + past, :] = kb_ctx.astype(_BF16)
            if not exact:
                note_key_norms(0, ka_ctx, kb_ctx, first=False)
            put_values(jnp.concatenate([ctxa_ref[0, 0, 1], ctxb_ref[0, 0, 1]], axis=0).astype(_BF16),
                       slice(grp, grp + past))

    blk_rows = pl.ds(pl.multiple_of(j * TOKEN_BLOCK, TOKEN_BLOCK), TOKEN_BLOCK)

    pf_chunks = {}

    def pf_chunk(c):
        if c not in pf_chunks:
            pf_chunks[c] = _dot_nt(wft_ref[l, c * PF_CHUNK:(c + 1) * PF_CHUNK, :], h_s[blk_rows, :])
        return pf_chunks[c]

    def pf_rows(r0, n, cols):
        c, off = divmod(r0, PF_CHUNK)
        assert off + n <= PF_CHUNK
        return pf_chunk(c)[off:off + n, cols]

    lv = lam_ref[l]
    lam_init = jnp.where(l == 0, LAM_INIT[0], LAM_INIT[1])
    lam = (jnp.exp(jnp.sum(lv[0:1] * lv[1:2], axis=-1, keepdims=True))
           - jnp.exp(jnp.sum(lv[2:3] * lv[3:4], axis=-1, keepdims=True)) + lam_init)

    def q_tile(r0, dim, gain_ref, cos_ref, sin_ref, cols, pad_start, pad_total, kn2_s):
        q = pf_rows(r0, dim, cols)
        q = q * _rms_rows(q) * gain_ref[l]
        if latent:
            q = q * cos_ref[...] + _swap_halves(q, dim // 4) * sin_ref[...]
        bound = None
        if not exact:
            i = cols.start // tb if nb > 1 else 0
            kn2 = kn2_s[i:i + 1, pad_start:pad_start + 1]
            bound = jnp.sqrt(jnp.sum(q * q, axis=0, keepdims=True) * kn2) * BOUND_SLACK
        return _pad_rows(q.astype(_BF16), pad_start, pad_total), bound

    def q_a(h, cols):
        g = h // (N_HEADS_A // N_KV_A)
        return q_tile(F_QA + h * HEAD_DIM, HEAD_DIM, gqa_ref, cqa_ref if latent else None,
                      sqa_ref if latent else None, cols, g * HEAD_DIM, KV_WIDTH_A, kn2a_s)

    def q_b(h, comp, cols):
        r0 = h * HEAD_DIM + comp * DIFF_DIM
        return q_tile(F_QB + r0, DIFF_DIM, gqb_ref, cqb_ref if latent else None,
                      sqb_ref if latent else None, cols, r0, WIDTH_B, kn2b_s)

    def put_mix(gated_t, cols, c0):
        mix_s[cols, c0:c0 + gated_t.shape[0]] = gated_t.T.astype(_BF16)

    half = {}

    def put_head(slot, cols, gated_t):
        if slot % 2 == 0:
            half[cols.start] = gated_t
        else:
            put_mix(jnp.concatenate([half.pop(cols.start), gated_t], axis=0), cols, (slot - 1) * HEAD_DIM)

    def store_a(h, cols, o):
        ga = pf_rows(F_GA + h * HEAD_DIM, HEAD_DIM, cols)
        put_head(h, cols, o * _silu(ga))

    def store_b(h, cols, o1, o2):
        ob = o1 - lam * o2
        ob = ob * _rms_rows(ob) * gsub_ref[l]
        gb = pf_rows(F_GB + h * HEAD_DIM, HEAD_DIM, cols)
        put_head(N_HEADS_A + h, cols, ob * _silu(gb))

    def v_group(g):
        return slice(g * V_GROUP_ROWS, (g + 1) * V_GROUP_ROWS)

    units = []
    all_cols = slice(0, TOKEN_BLOCK)
    if nb == 1:
        keys = slice(0, grp + past)
        for h in range(N_HEADS_A):
            g = h // (N_HEADS_A // N_KV_A)
            units.append(("a", keys, v_group(g),
                          [functools.partial(q_a, h, all_cols)],
                          functools.partial(store_a, h, all_cols)))
        for h in range(N_HEADS_B):
            v_rows = v_group(N_KV_A + h)
            held = {}
            for comp in range(2):
                def fin(o, h=h, comp=comp, held=held):
                    held[comp] = o
                    if comp == 1:
                        store_b(h, all_cols, held[0], held[1])
                units.append(("b", keys, v_rows, [functools.partial(q_b, h, comp, all_cols)], fin))
    else:
        seqs = [(slice(i * seq, (i + 1) * seq), slice(i * tb, (i + 1) * tb)) for i in range(nb)]
        for u in range(N_HEADS_A // 2):
            g = (2 * u) // (N_HEADS_A // N_KV_A)
            for keys, cols in seqs:
                def fin(o, u=u, cols=cols):
                    store_a(2 * u, cols, o[:, 0:tb])
                    store_a(2 * u + 1, cols, o[:, tb:2 * tb])
                units.append(("a", keys, v_group(g),
                              [functools.partial(q_a, 2 * u, cols), functools.partial(q_a, 2 * u + 1, cols)], fin))
        for h in range(N_HEADS_B):
            for keys, cols in seqs:
                def fin(o, h=h, cols=cols):
                    store_b(h, cols, o[:, 0:tb], o[:, tb:2 * tb])
                units.append(("b", keys, v_group(N_KV_A + h),
                              [functools.partial(q_b, h, 0, cols), functools.partial(q_b, h, 1, cols)], fin))

    def scores(unit):
        kind, keys, _, tiles, _ = unit
        k_ref = ka_s if kind == "a" else kb_s
        qs, shifts = zip(*[t() for t in tiles])
        shift = None if exact else _cat(shifts, 1)
        return _dot(k_ref[keys, :], _cat(qs, 1)), shift

    def attend(unit, s_t, shift):
        o = _dot(vt_s[unit[2], unit[1]], _softmax_t(s_t, shift))
        unit[4](o[0:HEAD_DIM] / o[HEAD_DIM:HEAD_DIM + 1])

    def mixer_c():
        if nb == 1:
            ft = _dot(t1_s[...], cs_ref[...]) + _dot(t2_s[...], nss_ref[...])
        else:
            ft = _cat([_dot(t1_s[:, i * seq:(i + 1) * seq], cs_ref[...])
                       + _dot(t2_s[:, i * seq:(i + 1) * seq], nss_ref[...]) for i in range(nb)], 1)
        oc = _dot(wct_ref[l], ft.astype(_BF16))
        put_mix(oc * _silu(pf_rows(F_GC, WIDTH_C, all_cols)), all_cols, WIDTH_A + WIDTH_B)

    out_parts = []

    def out_chunk(k):
        cols = slice(k * OUT_CHUNK, (k + 1) * OUT_CHUNK)
        out_parts.append(_dot(mix_s[:, cols], wout_ref[l, cols, :]))

    fillers = {
        0: [functools.partial(pf_chunk, F_GA // PF_CHUNK)],
        1: [functools.partial(pf_chunk, F_QA // PF_CHUNK + 1)],
        2: [functools.partial(pf_chunk, F_GA // PF_CHUNK + 1)],
        3: [functools.partial(pf_chunk, F_QB // PF_CHUNK)],
        4: [functools.partial(pf_chunk, F_GB // PF_CHUNK)],
        5: [mixer_c],
        6: [functools.partial(out_chunk, 3), functools.partial(out_chunk, 0)],
        9: [functools.partial(out_chunk, 1)],
    }
    s_next = scores(units[0])
    for u, unit in enumerate(units):
        s_cur, shift = s_next
        if u + 1 < len(units):
            s_next = scores(units[u + 1])
        for work in fillers.get(u, ()):
            work()
        attend(unit, s_cur, shift)
    out_chunk(2)

    y = xres_s[blk_rows, :] + gate * functools.reduce(lambda a, b: a + b, out_parts)
    xres_s[blk_rows, :] = y

    @pl.when(l == DEPTH - 1)
    def _():
        y_ref[...] = y.reshape(y_ref.shape)


def _const_spec(shape):
    nd = len(shape)
    return pl.BlockSpec(shape, lambda b, l, j: (0,) * nd, pipeline_mode=pl.Buffered(1))


def _mixer(x, mod, mod_row, wts, consts, nb, exact, ctx=None):
    latent = ctx is not None
    n_batch, seq, _ = x.shape
    grp = nb * seq
    n_blk = grp // TOKEN_BLOCK
    assert grp % TOKEN_BLOCK == 0 and (nb == 1 or grp == TOKEN_BLOCK) and n_batch % nb == 0 and nb <= SEQ_ROWS
    past = ctx[0].shape[-1] if latent else 0
    last = DEPTH - 1

    args = [x, mod]
    specs = [
        pl.BlockSpec((nb, seq, D_MODEL), lambda b, l, j: (b, 0, 0),
                     pipeline_mode=pl.Buffered(1) if latent else None),
        pl.BlockSpec((1, 3, D_MODEL), lambda b, l, j: (l * MOD_ROWS + mod_row(b), 0, 0)),
    ]
    for name in ("norm_g", "wft", "wt", "wct", "wout", "gqa", "gka", "gqb", "gkb", "gsub", "lam"):
        args.append(wts[name])
        specs.append(_const_spec(wts[name].shape))
    for name in ("bd64", "bd32", "bc", "bs"):
        args.append(consts[name])
        specs.append(_const_spec(consts[name].shape))
    dft_cols = TOKEN_BLOCK if nb == 1 else seq
    for name in ("cs", "nss"):
        args.append(consts[name])
        specs.append(pl.BlockSpec((seq, dft_cols), lambda b, l, j: (0, j)))
    if latent:
        args += [ctx[0], ctx[1]]
        specs += [
            pl.BlockSpec((1, 1, 2, KV_WIDTH_A, past), lambda b, l, j: (b, l, 0, 0, 0), pipeline_mode=pl.Buffered(1)),
            pl.BlockSpec((1, 1, 2, WIDTH_B, past), lambda b, l, j: (b, l, 0, 0, 0), pipeline_mode=pl.Buffered(1)),
        ]
        for name, rows in (("cqa", HEAD_DIM), ("sqa", HEAD_DIM), ("cqb", DIFF_DIM), ("sqb", DIFF_DIM)):
            args.append(consts[name])
            specs.append(pl.BlockSpec((rows, TOKEN_BLOCK), lambda b, l, j: (0, j)))
        for name in ("cka", "skpa", "skma", "ckb", "skpb", "skmb"):
            args.append(consts[name])
            specs.append(_const_spec(consts[name].shape))

    y_shape = jax.ShapeDtypeStruct(x.shape, _F32)
    if nb == 1:
        y_spec = pl.BlockSpec((1, TOKEN_BLOCK, D_MODEL), lambda b, l, j: (b, jnp.where(l == last, j, 0), 0))
    else:
        y_spec = pl.BlockSpec((nb, seq, D_MODEL), lambda b, l, j: (b, 0, 0))
    if latent:
        out_shape, out_specs = y_shape, y_spec
    else:
        out_shape = (y_shape,
                     jax.ShapeDtypeStruct((n_batch, DEPTH, 2, KV_WIDTH_A, seq), _F32),
                     jax.ShapeDtypeStruct((n_batch, DEPTH, 2, WIDTH_B, seq), _F32))
        out_specs = (y_spec,
                     pl.BlockSpec((nb, 1, 2, KV_WIDTH_A, seq), lambda b, l, j: (b, l, 0, 0, 0)),
                     pl.BlockSpec((nb, 1, 2, WIDTH_B, seq), lambda b, l, j: (b, l, 0, 0, 0)))

    n_keys = grp + past
    scratch = [
        pltpu.VMEM((grp, D_MODEL), _F32),
        pltpu.VMEM((grp, D_MODEL), _BF16),
        pltpu.VMEM((n_keys, KV_WIDTH_A), _BF16),
        pltpu.VMEM((n_keys, WIDTH_B), _BF16),
        pltpu.VMEM((N_V_GROUPS * V_GROUP_ROWS, n_keys), _BF16),
        pltpu.VMEM((WIDTH_C, grp), _BF16),
        pltpu.VMEM((WIDTH_C, grp), _BF16),
        pltpu.VMEM((TOKEN_BLOCK, D_MODEL), _BF16),
        pltpu.VMEM((SEQ_ROWS, KV_WIDTH_A), _F32),
        pltpu.VMEM((SEQ_ROWS, WIDTH_B), _F32),
    ]
    kern = functools.partial(_mixer_kernel, latent=latent, seq=seq, nb=nb, past=past, exact=exact)
    return pl.pallas_call(
        kern,
        grid=(n_batch // nb, DEPTH, n_blk),
        in_specs=specs,
        out_specs=out_specs,
        out_shape=out_shape,
        scratch_shapes=scratch,
        compiler_params=pltpu.CompilerParams(
            dimension_semantics=("arbitrary", "arbitrary", "arbitrary"),
            vmem_limit_bytes=V7X_VMEM_LIMIT_BYTES),
        name=("latent_mixer" if latent else "context_mixer") + ("_exact" if exact else ""),
    )(*args)


def _rope_tables(n_tok, dim):
    m = dim // 4
    t = np.arange(n_tok)
    inv = 1.0 / (ROPE_BASE ** (np.arange(m, dtype=np.float64) / m))
    ar = (t // GRID_W)[:, None] * inv
    ac = (t % GRID_W)[:, None] * inv
    cos = np.concatenate([np.cos(ar), np.cos(ar), np.cos(ac), np.cos(ac)], axis=1)
    sin = np.concatenate([-np.sin(ar), np.sin(ar), -np.sin(ac), np.sin(ac)], axis=1)
    first = np.tile(np.concatenate([np.ones(m), np.zeros(m)]), 2)[None, :]
    return cos.astype(np.float32), sin.astype(np.float32), first.astype(np.float32)


def _block_diag(block, n):
    out = np.zeros((block.shape[0] * n, block.shape[1] * n), np.float64)
    for i in range(n):
        out[i * block.shape[0]:(i + 1) * block.shape[0], i * block.shape[1]:(i + 1) * block.shape[1]] = block
    return out


def _dft_cos_sin(n):
    k = np.arange(n)
    ang = 2.0 * np.pi * ((k[:, None] * k[None, :]) % n) / n
    return np.cos(ang), np.sin(ang)


def _mixer_consts(seq, latent):
    c64, s64 = _dft_cos_sin(GROUP_C)
    norm = 1.0 / math.sqrt(GROUP_C * seq)
    cs, ss = _dft_cos_sin(seq)
    f32c = lambda a: jnp.asarray(np.asarray(a, np.float32))
    consts = {
        "bd64": f32c(_block_diag(np.full((HEAD_DIM, HEAD_DIM), 1.0 / HEAD_DIM), N_KV_A)).astype(_BF16),
        "bd32": f32c(_block_diag(np.full((DIFF_DIM, DIFF_DIM), 1.0 / DIFF_DIM), 2 * N_HEADS_B)).astype(_BF16),
        "bc": f32c(_block_diag(c64, N_GROUPS_C) * norm).astype(_BF16),
        "bs": f32c(_block_diag(s64, N_GROUPS_C) * norm).astype(_BF16),
        "cs": f32c(cs).astype(_BF16),
        "nss": f32c(-ss).astype(_BF16),
    }
    if latent:
        cos_a, sin_a, first_a = _rope_tables(seq, HEAD_DIM)
        cos_b, sin_b, first_b = _rope_tables(seq, DIFF_DIM)
        rep_a, rep_b = LANES // HEAD_DIM, LANES // DIFF_DIM
        consts.update({
            "cqa": f32c(cos_a.T), "sqa": f32c(sin_a.T), "cqb": f32c(cos_b.T), "sqb": f32c(sin_b.T),
            "cka": f32c(np.tile(cos_a, (1, rep_a))),
            "skpa": f32c(np.tile(sin_a * (1.0 - first_a), (1, rep_a))),
            "skma": f32c(np.tile(sin_a * first_a, (1, rep_a))),
            "ckb": f32c(np.tile(cos_b, (1, rep_b))),
            "skpb": f32c(np.tile(sin_b * (1.0 - first_b), (1, rep_b))),
            "skmb": f32c(np.tile(sin_b * first_b, (1, rep_b))),
        })
    return consts


def _mixer_weights(norm_g, w_in, q_norm_a, k_norm_a, q_norm_b, k_norm_b,
                   lambda_q1, lambda_k1, lambda_q2, lambda_k2, subln_g, w_fourier, w_out):
    keep = jnp.asarray([1.0 - v for v in LAM_INIT], _F32)
    return {
        "norm_g": norm_g.reshape(DEPTH, 1, D_MODEL),
        "wft": _relayout(w_in, FEATURE_TILES, True, "w_in_feature_major"),
        "wt": _relayout(w_in, TOKEN_TILES, False, "w_in_token_major"),
        "wct": _relayout(w_fourier, (0,), True, "w_fourier_t"),
        "wout": _relayout(w_out, tuple(range(D_MODEL // WEIGHT_TILE)), False, "w_out_bf16"),
        "gqa": (q_norm_a * (HEAD_DIM ** -0.5 * LOG2_E)).reshape(DEPTH, HEAD_DIM, 1),
        "gka": jnp.tile(k_norm_a, (1, N_KV_A)).reshape(DEPTH, 1, KV_WIDTH_A),
        "gqb": (q_norm_b * (DIFF_DIM ** -0.5 * LOG2_E)).reshape(DEPTH, DIFF_DIM, 1),
        "gkb": jnp.tile(k_norm_b, (1, 2 * N_HEADS_B)).reshape(DEPTH, 1, WIDTH_B),
        "gsub": (subln_g * keep[:, None]).reshape(DEPTH, HEAD_DIM, 1),
        "lam": jnp.stack([lambda_q1, lambda_k1, lambda_q2, lambda_k2], axis=1),
    }


def kernel(x_prompt, x_sample, cache_attn_a, cache_attn_b, c, c_ctx, norm_g, w_mod, b_mod, w_in,
           q_norm_a, k_norm_a, q_norm_b, k_norm_b, lambda_q1, lambda_k1, lambda_q2, lambda_k2,
           subln_g, w_fourier, w_out):
    n_ctx_batch, ctx_seq, _ = x_prompt.shape
    n_lat_batch, lat_seq, _ = x_sample.shape
    past = cache_attn_a.shape[2]
    assert n_lat_batch + 1 <= MOD_ROWS and w_in.shape == (DEPTH, D_MODEL, D_IN)

    cond = jnp.concatenate(
        [c_ctx[None, :], c, jnp.zeros((MOD_ROWS - 1 - n_lat_batch, D_MODEL), _F32)], axis=0)
    mod = _modulation(cond, w_mod, b_mod).reshape(DEPTH * MOD_ROWS, 3, D_MODEL)

    weights = _mixer_weights(norm_g, w_in, q_norm_a, k_norm_a, q_norm_b, k_norm_b,
                             lambda_q1, lambda_k1, lambda_q2, lambda_k2, subln_g, w_fourier, w_out)

    def feature_major(cache):
        n_heads, dim = cache.shape[-2:]
        return jnp.transpose(cache, (0, 1, 3, 4, 5, 2)).reshape(n_lat_batch, DEPTH, 2, n_heads * dim, past)

    def token_major(new, n_heads):
        seq = new.shape[-1]
        return jnp.transpose(new.reshape(n_ctx_batch, DEPTH, 2, n_heads, HEAD_DIM, seq), (0, 1, 5, 2, 3, 4))

    ctx_a, ctx_b = feature_major(cache_attn_a), feature_major(cache_attn_b)

    def max_norm(gain, dim):
        return math.sqrt(dim) * jnp.max(jnp.abs(gain), axis=-1)

    def cached_norm(ctx, dim):
        k = ctx[:, :, 0].reshape(n_lat_batch, DEPTH, -1, dim, past)
        return jnp.sqrt(jnp.max(jnp.sum(k * k, axis=3), axis=(0, 2, 3)))

    q_a, q_b = max_norm(weights["gqa"][:, :, 0], HEAD_DIM), max_norm(weights["gqb"][:, :, 0], DIFF_DIM)
    k_a, k_b = max_norm(k_norm_a, HEAD_DIM), max_norm(k_norm_b, DIFF_DIM)
    ctx_bound = GUARD_SLACK * jnp.maximum(q_a * k_a, q_b * k_b)
    lat_bound = GUARD_SLACK * jnp.maximum(q_a * jnp.maximum(k_a, cached_norm(ctx_a, HEAD_DIM)),
                                          q_b * jnp.maximum(k_b, cached_norm(ctx_b, DIFF_DIM)))

    ctx_consts = _mixer_consts(ctx_seq, latent=False)
    lat_consts = _mixer_consts(lat_seq, latent=True)

    def context(exact):
        return _mixer(x_prompt, mod, lambda b: 0, weights, ctx_consts, nb=TOKEN_BLOCK // ctx_seq, exact=exact)

    def latent(exact):
        return _mixer(x_sample, mod, lambda b: b + 1, weights, lat_consts, nb=1, exact=exact, ctx=(ctx_a, ctx_b))

    y_prompt, new_a, new_b = lax.cond(jnp.all(ctx_bound <= MAX_SCORE_BOUND),
                                      functools.partial(context, False), functools.partial(context, True))
    y_sample = lax.cond(jnp.all(lat_bound <= MAX_SCORE_BOUND),
                        functools.partial(latent, False), functools.partial(latent, True))
    return (y_prompt, y_sample, token_major(new_a, N_KV_A), token_major(new_b, N_HEADS_B))
```

```python
import functools
import math

import jax
import jax.numpy as jnp
import numpy as np
from jax import lax
from jax.experimental import pallas as pl
from jax.experimental.pallas import tpu as pltpu

D_MODEL = 1024
DEPTH = 2
GRID_W = 64
HEAD_DIM = 64
N_HEADS_A = 8
N_KV_A = 2
N_HEADS_B = 4
DIFF_DIM = 32
GROUP_C = 64
N_GROUPS_C = 4
WIDTH_A = N_HEADS_A * HEAD_DIM
WIDTH_B = N_HEADS_B * HEAD_DIM
WIDTH_C = N_GROUPS_C * GROUP_C
KV_WIDTH_A = N_KV_A * HEAD_DIM
D_IN = 2 * WIDTH_A + 2 * KV_WIDTH_A + 4 * WIDTH_B + 2 * WIDTH_C
RMS_EPS = 1e-6
ROPE_BASE = 10000.0
LOG2_E = math.log2(math.e)
LAM_INIT = tuple(0.8 - 0.6 * math.exp(-0.3 * l) for l in range(DEPTH))

LANES = 128
TOKEN_BLOCK = 512
MOD_ROWS = 8
WEIGHT_TILE = 256
V7X_VMEM_LIMIT_BYTES = 58 * 1024 * 1024
BOUND_SLACK = 1.02
MAX_SCORE_BOUND = 40.0
GUARD_SLACK = 1.05
SEQ_ROWS = 8
P_ROWS, P_LANES = 8, 256
P_GKA, P_GKB, P_LAM = 0, 1, 2
C_COLS = 8
C_GQA, C_GQB, C_GSUB = 0, 1, 2

FEATURE_TILES = (0, 1, 3, 4, 5, 8, 10, 9)
TOKEN_TILES = (2, 6, 7)
F_QA, F_GA, F_QB, F_GB, F_GC, F_UC = 0, 512, 1024, 1280, 1536, 1792
PF_CHUNK = 256
OUT_CHUNK = 256
F_ROWS = 2048
T_KA, T_VA, T_KB, T_VB = 0, 128, 256, 512
T_COLS = 768
ONES_ROWS = 16
V_GROUP_ROWS = HEAD_DIM + ONES_ROWS
N_V_GROUPS = N_KV_A + N_HEADS_B

_BF16 = jnp.bfloat16
_F32 = jnp.float32


def _silu(x):
    return x * (1.0 / (1.0 + jnp.exp(-x)))


def _dot(a, b):
    return jnp.dot(a, b, preferred_element_type=_F32)


def _dot_nt(a, b):
    return lax.dot_general(a, b, (((1,), (1,)), ((), ())), preferred_element_type=_F32)


def _mod_kernel(c_ref, w_ref, b_ref, o_ref):
    c = c_ref[...]
    o_ref[0, 0] = _dot(_silu(c).astype(_BF16), w_ref[0].astype(_BF16)) + b_ref[0]


def _modulation(cond, w_mod, b_mod):
    n_tile = D_MODEL
    return pl.pallas_call(
        _mod_kernel,
        grid=(DEPTH, 3 * D_MODEL // n_tile),
        in_specs=[
            pl.BlockSpec((MOD_ROWS, D_MODEL), lambda l, n: (0, 0)),
            pl.BlockSpec((1, D_MODEL, n_tile), lambda l, n: (l, 0, n)),
            pl.BlockSpec((1, 1, n_tile), lambda l, n: (l, 0, n)),
        ],
        out_specs=pl.BlockSpec((1, 1, MOD_ROWS, n_tile), lambda l, n: (l, n, 0, 0)),
        out_shape=jax.ShapeDtypeStruct((DEPTH, 3, MOD_ROWS, D_MODEL), _F32),
        name="modulation",
    )(cond, w_mod, b_mod.reshape(DEPTH, 1, 3 * D_MODEL))


def _relayout_kernel(tiles_ref, w_ref, o_ref, *, transpose):
    del tiles_ref
    w = w_ref[0]
    o_ref[0] = (w.T if transpose else w).astype(_BF16)


def _relayout(w, tiles, transpose, name):
    depth, rows, _ = w.shape
    n = len(tiles)
    if transpose:
        out_shape = (depth, n * WEIGHT_TILE, rows)
        out_spec = pl.BlockSpec((1, WEIGHT_TILE, rows), lambda l, i, t: (l, i, 0))
    else:
        out_shape = (depth, rows, n * WEIGHT_TILE)
        out_spec = pl.BlockSpec((1, rows, WEIGHT_TILE), lambda l, i, t: (l, 0, i))
    return pl.pallas_call(
        functools.partial(_relayout_kernel, transpose=transpose),
        grid_spec=pltpu.PrefetchScalarGridSpec(
            num_scalar_prefetch=1,
            grid=(depth, n),
            in_specs=[pl.BlockSpec((1, rows, WEIGHT_TILE), lambda l, i, t: (l, 0, t[i]))],
            out_specs=out_spec),
        out_shape=jax.ShapeDtypeStruct(out_shape, _BF16),
        name=name,
    )(jnp.asarray(tiles, jnp.int32), w)


def _softmax_t(s_t, shift=None):
    if shift is None:
        shift = jnp.max(s_t, axis=0, keepdims=True)
    return jnp.exp2(s_t - shift).astype(_BF16)


def _rms_rows(x):
    return lax.rsqrt(jnp.mean(x * x, axis=0, keepdims=True) + RMS_EPS)


def _swap_halves(x, m):
    return jnp.concatenate([x[m:2 * m], x[0:m], x[3 * m:4 * m], x[2 * m:3 * m]], axis=0)


def _pad_rows(x, start, total):
    parts = []
    if start:
        parts.append(jnp.zeros((start, x.shape[1]), x.dtype))
    parts.append(x)
    rest = total - start - x.shape[0]
    if rest:
        parts.append(jnp.zeros((rest, x.shape[1]), x.dtype))
    return jnp.concatenate(parts, axis=0)


def _cat(parts, axis):
    return parts[0] if len(parts) == 1 else jnp.concatenate(parts, axis=axis)


def _mixer_kernel(*refs, latent, seq, nb, past, exact):
    grp = nb * seq
    n_blk = grp // TOKEN_BLOCK
    tb = TOKEN_BLOCK // nb
    it = iter(refs)
    x_ref, mod_ref, ng_ref = next(it), next(it), next(it)
    wft_ref, wt_ref, wct_ref, wout_ref = next(it), next(it), next(it), next(it)
    prow_ref, pcol_ref = next(it), next(it)
    bd64_ref, bd32_ref, bc_ref, bs_ref, cs_ref, nss_ref = (next(it) for _ in range(6))
    if latent:
        ctxa_ref, ctxb_ref = next(it), next(it)
        cqa_ref, sqa_ref, cqb_ref, sqb_ref = (next(it) for _ in range(4))
        cka_ref, skpa_ref, skma_ref, ckb_ref, skpb_ref, skmb_ref = (next(it) for _ in range(6))
        y_ref = next(it)
    else:
        y_ref, newa_ref, newb_ref = next(it), next(it), next(it)
    xres_s, h_s, ka_s, kb_s, vt_s, t1_s, t2_s, mix_s, kn2a_s, kn2b_s = (next(it) for _ in range(10))

    l = pl.program_id(1)
    j = pl.program_id(2)
    mod_row = pl.ds(pl.program_id(0) + 1 if latent else 0, 1)
    shift = mod_ref[0, 0, mod_row, :]
    scale = mod_ref[0, 1, mod_row, :]
    gate = mod_ref[0, 2, mod_row, :]
    prow, pcol = prow_ref[l], pcol_ref[l]
    gka, gkb = prow[P_GKA:P_GKA + 1, 0:KV_WIDTH_A], prow[P_GKB:P_GKB + 1, :]
    gqa, gqb, gsub = pcol[:, C_GQA:C_GQA + 1], pcol[0:DIFF_DIM, C_GQB:C_GQB + 1], pcol[:, C_GSUB:C_GSUB + 1]

    @pl.when((l == 0) & (j == 0))
    def _():
        xres_s[...] = x_ref[...].reshape(grp, D_MODEL)
        for g in range(N_V_GROUPS):
            vt_s[g * V_GROUP_ROWS + HEAD_DIM:(g + 1) * V_GROUP_ROWS, :] = jnp.ones(
                (ONES_ROWS, grp + past), _BF16)

    def rope_k(k, cos_ref, sp_ref, sm_ref, rows, m):
        out = []
        for c0 in range(0, k.shape[1], LANES):
            kc = k[:, c0:c0 + LANES]
            out.append(kc * cos_ref[rows, :] + pltpu.roll(kc, m, 1) * sp_ref[rows, :]
                       + pltpu.roll(kc, LANES - m, 1) * sm_ref[rows, :])
        return _cat(out, 1)

    def note_key_norms(i, ka, kb, first):
        for k, bd_ref, dim, kn2_s in ((ka, bd64_ref, HEAD_DIM, kn2a_s), (kb, bd32_ref, DIFF_DIM, kn2b_s)):
            n2 = jnp.max(_dot((k * k).astype(_BF16), bd_ref[...]), axis=0, keepdims=True) * float(dim)
            kn2_s[i:i + 1, :] = n2 if first else jnp.maximum(kn2_s[i:i + 1, :], n2)

    def put_values(vt, cols):
        for g in range(N_V_GROUPS):
            vt_s[g * V_GROUP_ROWS:g * V_GROUP_ROWS + HEAD_DIM, cols] = vt[g * HEAD_DIM:(g + 1) * HEAD_DIM]

    @pl.when(j == 0)
    def _prep():
        for c in range(n_blk):
            rows = slice(c * TOKEN_BLOCK, (c + 1) * TOKEN_BLOCK)
            xc = xres_s[rows, :]
            ms = jnp.mean(xc * xc, axis=-1, keepdims=True)
            hc = (xc * lax.rsqrt(ms + RMS_EPS) * ng_ref[l] * (1.0 + scale) + shift).astype(_BF16)
            h_s[rows, :] = hc
            pt = _dot(hc, wt_ref[l])
            ka = pt[:, T_KA:T_KA + KV_WIDTH_A]
            va = pt[:, T_VA:T_VA + KV_WIDTH_A]
            kb = pt[:, T_KB:T_KB + WIDTH_B]
            vb = pt[:, T_VB:T_VB + WIDTH_B]
            vat, vbt = va.T, vb.T
            ka = ka * lax.rsqrt(_dot((ka * ka).astype(_BF16), bd64_ref[...]) + RMS_EPS) * gka
            kb = kb * lax.rsqrt(_dot((kb * kb).astype(_BF16), bd32_ref[...]) + RMS_EPS) * gkb
            if latent:
                ka = rope_k(ka, cka_ref, skpa_ref, skma_ref, rows, HEAD_DIM // 4)
                kb = rope_k(kb, ckb_ref, skpb_ref, skmb_ref, rows, DIFF_DIM // 4)
            else:
                kat, kbt = ka.T, kb.T
                for i in range(nb):
                    r = slice(i * seq, (i + 1) * seq)
                    newa_ref[i, 0, 0], newa_ref[i, 0, 1] = kat[:, r], vat[:, r]
                    newb_ref[i, 0, 0], newb_ref[i, 0, 1] = kbt[:, r], vbt[:, r]
            ka_s[rows, :] = ka.astype(_BF16)
            kb_s[rows, :] = kb.astype(_BF16)
            if not exact:
                for i in range(nb):
                    r = slice(i * seq, (i + 1) * seq) if nb > 1 else slice(None)
                    note_key_norms(i, ka[r], kb[r], first=(c == 0))
            put_values(jnp.concatenate([vat, vbt], axis=0).astype(_BF16), rows)
            uct = _dot_nt(wft_ref[l, F_UC:F_ROWS, :], hc).astype(_BF16)
            t1_s[:, rows] = _dot(bc_ref[...], uct).astype(_BF16)
            t2_s[:, rows] = _dot(bs_ref[...], uct).astype(_BF16)
        if latent:
            ka_ctx, kb_ctx = ctxa_ref[0, 0, 0].T, ctxb_ref[0, 0, 0].T
            ka_s[grp:grp + past, :] = ka_ctx.astype(_BF16)
            kb_s[grp:grp + past, :] = kb_ctx.astype(_BF16)
            if not exact:
                note_key_norms(0, ka_ctx, kb_ctx, first=False)
            put_values(jnp.concatenate([ctxa_ref[0, 0, 1], ctxb_ref[0, 0, 1]], axis=0).astype(_BF16),
                       slice(grp, grp + past))

    blk_rows = pl.ds(pl.multiple_of(j * TOKEN_BLOCK, TOKEN_BLOCK), TOKEN_BLOCK)

    pf_chunks = {}

    def pf_chunk(c):
        if c not in pf_chunks:
            pf_chunks[c] = _dot_nt(wft_ref[l, c * PF_CHUNK:(c + 1) * PF_CHUNK, :], h_s[blk_rows, :])
        return pf_chunks[c]

    def pf_rows(r0, n, cols):
        c, off = divmod(r0, PF_CHUNK)
        assert off + n <= PF_CHUNK
        return pf_chunk(c)[off:off + n, cols]

    lv = prow[P_LAM:P_LAM + 1, :]
    lq1, lk1, lq2, lk2 = (lv[:, i * DIFF_DIM:(i + 1) * DIFF_DIM] for i in range(4))
    lam_init = jnp.where(l == 0, LAM_INIT[0], LAM_INIT[1])
    lam = (jnp.exp(jnp.sum(lq1 * lk1, axis=-1, keepdims=True))
           - jnp.exp(jnp.sum(lq2 * lk2, axis=-1, keepdims=True)) + lam_init)

    def q_tile(r0, dim, gain, cos_ref, sin_ref, cols, pad_start, pad_total, kn2_s):
        q = pf_rows(r0, dim, cols)
        q = q * _rms_rows(q) * gain
        if latent:
            q = q * cos_ref[...] + _swap_halves(q, dim // 4) * sin_ref[...]
        bound = None
        if not exact:
            i = cols.start // tb if nb > 1 else 0
            kn2 = kn2_s[i:i + 1, pad_start:pad_start + 1]
            bound = jnp.sqrt(jnp.sum(q * q, axis=0, keepdims=True) * kn2) * BOUND_SLACK
        return _pad_rows(q.astype(_BF16), pad_start, pad_total), bound

    def q_a(h, cols):
        g = h // (N_HEADS_A // N_KV_A)
        return q_tile(F_QA + h * HEAD_DIM, HEAD_DIM, gqa, cqa_ref if latent else None,
                      sqa_ref if latent else None, cols, g * HEAD_DIM, KV_WIDTH_A, kn2a_s)

    def q_b(h, comp, cols):
        r0 = h * HEAD_DIM + comp * DIFF_DIM
        return q_tile(F_QB + r0, DIFF_DIM, gqb, cqb_ref if latent else None,
                      sqb_ref if latent else None, cols, r0, WIDTH_B, kn2b_s)

    def put_mix(gated_t, cols, c0):
        mix_s[cols, c0:c0 + gated_t.shape[0]] = gated_t.T.astype(_BF16)

    half = {}

    def put_head(slot, cols, gated_t):
        if slot % 2 == 0:
            half[cols.start] = gated_t
        else:
            put_mix(jnp.concatenate([half.pop(cols.start), gated_t], axis=0), cols, (slot - 1) * HEAD_DIM)

    def store_a(h, cols, o):
        ga = pf_rows(F_GA + h * HEAD_DIM, HEAD_DIM, cols)
        put_head(h, cols, o * _silu(ga))

    def store_b(h, cols, o1, o2):
        ob = o1 - lam * o2
        ob = ob * _rms_rows(ob) * gsub
        gb = pf_rows(F_GB + h * HEAD_DIM, HEAD_DIM, cols)
        put_head(N_HEADS_A + h, cols, ob * _silu(gb))

    def v_group(g):
        return slice(g * V_GROUP_ROWS, (g + 1) * V_GROUP_ROWS)

    units = []
    all_cols = slice(0, TOKEN_BLOCK)
    if nb == 1:
        keys = slice(0, grp + past)
        for h in range(N_HEADS_A):
            g = h // (N_HEADS_A // N_KV_A)
            units.append(("a", keys, v_group(g),
                          [functools.partial(q_a, h, all_cols)],
                          functools.partial(store_a, h, all_cols)))
        for h in range(N_HEADS_B):
            v_rows = v_group(N_KV_A + h)
            held = {}
            for comp in range(2):
                def fin(o, h=h, comp=comp, held=held):
                    held[comp] = o
                    if comp == 1:
                        store_b(h, all_cols, held[0], held[1])
                units.append(("b", keys, v_rows, [functools.partial(q_b, h, comp, all_cols)], fin))
    else:
        seqs = [(slice(i * seq, (i + 1) * seq), slice(i * tb, (i + 1) * tb)) for i in range(nb)]
        for u in range(N_HEADS_A // 2):
            g = (2 * u) // (N_HEADS_A // N_KV_A)
            for keys, cols in seqs:
                def fin(o, u=u, cols=cols):
                    store_a(2 * u, cols, o[:, 0:tb])
                    store_a(2 * u + 1, cols, o[:, tb:2 * tb])
                units.append(("a", keys, v_group(g),
                              [functools.partial(q_a, 2 * u, cols), functools.partial(q_a, 2 * u + 1, cols)], fin))
        for h in range(N_HEADS_B):
            for keys, cols in seqs:
                def fin(o, h=h, cols=cols):
                    store_b(h, cols, o[:, 0:tb], o[:, tb:2 * tb])
                units.append(("b", keys, v_group(N_KV_A + h),
                              [functools.partial(q_b, h, 0, cols), functools.partial(q_b, h, 1, cols)], fin))

    def scores(unit):
        kind, keys, _, tiles, _ = unit
        k_ref = ka_s if kind == "a" else kb_s
        qs, shifts = zip(*[t() for t in tiles])
        shift = None if exact else _cat(shifts, 1)
        return _dot(k_ref[keys, :], _cat(qs, 1)), shift

    def attend(unit, s_t, shift):
        o = _dot(vt_s[unit[2], unit[1]], _softmax_t(s_t, shift))
        unit[4](o[0:HEAD_DIM] / o[HEAD_DIM:HEAD_DIM + 1])

    def mixer_c():
        if nb == 1:
            ft = _dot(t1_s[...], cs_ref[j]) + _dot(t2_s[...], nss_ref[j])
        else:
            ft = _cat([_dot(t1_s[:, i * seq:(i + 1) * seq], cs_ref[0])
                       + _dot(t2_s[:, i * seq:(i + 1) * seq], nss_ref[0]) for i in range(nb)], 1)
        oc = _dot(wct_ref[l], ft.astype(_BF16))
        put_mix(oc * _silu(pf_rows(F_GC, WIDTH_C, all_cols)), all_cols, WIDTH_A + WIDTH_B)

    out_parts = []

    def out_chunk(k):
        cols = slice(k * OUT_CHUNK, (k + 1) * OUT_CHUNK)
        out_parts.append(_dot(mix_s[:, cols], wout_ref[l, cols, :]))

    fillers = {
        0: [functools.partial(pf_chunk, F_GA // PF_CHUNK)],
        1: [functools.partial(pf_chunk, F_QA // PF_CHUNK + 1)],
        2: [functools.partial(pf_chunk, F_GA // PF_CHUNK + 1)],
        3: [functools.partial(pf_chunk, F_QB // PF_CHUNK)],
        4: [functools.partial(pf_chunk, F_GB // PF_CHUNK)],
        5: [mixer_c],
        6: [functools.partial(out_chunk, 3), functools.partial(out_chunk, 0)],
        9: [functools.partial(out_chunk, 1)],
    }
    s_next = scores(units[0])
    for u, unit in enumerate(units):
        s_cur, shift = s_next
        if u + 1 < len(units):
            s_next = scores(units[u + 1])
        for work in fillers.get(u, ()):
            work()
        attend(unit, s_cur, shift)
    out_chunk(2)

    y = xres_s[blk_rows, :] + gate * functools.reduce(lambda a, b: a + b, out_parts)
    xres_s[blk_rows, :] = y

    @pl.when(l == DEPTH - 1)
    def _():
        y_ref[...] = y.reshape(y_ref.shape)


def _const_spec(shape):
    nd = len(shape)
    return pl.BlockSpec(shape, lambda b, l, j: (0,) * nd, pipeline_mode=pl.Buffered(1))


def _mixer(x, mod, wts, consts, nb, exact, ctx=None):
    latent = ctx is not None
    n_batch, seq, _ = x.shape
    grp = nb * seq
    n_blk = grp // TOKEN_BLOCK
    assert grp % TOKEN_BLOCK == 0 and (nb == 1 or grp == TOKEN_BLOCK) and n_batch % nb == 0 and nb <= SEQ_ROWS
    past = ctx[0].shape[-1] if latent else 0
    last = DEPTH - 1

    args = [x, mod]
    specs = [
        pl.BlockSpec((nb, seq, D_MODEL), lambda b, l, j: (b, 0, 0),
                     pipeline_mode=pl.Buffered(1) if latent else None),
        pl.BlockSpec((1, 3, MOD_ROWS, D_MODEL), lambda b, l, j: (l, 0, 0, 0)),
    ]
    for name in ("norm_g", "wft", "wt", "wct", "wout", "prow", "pcol"):
        args.append(wts[name])
        specs.append(_const_spec(wts[name].shape))
    for name in ("bd64", "bd32", "bc", "bs", "cs", "nss"):
        args.append(consts[name])
        specs.append(_const_spec(consts[name].shape))
    if latent:
        args += [ctx[0], ctx[1]]
        specs += [
            pl.BlockSpec((1, 1, 2, KV_WIDTH_A, past), lambda b, l, j: (b, l, 0, 0, 0), pipeline_mode=pl.Buffered(1)),
            pl.BlockSpec((1, 1, 2, WIDTH_B, past), lambda b, l, j: (b, l, 0, 0, 0), pipeline_mode=pl.Buffered(1)),
        ]
        for name, rows in (("cqa", HEAD_DIM), ("sqa", HEAD_DIM), ("cqb", DIFF_DIM), ("sqb", DIFF_DIM)):
            args.append(consts[name])
            specs.append(pl.BlockSpec((rows, TOKEN_BLOCK), lambda b, l, j: (0, j)))
        for name in ("cka", "skpa", "skma", "ckb", "skpb", "skmb"):
            args.append(consts[name])
            specs.append(_const_spec(consts[name].shape))

    y_shape = jax.ShapeDtypeStruct(x.shape, _F32)
    if nb == 1:
        y_spec = pl.BlockSpec((1, TOKEN_BLOCK, D_MODEL), lambda b, l, j: (b, jnp.where(l == last, j, 0), 0))
    else:
        y_spec = pl.BlockSpec((nb, seq, D_MODEL), lambda b, l, j: (b, 0, 0))
    if latent:
        out_shape, out_specs = y_shape, y_spec
    else:
        out_shape = (y_shape,
                     jax.ShapeDtypeStruct((n_batch, DEPTH, 2, KV_WIDTH_A, seq), _F32),
                     jax.ShapeDtypeStruct((n_batch, DEPTH, 2, WIDTH_B, seq), _F32))
        out_specs = (y_spec,
                     pl.BlockSpec((nb, 1, 2, KV_WIDTH_A, seq), lambda b, l, j: (b, l, 0, 0, 0)),
                     pl.BlockSpec((nb, 1, 2, WIDTH_B, seq), lambda b, l, j: (b, l, 0, 0, 0)))

    n_keys = grp + past
    scratch = [
        pltpu.VMEM((grp, D_MODEL), _F32),
        pltpu.VMEM((grp, D_MODEL), _BF16),
        pltpu.VMEM((n_keys, KV_WIDTH_A), _BF16),
        pltpu.VMEM((n_keys, WIDTH_B), _BF16),
        pltpu.VMEM((N_V_GROUPS * V_GROUP_ROWS, n_keys), _BF16),
        pltpu.VMEM((WIDTH_C, grp), _BF16),
        pltpu.VMEM((WIDTH_C, grp), _BF16),
        pltpu.VMEM((TOKEN_BLOCK, D_MODEL), _BF16),
        pltpu.VMEM((SEQ_ROWS, KV_WIDTH_A), _F32),
        pltpu.VMEM((SEQ_ROWS, WIDTH_B), _F32),
    ]
    kern = functools.partial(_mixer_kernel, latent=latent, seq=seq, nb=nb, past=past, exact=exact)
    return pl.pallas_call(
        kern,
        grid=(n_batch // nb, DEPTH, n_blk),
        in_specs=specs,
        out_specs=out_specs,
        out_shape=out_shape,
        scratch_shapes=scratch,
        compiler_params=pltpu.CompilerParams(
            dimension_semantics=("arbitrary", "arbitrary", "arbitrary"),
            vmem_limit_bytes=V7X_VMEM_LIMIT_BYTES),
        name=("latent_mixer" if latent else "context_mixer") + ("_exact" if exact else ""),
    )(*args)


def _rope_tables(n_tok, dim):
    m = dim // 4
    t = np.arange(n_tok)
    inv = 1.0 / (ROPE_BASE ** (np.arange(m, dtype=np.float64) / m))
    ar = (t // GRID_W)[:, None] * inv
    ac = (t % GRID_W)[:, None] * inv
    cos = np.concatenate([np.cos(ar), np.cos(ar), np.cos(ac), np.cos(ac)], axis=1)
    sin = np.concatenate([-np.sin(ar), np.sin(ar), -np.sin(ac), np.sin(ac)], axis=1)
    first = np.tile(np.concatenate([np.ones(m), np.zeros(m)]), 2)[None, :]
    return cos.astype(np.float32), sin.astype(np.float32), first.astype(np.float32)


def _block_diag(block, n):
    out = np.zeros((block.shape[0] * n, block.shape[1] * n), np.float64)
    for i in range(n):
        out[i * block.shape[0]:(i + 1) * block.shape[0], i * block.shape[1]:(i + 1) * block.shape[1]] = block
    return out


def _dft_cos_sin(n):
    k = np.arange(n)
    ang = 2.0 * np.pi * ((k[:, None] * k[None, :]) % n) / n
    return np.cos(ang), np.sin(ang)


def _mixer_consts(seq, latent):
    c64, s64 = _dft_cos_sin(GROUP_C)
    norm = 1.0 / math.sqrt(GROUP_C * seq)
    cs, ss = _dft_cos_sin(seq)
    f32c = lambda a: jnp.asarray(np.asarray(a, np.float32))
    width = min(seq, TOKEN_BLOCK)
    col_blocks = lambda a: a.reshape(seq, seq // width, width).transpose(1, 0, 2)
    consts = {
        "bd64": f32c(_block_diag(np.full((HEAD_DIM, HEAD_DIM), 1.0 / HEAD_DIM), N_KV_A)).astype(_BF16),
        "bd32": f32c(_block_diag(np.full((DIFF_DIM, DIFF_DIM), 1.0 / DIFF_DIM), 2 * N_HEADS_B)).astype(_BF16),
        "bc": f32c(_block_diag(c64, N_GROUPS_C) * norm).astype(_BF16),
        "bs": f32c(_block_diag(s64, N_GROUPS_C) * norm).astype(_BF16),
        "cs": f32c(col_blocks(cs)).astype(_BF16),
        "nss": f32c(col_blocks(-ss)).astype(_BF16),
    }
    if latent:
        cos_a, sin_a, first_a = _rope_tables(seq, HEAD_DIM)
        cos_b, sin_b, first_b = _rope_tables(seq, DIFF_DIM)
        rep_a, rep_b = LANES // HEAD_DIM, LANES // DIFF_DIM
        consts.update({
            "cqa": f32c(cos_a.T), "sqa": f32c(sin_a.T), "cqb": f32c(cos_b.T), "sqb": f32c(sin_b.T),
            "cka": f32c(np.tile(cos_a, (1, rep_a))),
            "skpa": f32c(np.tile(sin_a * (1.0 - first_a), (1, rep_a))),
            "skma": f32c(np.tile(sin_a * first_a, (1, rep_a))),
            "ckb": f32c(np.tile(cos_b, (1, rep_b))),
            "skpb": f32c(np.tile(sin_b * (1.0 - first_b), (1, rep_b))),
            "skmb": f32c(np.tile(sin_b * first_b, (1, rep_b))),
        })
    return consts


def _mixer_weights(norm_g, w_in, q_norm_a, k_norm_a, q_norm_b, k_norm_b,
                   lambda_q1, lambda_k1, lambda_q2, lambda_k2, subln_g, w_fourier, w_out):
    keep = jnp.asarray([1.0 - v for v in LAM_INIT], _F32)
    gqa = q_norm_a * (HEAD_DIM ** -0.5 * LOG2_E)
    gqb = q_norm_b * (DIFF_DIM ** -0.5 * LOG2_E)
    row = lambda v: jnp.pad(v, ((0, 0), (0, P_LANES - v.shape[1])))
    prow = jnp.stack([row(jnp.tile(k_norm_a, (1, N_KV_A))), row(jnp.tile(k_norm_b, (1, 2 * N_HEADS_B))),
                      row(jnp.concatenate([lambda_q1, lambda_k1, lambda_q2, lambda_k2], axis=1))]
                     + [jnp.zeros((DEPTH, P_LANES), _F32)] * (P_ROWS - 3), axis=1)
    col = lambda v: jnp.pad(v, ((0, 0), (0, HEAD_DIM - v.shape[1])))
    pcol = jnp.stack([col(gqa), col(gqb), col(subln_g * keep[:, None])]
                     + [jnp.zeros((DEPTH, HEAD_DIM), _F32)] * (C_COLS - 3), axis=2)
    return {
        "norm_g": norm_g.reshape(DEPTH, 1, D_MODEL),
        "wft": _relayout(w_in, FEATURE_TILES, True, "w_in_feature_major"),
        "wt": _relayout(w_in, TOKEN_TILES, False, "w_in_token_major"),
        "wct": _relayout(w_fourier, (0,), True, "w_fourier_t"),
        "wout": _relayout(w_out, tuple(range(D_MODEL // WEIGHT_TILE)), False, "w_out_bf16"),
        "prow": prow,
        "pcol": pcol,
    }


def kernel(x_prompt, x_sample, cache_attn_a, cache_attn_b, c, c_ctx, norm_g, w_mod, b_mod, w_in,
           q_norm_a, k_norm_a, q_norm_b, k_norm_b, lambda_q1, lambda_k1, lambda_q2, lambda_k2,
           subln_g, w_fourier, w_out):
    n_ctx_batch, ctx_seq, _ = x_prompt.shape
    n_lat_batch, lat_seq, _ = x_sample.shape
    past = cache_attn_a.shape[2]
    assert n_lat_batch + 1 <= MOD_ROWS and w_in.shape == (DEPTH, D_MODEL, D_IN)

    cond = jnp.concatenate(
        [c_ctx[None, :], c, jnp.zeros((MOD_ROWS - 1 - n_lat_batch, D_MODEL), _F32)], axis=0)
    mod = _modulation(cond, w_mod, b_mod)

    weights = _mixer_weights(norm_g, w_in, q_norm_a, k_norm_a, q_norm_b, k_norm_b,
                             lambda_q1, lambda_k1, lambda_q2, lambda_k2, subln_g, w_fourier, w_out)

    def feature_major(cache):
        n_heads, dim = cache.shape[-2:]
        return jnp.transpose(cache, (0, 1, 3, 4, 5, 2)).reshape(n_lat_batch, DEPTH, 2, n_heads * dim, past)

    def token_major(new, n_heads):
        seq = new.shape[-1]
        return jnp.transpose(new.reshape(n_ctx_batch, DEPTH, 2, n_heads, HEAD_DIM, seq), (0, 1, 5, 2, 3, 4))

    ctx_a, ctx_b = feature_major(cache_attn_a), feature_major(cache_attn_b)

    def max_norm(gain, dim):
        return math.sqrt(dim) * jnp.max(jnp.abs(gain), axis=-1)

    def cached_norm(ctx, dim):
        k = ctx[:, :, 0].reshape(n_lat_batch, DEPTH, -1, dim, past)
        return jnp.sqrt(jnp.max(jnp.sum(k * k, axis=3), axis=(0, 2, 3)))

    q_a = max_norm(q_norm_a * (HEAD_DIM ** -0.5 * LOG2_E), HEAD_DIM)
    q_b = max_norm(q_norm_b * (DIFF_DIM ** -0.5 * LOG2_E), DIFF_DIM)
    k_a, k_b = max_norm(k_norm_a, HEAD_DIM), max_norm(k_norm_b, DIFF_DIM)
    ctx_bound = GUARD_SLACK * jnp.maximum(q_a * k_a, q_b * k_b)
    lat_bound = GUARD_SLACK * jnp.maximum(q_a * jnp.maximum(k_a, cached_norm(ctx_a, HEAD_DIM)),
                                          q_b * jnp.maximum(k_b, cached_norm(ctx_b, DIFF_DIM)))

    ctx_consts = _mixer_consts(ctx_seq, latent=False)
    lat_consts = _mixer_consts(lat_seq, latent=True)

    def context(exact):
        return _mixer(x_prompt, mod, weights, ctx_consts, nb=TOKEN_BLOCK // ctx_seq, exact=exact)

    def latent(exact):
        return _mixer(x_sample, mod, weights, lat_consts, nb=1, exact=exact, ctx=(ctx_a, ctx_b))

    y_prompt, new_a, new_b = lax.cond(jnp.all(ctx_bound <= MAX_SCORE_BOUND),
                                      functools.partial(context, False), functools.partial(context, True))
    y_sample = lax.cond(jnp.all(lat_bound <= MAX_SCORE_BOUND),
                        functools.partial(latent, False), functools.partial(latent, True))
    return (y_prompt, y_sample, token_major(new_a, N_KV_A), token_major(new_b, N_HEADS_B))
```

```python
import functools
import math

import jax
import jax.numpy as jnp
import numpy as np
from jax import lax
from jax.experimental import pallas as pl
from jax.experimental.pallas import tpu as pltpu

D_MODEL = 1024
DEPTH = 2
GRID_W = 64
HEAD_DIM = 64
N_HEADS_A = 8
N_KV_A = 2
N_HEADS_B = 4
DIFF_DIM = 32
GROUP_C = 64
N_GROUPS_C = 4
WIDTH_A = N_HEADS_A * HEAD_DIM
WIDTH_B = N_HEADS_B * HEAD_DIM
WIDTH_C = N_GROUPS_C * GROUP_C
KV_WIDTH_A = N_KV_A * HEAD_DIM
D_IN = 2 * WIDTH_A + 2 * KV_WIDTH_A + 4 * WIDTH_B + 2 * WIDTH_C
RMS_EPS = 1e-6
ROPE_BASE = 10000.0
LOG2_E = math.log2(math.e)
LAM_INIT = tuple(0.8 - 0.6 * math.exp(-0.3 * l) for l in range(DEPTH))

LANES = 128
TOKEN_BLOCK = 512
MOD_ROWS = 8
WEIGHT_TILE = 256
V7X_VMEM_LIMIT_BYTES = 58 * 1024 * 1024
BOUND_SLACK = 1.02
MAX_SCORE_BOUND = 40.0
GUARD_SLACK = 1.05
SEQ_ROWS = 8
P_ROWS, P_LANES = 8, 256
P_GKA, P_GKB, P_LAM = 0, 1, 2
C_COLS = 8
C_GQA, C_GQB, C_GSUB = 0, 1, 2

FEATURE_TILES = ((0, 1), (3, 4), (5, 8), (10, 9))
TOKEN_TILES = ((2, 6, 7),)
F_QA, F_GA, F_QB, F_GB, F_GC, F_UC = 0, 512, 1024, 1280, 1536, 1792
PF_CHUNK = 256
OUT_CHUNK = 256
F_ROWS = 2048
T_KA, T_VA, T_KB, T_VB = 0, 128, 256, 512
T_COLS = 768
ONES_ROWS = 16
V_GROUP_ROWS = HEAD_DIM + ONES_ROWS
N_V_GROUPS = N_KV_A + N_HEADS_B

_BF16 = jnp.bfloat16
_F32 = jnp.float32


def _silu(x):
    return x * (1.0 / (1.0 + jnp.exp(-x)))


def _dot(a, b):
    return jnp.dot(a, b, preferred_element_type=_F32)


def _dot_nt(a, b):
    return lax.dot_general(a, b, (((1,), (1,)), ((), ())), preferred_element_type=_F32)


def _mod_kernel(c_ref, w_ref, b_ref, o_ref):
    c = c_ref[...]
    o_ref[0, 0] = _dot(_silu(c).astype(_BF16), w_ref[0].astype(_BF16)) + b_ref[0]


def _modulation(cond, w_mod, b_mod):
    n_tile = D_MODEL
    return pl.pallas_call(
        _mod_kernel,
        grid=(DEPTH, 3 * D_MODEL // n_tile),
        in_specs=[
            pl.BlockSpec((MOD_ROWS, D_MODEL), lambda l, n: (0, 0)),
            pl.BlockSpec((1, D_MODEL, n_tile), lambda l, n: (l, 0, n)),
            pl.BlockSpec((1, 1, n_tile), lambda l, n: (l, 0, n)),
        ],
        out_specs=pl.BlockSpec((1, 1, MOD_ROWS, n_tile), lambda l, n: (l, n, 0, 0)),
        out_shape=jax.ShapeDtypeStruct((DEPTH, 3, MOD_ROWS, D_MODEL), _F32),
        name="modulation",
    )(cond, w_mod, b_mod.reshape(DEPTH, 1, 3 * D_MODEL))


def _relayout_kernel(tiles_ref, *refs, transpose):
    del tiles_ref
    *w_refs, o_ref = refs
    for k, w_ref in enumerate(w_refs):
        w = w_ref[0]
        width = w.shape[1]
        if transpose:
            o_ref[0, k * width:(k + 1) * width, :] = w.T.astype(_BF16)
        else:
            o_ref[0, :, k * width:(k + 1) * width] = w.astype(_BF16)


def _relayout(w, groups, width, transpose, name):
    depth, rows, _ = w.shape
    n, g = len(groups), len(groups[0])
    assert all(len(grp) == g for grp in groups)
    if transpose:
        out_shape = (depth, n * g * width, rows)
        out_spec = pl.BlockSpec((1, g * width, rows), lambda l, i, t: (l, i, 0))
    else:
        out_shape = (depth, rows, n * g * width)
        out_spec = pl.BlockSpec((1, rows, g * width), lambda l, i, t: (l, 0, i))

    def tile_spec(k):
        return pl.BlockSpec((1, rows, width), lambda l, i, t: (l, 0, t[i * g + k]))

    return pl.pallas_call(
        functools.partial(_relayout_kernel, transpose=transpose),
        grid_spec=pltpu.PrefetchScalarGridSpec(
            num_scalar_prefetch=1,
            grid=(depth, n),
            in_specs=[tile_spec(k) for k in range(g)],
            out_specs=out_spec),
        out_shape=jax.ShapeDtypeStruct(out_shape, _BF16),
        name=name,
    )(jnp.asarray(groups, jnp.int32).reshape(-1), *([w] * g))


def _key_norm_kernel(ka_ref, kb_ref, o_ref):
    def max_norm2(k, dim):
        k2 = k * k
        per_head = [jnp.sum(k2[r:r + dim], axis=0, keepdims=True) for r in range(0, k.shape[0], dim)]
        return jnp.max(functools.reduce(jnp.maximum, per_head), axis=1, keepdims=True)

    o_ref[0, 0, 0:1, :] = jnp.broadcast_to(max_norm2(ka_ref[0, 0, 0], HEAD_DIM), (1, LANES))
    o_ref[0, 0, 1:2, :] = jnp.broadcast_to(max_norm2(kb_ref[0, 0, 0], DIFF_DIM), (1, LANES))
    o_ref[0, 0, 2:, :] = jnp.zeros((SEQ_ROWS - 2, LANES), _F32)


def _cached_key_norms(ctx_a, ctx_b):
    n_batch, _, _, _, past = ctx_a.shape
    n2 = pl.pallas_call(
        _key_norm_kernel,
        grid=(n_batch, DEPTH),
        in_specs=[pl.BlockSpec((1, 1, 1, KV_WIDTH_A, past), lambda b, l: (b, l, 0, 0, 0)),
                  pl.BlockSpec((1, 1, 1, WIDTH_B, past), lambda b, l: (b, l, 0, 0, 0))],
        out_specs=pl.BlockSpec((1, 1, SEQ_ROWS, LANES), lambda b, l: (b, l, 0, 0)),
        out_shape=jax.ShapeDtypeStruct((n_batch, DEPTH, SEQ_ROWS, LANES), _F32),
        name="cached_key_norms",
    )(ctx_a, ctx_b)
    return jnp.sqrt(jnp.max(n2[:, :, 0, 0], axis=0)), jnp.sqrt(jnp.max(n2[:, :, 1, 0], axis=0))


def _softmax_t(s_t, shift=None):
    if shift is None:
        shift = jnp.max(s_t, axis=0, keepdims=True)
    return jnp.exp2(s_t - shift).astype(_BF16)


def _rms_rows(x):
    return lax.rsqrt(jnp.mean(x * x, axis=0, keepdims=True) + RMS_EPS)


def _swap_halves(x, m):
    return jnp.concatenate([x[m:2 * m], x[0:m], x[3 * m:4 * m], x[2 * m:3 * m]], axis=0)


def _pad_rows(x, start, total):
    parts = []
    if start:
        parts.append(jnp.zeros((start, x.shape[1]), x.dtype))
    parts.append(x)
    rest = total - start - x.shape[0]
    if rest:
        parts.append(jnp.zeros((rest, x.shape[1]), x.dtype))
    return jnp.concatenate(parts, axis=0)


def _cat(parts, axis):
    return parts[0] if len(parts) == 1 else jnp.concatenate(parts, axis=axis)


def _mixer_kernel(*refs, latent, seq, nb, past, exact):
    grp = nb * seq
    n_blk = grp // TOKEN_BLOCK
    tb = TOKEN_BLOCK // nb
    it = iter(refs)
    x_ref, mod_ref, ng_ref = next(it), next(it), next(it)
    wft_ref, wt_ref, wct_ref, wout_ref = next(it), next(it), next(it), next(it)
    prow_ref, pcol_ref = next(it), next(it)
    bd64_ref, bd32_ref, bc_ref, bs_ref, cs_ref, nss_ref = (next(it) for _ in range(6))
    if latent:
        ctxa_ref, ctxb_ref = next(it), next(it)
        cqa_ref, sqa_ref, cqb_ref, sqb_ref = (next(it) for _ in range(4))
        cka_ref, skpa_ref, skma_ref, ckb_ref, skpb_ref, skmb_ref = (next(it) for _ in range(6))
        y_ref = next(it)
    else:
        y_ref, newa_ref, newb_ref = next(it), next(it), next(it)
    xres_s, h_s, ka_s, kb_s, vt_s, t1_s, t2_s, mix_s, kn2a_s, kn2b_s = (next(it) for _ in range(10))

    l = pl.program_id(1)
    j = pl.program_id(2)
    mod_row = pl.ds(pl.program_id(0) + 1 if latent else 0, 1)
    shift = mod_ref[0, 0, mod_row, :]
    scale = mod_ref[0, 1, mod_row, :]
    gate = mod_ref[0, 2, mod_row, :]
    prow, pcol = prow_ref[l], pcol_ref[l]
    gka, gkb = prow[P_GKA:P_GKA + 1, 0:KV_WIDTH_A], prow[P_GKB:P_GKB + 1, :]
    gqa, gqb, gsub = pcol[:, C_GQA:C_GQA + 1], pcol[0:DIFF_DIM, C_GQB:C_GQB + 1], pcol[:, C_GSUB:C_GSUB + 1]

    @pl.when((l == 0) & (j == 0))
    def _():
        xres_s[...] = x_ref[...].reshape(grp, D_MODEL)
        for g in range(N_V_GROUPS):
            vt_s[g * V_GROUP_ROWS + HEAD_DIM:(g + 1) * V_GROUP_ROWS, :] = jnp.ones(
                (ONES_ROWS, grp + past), _BF16)

    def rope_k(k, cos_ref, sp_ref, sm_ref, rows, m):
        out = []
        for c0 in range(0, k.shape[1], LANES):
            kc = k[:, c0:c0 + LANES]
            out.append(kc * cos_ref[rows, :] + pltpu.roll(kc, m, 1) * sp_ref[rows, :]
                       + pltpu.roll(kc, LANES - m, 1) * sm_ref[rows, :])
        return _cat(out, 1)

    def note_key_norms(i, ka, kb, first):
        for k, bd_ref, dim, kn2_s in ((ka, bd64_ref, HEAD_DIM, kn2a_s), (kb, bd32_ref, DIFF_DIM, kn2b_s)):
            n2 = jnp.max(_dot((k * k).astype(_BF16), bd_ref[...]), axis=0, keepdims=True) * float(dim)
            kn2_s[i:i + 1, :] = n2 if first else jnp.maximum(kn2_s[i:i + 1, :], n2)

    def put_values(vt, cols):
        for g in range(N_V_GROUPS):
            vt_s[g * V_GROUP_ROWS:g * V_GROUP_ROWS + HEAD_DIM, cols] = vt[g * HEAD_DIM:(g + 1) * HEAD_DIM]

    @pl.when(j == 0)
    def _prep():
        for c in range(n_blk):
            rows = slice(c * TOKEN_BLOCK, (c + 1) * TOKEN_BLOCK)
            xc = xres_s[rows, :]
            ms = jnp.mean(xc * xc, axis=-1, keepdims=True)
            hc = (xc * lax.rsqrt(ms + RMS_EPS) * ng_ref[l] * (1.0 + scale) + shift).astype(_BF16)
            h_s[rows, :] = hc
            pt = _dot(hc, wt_ref[l])
            ka = pt[:, T_KA:T_KA + KV_WIDTH_A]
            va = pt[:, T_VA:T_VA + KV_WIDTH_A]
            kb = pt[:, T_KB:T_KB + WIDTH_B]
            vb = pt[:, T_VB:T_VB + WIDTH_B]
            vat, vbt = va.T, vb.T
            ka = ka * lax.rsqrt(_dot((ka * ka).astype(_BF16), bd64_ref[...]) + RMS_EPS) * gka
            kb = kb * lax.rsqrt(_dot((kb * kb).astype(_BF16), bd32_ref[...]) + RMS_EPS) * gkb
            if latent:
                ka = rope_k(ka, cka_ref, skpa_ref, skma_ref, rows, HEAD_DIM // 4)
                kb = rope_k(kb, ckb_ref, skpb_ref, skmb_ref, rows, DIFF_DIM // 4)
            else:
                kat, kbt = ka.T, kb.T
                for i in range(nb):
                    r = slice(i * seq, (i + 1) * seq)
                    newa_ref[i, 0, 0], newa_ref[i, 0, 1] = kat[:, r], vat[:, r]
                    newb_ref[i, 0, 0], newb_ref[i, 0, 1] = kbt[:, r], vbt[:, r]
            ka_s[rows, :] = ka.astype(_BF16)
            kb_s[rows, :] = kb.astype(_BF16)
            if not exact:
                for i in range(nb):
                    r = slice(i * seq, (i + 1) * seq) if nb > 1 else slice(None)
                    note_key_norms(i, ka[r], kb[r], first=(c == 0))
            put_values(jnp.concatenate([vat, vbt], axis=0).astype(_BF16), rows)
            uct = _dot_nt(wft_ref[l, F_UC:F_ROWS, :], hc).astype(_BF16)
            t1_s[:, rows] = _dot(bc_ref[...], uct).astype(_BF16)
            t2_s[:, rows] = _dot(bs_ref[...], uct).astype(_BF16)
        if latent:
            ka_ctx, kb_ctx = ctxa_ref[0, 0, 0].T, ctxb_ref[0, 0, 0].T
            ka_s[grp:grp + past, :] = ka_ctx.astype(_BF16)
            kb_s[grp:grp + past, :] = kb_ctx.astype(_BF16)
            if not exact:
                note_key_norms(0, ka_ctx, kb_ctx, first=False)
            put_values(jnp.concatenate([ctxa_ref[0, 0, 1], ctxb_ref[0, 0, 1]], axis=0).astype(_BF16),
                       slice(grp, grp + past))

    blk_rows = pl.ds(pl.multiple_of(j * TOKEN_BLOCK, TOKEN_BLOCK), TOKEN_BLOCK)

    pf_chunks = {}

    def pf_chunk(c):
        if c not in pf_chunks:
            pf_chunks[c] = _dot_nt(wft_ref[l, c * PF_CHUNK:(c + 1) * PF_CHUNK, :], h_s[blk_rows, :])
        return pf_chunks[c]

    def pf_rows(r0, n, cols):
        c, off = divmod(r0, PF_CHUNK)
        assert off + n <= PF_CHUNK
        return pf_chunk(c)[off:off + n, cols]

    lv = prow[P_LAM:P_LAM + 1, :]
    lq1, lk1, lq2, lk2 = (lv[:, i * DIFF_DIM:(i + 1) * DIFF_DIM] for i in range(4))
    lam_init = jnp.where(l == 0, LAM_INIT[0], LAM_INIT[1])
    lam = (jnp.exp(jnp.sum(lq1 * lk1, axis=-1, keepdims=True))
           - jnp.exp(jnp.sum(lq2 * lk2, axis=-1, keepdims=True)) + lam_init)

    def q_tile(r0, dim, gain, cos_ref, sin_ref, cols, pad_start, pad_total, kn2_s):
        q = pf_rows(r0, dim, cols)
        q = q * _rms_rows(q) * gain
        if latent:
            q = q * cos_ref[...] + _swap_halves(q, dim // 4) * sin_ref[...]
        bound = None
        if not exact:
            i = cols.start // tb if nb > 1 else 0
            kn2 = kn2_s[i:i + 1, pad_start:pad_start + 1]
            bound = jnp.sqrt(jnp.sum(q * q, axis=0, keepdims=True) * kn2) * BOUND_SLACK
        return _pad_rows(q.astype(_BF16), pad_start, pad_total), bound

    def q_a(h, cols):
        g = h // (N_HEADS_A // N_KV_A)
        return q_tile(F_QA + h * HEAD_DIM, HEAD_DIM, gqa, cqa_ref if latent else None,
                      sqa_ref if latent else None, cols, g * HEAD_DIM, KV_WIDTH_A, kn2a_s)

    def q_b(h, comp, cols):
        r0 = h * HEAD_DIM + comp * DIFF_DIM
        return q_tile(F_QB + r0, DIFF_DIM, gqb, cqb_ref if latent else None,
                      sqb_ref if latent else None, cols, r0, WIDTH_B, kn2b_s)

    def put_mix(gated_t, cols, c0):
        mix_s[cols, c0:c0 + gated_t.shape[0]] = gated_t.T.astype(_BF16)

    half = {}

    def put_head(slot, cols, gated_t):
        if slot % 2 == 0:
            half[cols.start] = gated_t
        else:
            put_mix(jnp.concatenate([half.pop(cols.start), gated_t], axis=0), cols, (slot - 1) * HEAD_DIM)

    def store_a(h, cols, o):
        ga = pf_rows(F_GA + h * HEAD_DIM, HEAD_DIM, cols)
        put_head(h, cols, o * _silu(ga))

    def store_b(h, cols, o1, o2):
        ob = o1 - lam * o2
        ob = ob * _rms_rows(ob) * gsub
        gb = pf_rows(F_GB + h * HEAD_DIM, HEAD_DIM, cols)
        put_head(N_HEADS_A + h, cols, ob * _silu(gb))

    def v_group(g):
        return slice(g * V_GROUP_ROWS, (g + 1) * V_GROUP_ROWS)

    units = []
    all_cols = slice(0, TOKEN_BLOCK)
    if nb == 1:
        keys = slice(0, grp + past)
        for h in range(N_HEADS_A):
            g = h // (N_HEADS_A // N_KV_A)
            units.append(("a", keys, v_group(g),
                          [functools.partial(q_a, h, all_cols)],
                          functools.partial(store_a, h, all_cols)))
        for h in range(N_HEADS_B):
            v_rows = v_group(N_KV_A + h)
            held = {}
            for comp in range(2):
                def fin(o, h=h, comp=comp, held=held):
                    held[comp] = o
                    if comp == 1:
                        store_b(h, all_cols, held[0], held[1])
                units.append(("b", keys, v_rows, [functools.partial(q_b, h, comp, all_cols)], fin))
    else:
        seqs = [(slice(i * seq, (i + 1) * seq), slice(i * tb, (i + 1) * tb)) for i in range(nb)]
        for u in range(N_HEADS_A // 2):
            g = (2 * u) // (N_HEADS_A // N_KV_A)
            for keys, cols in seqs:
                def fin(o, u=u, cols=cols):
                    store_a(2 * u, cols, o[:, 0:tb])
                    store_a(2 * u + 1, cols, o[:, tb:2 * tb])
                units.append(("a", keys, v_group(g),
                              [functools.partial(q_a, 2 * u, cols), functools.partial(q_a, 2 * u + 1, cols)], fin))
        for h in range(N_HEADS_B):
            for keys, cols in seqs:
                def fin(o, h=h, cols=cols):
                    store_b(h, cols, o[:, 0:tb], o[:, tb:2 * tb])
                units.append(("b", keys, v_group(N_KV_A + h),
                              [functools.partial(q_b, h, 0, cols), functools.partial(q_b, h, 1, cols)], fin))

    def scores(unit):
        kind, keys, _, tiles, _ = unit
        k_ref = ka_s if kind == "a" else kb_s
        qs, shifts = zip(*[t() for t in tiles])
        shift = None if exact else _cat(shifts, 1)
        return _dot(k_ref[keys, :], _cat(qs, 1)), shift

    def attend(unit, s_t, shift):
        o = _dot(vt_s[unit[2], unit[1]], _softmax_t(s_t, shift))
        unit[4](o[0:HEAD_DIM] / o[HEAD_DIM:HEAD_DIM + 1])

    def mixer_c():
        if nb == 1:
            ft = _dot(t1_s[...], cs_ref[j]) + _dot(t2_s[...], nss_ref[j])
        else:
            ft = _cat([_dot(t1_s[:, i * seq:(i + 1) * seq], cs_ref[0])
                       + _dot(t2_s[:, i * seq:(i + 1) * seq], nss_ref[0]) for i in range(nb)], 1)
        oc = _dot(wct_ref[l], ft.astype(_BF16))
        put_mix(oc * _silu(pf_rows(F_GC, WIDTH_C, all_cols)), all_cols, WIDTH_A + WIDTH_B)

    out_parts = []

    def out_chunk(k):
        cols = slice(k * OUT_CHUNK, (k + 1) * OUT_CHUNK)
        out_parts.append(_dot(mix_s[:, cols], wout_ref[l, cols, :]))

    fillers = {
        0: [functools.partial(pf_chunk, F_GA // PF_CHUNK)],
        1: [functools.partial(pf_chunk, F_QA // PF_CHUNK + 1)],
        2: [functools.partial(pf_chunk, F_GA // PF_CHUNK + 1)],
        3: [functools.partial(pf_chunk, F_QB // PF_CHUNK)],
        4: [functools.partial(pf_chunk, F_GB // PF_CHUNK)],
        5: [mixer_c],
        6: [functools.partial(out_chunk, 3), functools.partial(out_chunk, 0)],
        9: [functools.partial(out_chunk, 1)],
    }
    s_next = scores(units[0])
    for u, unit in enumerate(units):
        s_cur, shift = s_next
        if u + 1 < len(units):
            s_next = scores(units[u + 1])
        for work in fillers.get(u, ()):
            work()
        attend(unit, s_cur, shift)
    out_chunk(2)

    y = xres_s[blk_rows, :] + gate * functools.reduce(lambda a, b: a + b, out_parts)
    xres_s[blk_rows, :] = y

    @pl.when(l == DEPTH - 1)
    def _():
        y_ref[...] = y.reshape(y_ref.shape)


def _const_spec(shape):
    nd = len(shape)
    return pl.BlockSpec(shape, lambda b, l, j: (0,) * nd, pipeline_mode=pl.Buffered(1))


def _mixer(x, mod, wts, consts, nb, exact, ctx=None):
    latent = ctx is not None
    n_batch, seq, _ = x.shape
    grp = nb * seq
    n_blk = grp // TOKEN_BLOCK
    assert grp % TOKEN_BLOCK == 0 and (nb == 1 or grp == TOKEN_BLOCK) and n_batch % nb == 0 and nb <= SEQ_ROWS
    past = ctx[0].shape[-1] if latent else 0
    last = DEPTH - 1

    args = [x, mod]
    specs = [
        pl.BlockSpec((nb, seq, D_MODEL), lambda b, l, j: (b, 0, 0),
                     pipeline_mode=pl.Buffered(1) if latent else None),
        pl.BlockSpec((1, 3, MOD_ROWS, D_MODEL), lambda b, l, j: (l, 0, 0, 0)),
    ]
    for name in ("norm_g", "wft", "wt", "wct", "wout", "prow", "pcol"):
        args.append(wts[name])
        specs.append(_const_spec(wts[name].shape))
    for name in ("bd64", "bd32", "bc", "bs", "cs", "nss"):
        args.append(consts[name])
        specs.append(_const_spec(consts[name].shape))
    if latent:
        args += [ctx[0], ctx[1]]
        specs += [
            pl.BlockSpec((1, 1, 2, KV_WIDTH_A, past), lambda b, l, j: (b, l, 0, 0, 0), pipeline_mode=pl.Buffered(1)),
            pl.BlockSpec((1, 1, 2, WIDTH_B, past), lambda b, l, j: (b, l, 0, 0, 0), pipeline_mode=pl.Buffered(1)),
        ]
        for name, rows in (("cqa", HEAD_DIM), ("sqa", HEAD_DIM), ("cqb", DIFF_DIM), ("sqb", DIFF_DIM)):
            args.append(consts[name])
            specs.append(pl.BlockSpec((rows, TOKEN_BLOCK), lambda b, l, j: (0, j)))
        for name in ("cka", "skpa", "skma", "ckb", "skpb", "skmb"):
            args.append(consts[name])
            specs.append(_const_spec(consts[name].shape))

    y_shape = jax.ShapeDtypeStruct(x.shape, _F32)
    if nb == 1:
        y_spec = pl.BlockSpec((1, TOKEN_BLOCK, D_MODEL), lambda b, l, j: (b, jnp.where(l == last, j, 0), 0))
    else:
        y_spec = pl.BlockSpec((nb, seq, D_MODEL), lambda b, l, j: (b, 0, 0))
    if latent:
        out_shape, out_specs = y_shape, y_spec
    else:
        out_shape = (y_shape,
                     jax.ShapeDtypeStruct((n_batch, DEPTH, 2, KV_WIDTH_A, seq), _F32),
                     jax.ShapeDtypeStruct((n_batch, DEPTH, 2, WIDTH_B, seq), _F32))
        out_specs = (y_spec,
                     pl.BlockSpec((nb, 1, 2, KV_WIDTH_A, seq), lambda b, l, j: (b, l, 0, 0, 0)),
                     pl.BlockSpec((nb, 1, 2, WIDTH_B, seq), lambda b, l, j: (b, l, 0, 0, 0)))

    n_keys = grp + past
    scratch = [
        pltpu.VMEM((grp, D_MODEL), _F32),
        pltpu.VMEM((grp, D_MODEL), _BF16),
        pltpu.VMEM((n_keys, KV_WIDTH_A), _BF16),
        pltpu.VMEM((n_keys, WIDTH_B), _BF16),
        pltpu.VMEM((N_V_GROUPS * V_GROUP_ROWS, n_keys), _BF16),
        pltpu.VMEM((WIDTH_C, grp), _BF16),
        pltpu.VMEM((WIDTH_C, grp), _BF16),
        pltpu.VMEM((TOKEN_BLOCK, D_MODEL), _BF16),
        pltpu.VMEM((SEQ_ROWS, KV_WIDTH_A), _F32),
        pltpu.VMEM((SEQ_ROWS, WIDTH_B), _F32),
    ]
    kern = functools.partial(_mixer_kernel, latent=latent, seq=seq, nb=nb, past=past, exact=exact)
    return pl.pallas_call(
        kern,
        grid=(n_batch // nb, DEPTH, n_blk),
        in_specs=specs,
        out_specs=out_specs,
        out_shape=out_shape,
        scratch_shapes=scratch,
        compiler_params=pltpu.CompilerParams(
            dimension_semantics=("arbitrary", "arbitrary", "arbitrary"),
            vmem_limit_bytes=V7X_VMEM_LIMIT_BYTES),
        name=("latent_mixer" if latent else "context_mixer") + ("_exact" if exact else ""),
    )(*args)


def _rope_tables(n_tok, dim):
    m = dim // 4
    t = np.arange(n_tok)
    inv = 1.0 / (ROPE_BASE ** (np.arange(m, dtype=np.float64) / m))
    ar = (t // GRID_W)[:, None] * inv
    ac = (t % GRID_W)[:, None] * inv
    cos = np.concatenate([np.cos(ar), np.cos(ar), np.cos(ac), np.cos(ac)], axis=1)
    sin = np.concatenate([-np.sin(ar), np.sin(ar), -np.sin(ac), np.sin(ac)], axis=1)
    first = np.tile(np.concatenate([np.ones(m), np.zeros(m)]), 2)[None, :]
    return cos.astype(np.float32), sin.astype(np.float32), first.astype(np.float32)


def _block_diag(block, n):
    out = np.zeros((block.shape[0] * n, block.shape[1] * n), np.float64)
    for i in range(n):
        out[i * block.shape[0]:(i + 1) * block.shape[0], i * block.shape[1]:(i + 1) * block.shape[1]] = block
    return out


def _dft_cos_sin(n):
    k = np.arange(n)
    ang = 2.0 * np.pi * ((k[:, None] * k[None, :]) % n) / n
    return np.cos(ang), np.sin(ang)


def _mixer_consts(seq, latent):
    c64, s64 = _dft_cos_sin(GROUP_C)
    norm = 1.0 / math.sqrt(GROUP_C * seq)
    cs, ss = _dft_cos_sin(seq)
    f32c = lambda a: jnp.asarray(np.asarray(a, np.float32))
    width = min(seq, TOKEN_BLOCK)
    col_blocks = lambda a: a.reshape(seq, seq // width, width).transpose(1, 0, 2)
    consts = {
        "bd64": f32c(_block_diag(np.full((HEAD_DIM, HEAD_DIM), 1.0 / HEAD_DIM), N_KV_A)).astype(_BF16),
        "bd32": f32c(_block_diag(np.full((DIFF_DIM, DIFF_DIM), 1.0 / DIFF_DIM), 2 * N_HEADS_B)).astype(_BF16),
        "bc": f32c(_block_diag(c64, N_GROUPS_C) * norm).astype(_BF16),
        "bs": f32c(_block_diag(s64, N_GROUPS_C) * norm).astype(_BF16),
        "cs": f32c(col_blocks(cs)).astype(_BF16),
        "nss": f32c(col_blocks(-ss)).astype(_BF16),
    }
    if latent:
        cos_a, sin_a, first_a = _rope_tables(seq, HEAD_DIM)
        cos_b, sin_b, first_b = _rope_tables(seq, DIFF_DIM)
        rep_a, rep_b = LANES // HEAD_DIM, LANES // DIFF_DIM
        consts.update({
            "cqa": f32c(cos_a.T), "sqa": f32c(sin_a.T), "cqb": f32c(cos_b.T), "sqb": f32c(sin_b.T),
            "cka": f32c(np.tile(cos_a, (1, rep_a))),
            "skpa": f32c(np.tile(sin_a * (1.0 - first_a), (1, rep_a))),
            "skma": f32c(np.tile(sin_a * first_a, (1, rep_a))),
            "ckb": f32c(np.tile(cos_b, (1, rep_b))),
            "skpb": f32c(np.tile(sin_b * (1.0 - first_b), (1, rep_b))),
            "skmb": f32c(np.tile(sin_b * first_b, (1, rep_b))),
        })
    return consts


def _mixer_weights(norm_g, w_in, q_norm_a, k_norm_a, q_norm_b, k_norm_b,
                   lambda_q1, lambda_k1, lambda_q2, lambda_k2, subln_g, w_fourier, w_out):
    keep = jnp.asarray([1.0 - v for v in LAM_INIT], _F32)
    gqa = q_norm_a * (HEAD_DIM ** -0.5 * LOG2_E)
    gqb = q_norm_b * (DIFF_DIM ** -0.5 * LOG2_E)
    row = lambda v: jnp.pad(v, ((0, 0), (0, P_LANES - v.shape[1])))
    prow = jnp.stack([row(jnp.tile(k_norm_a, (1, N_KV_A))), row(jnp.tile(k_norm_b, (1, 2 * N_HEADS_B))),
                      row(jnp.concatenate([lambda_q1, lambda_k1, lambda_q2, lambda_k2], axis=1))]
                     + [jnp.zeros((DEPTH, P_LANES), _F32)] * (P_ROWS - 3), axis=1)
    col = lambda v: jnp.pad(v, ((0, 0), (0, HEAD_DIM - v.shape[1])))
    pcol = jnp.stack([col(gqa), col(gqb), col(subln_g * keep[:, None])]
                     + [jnp.zeros((DEPTH, HEAD_DIM), _F32)] * (C_COLS - 3), axis=2)
    return {
        "norm_g": norm_g.reshape(DEPTH, 1, D_MODEL),
        "wft": _relayout(w_in, FEATURE_TILES, WEIGHT_TILE, True, "w_in_feature_major"),
        "wt": _relayout(w_in, TOKEN_TILES, WEIGHT_TILE, False, "w_in_token_major"),
        "wct": _relayout(w_fourier, ((0,),), WIDTH_C, True, "w_fourier_t"),
        "wout": _relayout(w_out, ((0,),), D_MODEL, False, "w_out_bf16"),
        "prow": prow,
        "pcol": pcol,
    }


def kernel(x_prompt, x_sample, cache_attn_a, cache_attn_b, c, c_ctx, norm_g, w_mod, b_mod, w_in,
           q_norm_a, k_norm_a, q_norm_b, k_norm_b, lambda_q1, lambda_k1, lambda_q2, lambda_k2,
           subln_g, w_fourier, w_out):
    n_ctx_batch, ctx_seq, _ = x_prompt.shape
    n_lat_batch, lat_seq, _ = x_sample.shape
    past = cache_attn_a.shape[2]
    assert n_lat_batch + 1 <= MOD_ROWS and w_in.shape == (DEPTH, D_MODEL, D_IN)

    cond = jnp.concatenate(
        [c_ctx[None, :], c, jnp.zeros((MOD_ROWS - 1 - n_lat_batch, D_MODEL), _F32)], axis=0)
    mod = _modulation(cond, w_mod, b_mod)

    weights = _mixer_weights(norm_g, w_in, q_norm_a, k_norm_a, q_norm_b, k_norm_b,
                             lambda_q1, lambda_k1, lambda_q2, lambda_k2, subln_g, w_fourier, w_out)

    def feature_major(cache):
        n_heads, dim = cache.shape[-2:]
        return jnp.transpose(cache, (0, 1, 3, 4, 5, 2)).reshape(n_lat_batch, DEPTH, 2, n_heads * dim, past)

    def token_major(new, n_heads):
        seq = new.shape[-1]
        return jnp.transpose(new.reshape(n_ctx_batch, DEPTH, 2, n_heads, HEAD_DIM, seq), (0, 1, 5, 2, 3, 4))

    ctx_a, ctx_b = feature_major(cache_attn_a), feature_major(cache_attn_b)

    def max_norm(gain, dim):
        return math.sqrt(dim) * jnp.max(jnp.abs(gain), axis=-1)

    q_a = max_norm(q_norm_a * (HEAD_DIM ** -0.5 * LOG2_E), HEAD_DIM)
    q_b = max_norm(q_norm_b * (DIFF_DIM ** -0.5 * LOG2_E), DIFF_DIM)
    k_a, k_b = max_norm(k_norm_a, HEAD_DIM), max_norm(k_norm_b, DIFF_DIM)
    cached_a, cached_b = _cached_key_norms(ctx_a, ctx_b)
    ctx_bound = GUARD_SLACK * jnp.maximum(q_a * k_a, q_b * k_b)
    lat_bound = GUARD_SLACK * jnp.maximum(q_a * jnp.maximum(k_a, cached_a), q_b * jnp.maximum(k_b, cached_b))

    ctx_consts = _mixer_consts(ctx_seq, latent=False)
    lat_consts = _mixer_consts(lat_seq, latent=True)

    def context(exact):
        return _mixer(x_prompt, mod, weights, ctx_consts, nb=TOKEN_BLOCK // ctx_seq, exact=exact)

    def latent(exact):
        return _mixer(x_sample, mod, weights, lat_consts, nb=1, exact=exact, ctx=(ctx_a, ctx_b))

    y_prompt, new_a, new_b = lax.cond(jnp.all(ctx_bound <= MAX_SCORE_BOUND),
                                      functools.partial(context, False), functools.partial(context, True))
    y_sample = lax.cond(jnp.all(lat_bound <= MAX_SCORE_BOUND),
                        functools.partial(latent, False), functools.partial(latent, True))
    return (y_prompt, y_sample, token_major(new_a, N_KV_A), token_major(new_b, N_HEADS_B))
```

```python
import functools
import math

import jax
import jax.numpy as jnp
import numpy as np
from jax import lax
from jax.experimental import pallas as pl
from jax.experimental.pallas import tpu as pltpu

D_MODEL = 1024
DEPTH = 2
GRID_W = 64
HEAD_DIM = 64
N_HEADS_A = 8
N_KV_A = 2
N_HEADS_B = 4
DIFF_DIM = 32
GROUP_C = 64
N_GROUPS_C = 4
WIDTH_A = N_HEADS_A * HEAD_DIM
WIDTH_B = N_HEADS_B * HEAD_DIM
WIDTH_C = N_GROUPS_C * GROUP_C
KV_WIDTH_A = N_KV_A * HEAD_DIM
D_IN = 2 * WIDTH_A + 2 * KV_WIDTH_A + 4 * WIDTH_B + 2 * WIDTH_C
RMS_EPS = 1e-6
ROPE_BASE = 10000.0
LOG2_E = math.log2(math.e)
LAM_INIT = tuple(0.8 - 0.6 * math.exp(-0.3 * l) for l in range(DEPTH))

LANES = 128
TOKEN_BLOCK = 512
MOD_ROWS = 8
WEIGHT_TILE = 256
V7X_VMEM_LIMIT_BYTES = 58 * 1024 * 1024
BOUND_SLACK = 1.02
MAX_SCORE_BOUND = 40.0
GUARD_SLACK = 1.05
SEQ_ROWS = 8
P_ROWS, P_LANES = 8, 256
P_GKA, P_GKB, P_LAM = 0, 1, 2
C_COLS = 8
C_GQA, C_GQB, C_GSUB = 0, 1, 2

FEATURE_TILES = ((0, 1), (3, 4), (5, 8), (10, 9))
TOKEN_TILES = ((2, 6, 7),)
F_QA, F_GA, F_QB, F_GB, F_GC, F_UC = 0, 512, 1024, 1280, 1536, 1792
PF_CHUNK = 256
OUT_CHUNK = 256
KEY_CHUNK = 256
F_ROWS = 2048
T_KA, T_VA, T_KB, T_VB = 0, 128, 256, 512
T_COLS = 768
ONES_ROWS = 16
V_GROUP_ROWS = HEAD_DIM + ONES_ROWS
N_V_GROUPS = N_KV_A + N_HEADS_B

_BF16 = jnp.bfloat16
_F32 = jnp.float32


def _silu(x):
    return x * (1.0 / (1.0 + jnp.exp(-x)))


def _dot(a, b):
    return jnp.dot(a, b, preferred_element_type=_F32)


def _dot_nt(a, b):
    return lax.dot_general(a, b, (((1,), (1,)), ((), ())), preferred_element_type=_F32)


def _mod_kernel(c_ref, w_ref, b_ref, o_ref):
    c = c_ref[...]
    o_ref[0, 0] = _dot(_silu(c).astype(_BF16), w_ref[0].astype(_BF16)) + b_ref[0]


def _modulation(cond, w_mod, b_mod):
    n_tile = D_MODEL
    return pl.pallas_call(
        _mod_kernel,
        grid=(DEPTH, 3 * D_MODEL // n_tile),
        in_specs=[
            pl.BlockSpec((MOD_ROWS, D_MODEL), lambda l, n: (0, 0)),
            pl.BlockSpec((1, D_MODEL, n_tile), lambda l, n: (l, 0, n)),
            pl.BlockSpec((1, 1, n_tile), lambda l, n: (l, 0, n)),
        ],
        out_specs=pl.BlockSpec((1, 1, MOD_ROWS, n_tile), lambda l, n: (l, n, 0, 0)),
        out_shape=jax.ShapeDtypeStruct((DEPTH, 3, MOD_ROWS, D_MODEL), _F32),
        name="modulation",
    )(cond, w_mod, b_mod.reshape(DEPTH, 1, 3 * D_MODEL))


def _relayout_kernel(tiles_ref, *refs, transpose):
    del tiles_ref
    *w_refs, o_ref = refs
    for k, w_ref in enumerate(w_refs):
        w = w_ref[0]
        width = w.shape[1]
        if transpose:
            o_ref[0, k * width:(k + 1) * width, :] = w.T.astype(_BF16)
        else:
            o_ref[0, :, k * width:(k + 1) * width] = w.astype(_BF16)


def _relayout(w, groups, width, transpose, name):
    depth, rows, _ = w.shape
    n, g = len(groups), len(groups[0])
    assert all(len(grp) == g for grp in groups)
    if transpose:
        out_shape = (depth, n * g * width, rows)
        out_spec = pl.BlockSpec((1, g * width, rows), lambda l, i, t: (l, i, 0))
    else:
        out_shape = (depth, rows, n * g * width)
        out_spec = pl.BlockSpec((1, rows, g * width), lambda l, i, t: (l, 0, i))

    def tile_spec(k):
        return pl.BlockSpec((1, rows, width), lambda l, i, t: (l, 0, t[i * g + k]))

    return pl.pallas_call(
        functools.partial(_relayout_kernel, transpose=transpose),
        grid_spec=pltpu.PrefetchScalarGridSpec(
            num_scalar_prefetch=1,
            grid=(depth, n),
            in_specs=[tile_spec(k) for k in range(g)],
            out_specs=out_spec),
        out_shape=jax.ShapeDtypeStruct(out_shape, _BF16),
        name=name,
    )(jnp.asarray(groups, jnp.int32).reshape(-1), *([w] * g))


def _key_norm_kernel(ka_ref, kb_ref, o_ref):
    def max_norm2(k, dim):
        k2 = k * k
        per_head = [jnp.sum(k2[r:r + dim], axis=0, keepdims=True) for r in range(0, k.shape[0], dim)]
        return jnp.max(functools.reduce(jnp.maximum, per_head), axis=1, keepdims=True)

    o_ref[0, 0, 0:1, :] = jnp.broadcast_to(max_norm2(ka_ref[0, 0, 0], HEAD_DIM), (1, LANES))
    o_ref[0, 0, 1:2, :] = jnp.broadcast_to(max_norm2(kb_ref[0, 0, 0], DIFF_DIM), (1, LANES))
    o_ref[0, 0, 2:, :] = jnp.zeros((SEQ_ROWS - 2, LANES), _F32)


def _cached_key_norms(ctx_a, ctx_b):
    n_batch, _, _, _, past = ctx_a.shape
    n2 = pl.pallas_call(
        _key_norm_kernel,
        grid=(n_batch, DEPTH),
        in_specs=[pl.BlockSpec((1, 1, 1, KV_WIDTH_A, past), lambda b, l: (b, l, 0, 0, 0)),
                  pl.BlockSpec((1, 1, 1, WIDTH_B, past), lambda b, l: (b, l, 0, 0, 0))],
        out_specs=pl.BlockSpec((1, 1, SEQ_ROWS, LANES), lambda b, l: (b, l, 0, 0)),
        out_shape=jax.ShapeDtypeStruct((n_batch, DEPTH, SEQ_ROWS, LANES), _F32),
        name="cached_key_norms",
    )(ctx_a, ctx_b)
    return jnp.sqrt(jnp.max(n2[:, :, 0, 0], axis=0)), jnp.sqrt(jnp.max(n2[:, :, 1, 0], axis=0))


def _softmax_t(s_t, shift=None):
    if shift is None:
        shift = jnp.max(s_t, axis=0, keepdims=True)
    return jnp.exp2(s_t - shift).astype(_BF16)


def _rms_rows(x):
    return lax.rsqrt(jnp.mean(x * x, axis=0, keepdims=True) + RMS_EPS)


def _swap_halves(x, m):
    return jnp.concatenate([x[m:2 * m], x[0:m], x[3 * m:4 * m], x[2 * m:3 * m]], axis=0)


def _pad_rows(x, start, total):
    parts = []
    if start:
        parts.append(jnp.zeros((start, x.shape[1]), x.dtype))
    parts.append(x)
    rest = total - start - x.shape[0]
    if rest:
        parts.append(jnp.zeros((rest, x.shape[1]), x.dtype))
    return jnp.concatenate(parts, axis=0)


def _cat(parts, axis):
    return parts[0] if len(parts) == 1 else jnp.concatenate(parts, axis=axis)


def _mixer_kernel(*refs, latent, seq, nb, past, exact):
    grp = nb * seq
    n_blk = grp // TOKEN_BLOCK
    tb = TOKEN_BLOCK // nb
    it = iter(refs)
    x_ref, mod_ref, ng_ref = next(it), next(it), next(it)
    wft_ref, wt_ref, wct_ref, wout_ref = next(it), next(it), next(it), next(it)
    prow_ref, pcol_ref = next(it), next(it)
    bd64_ref, bd32_ref, bc_ref, bs_ref, cs_ref, nss_ref = (next(it) for _ in range(6))
    if latent:
        ctxa_ref, ctxb_ref = next(it), next(it)
        cqa_ref, sqa_ref, cqb_ref, sqb_ref = (next(it) for _ in range(4))
        cka_ref, skpa_ref, skma_ref, ckb_ref, skpb_ref, skmb_ref = (next(it) for _ in range(6))
        y_ref = next(it)
    else:
        y_ref, newa_ref, newb_ref = next(it), next(it), next(it)
    xres_s, h_s, ka_s, kb_s, vt_s, t1_s, t2_s, mix_s, kn2a_s, kn2b_s = (next(it) for _ in range(10))

    l = pl.program_id(1)
    j = pl.program_id(2)
    mod_row = pl.ds(pl.program_id(0) + 1 if latent else 0, 1)
    shift = mod_ref[0, 0, mod_row, :]
    scale = mod_ref[0, 1, mod_row, :]
    gate = mod_ref[0, 2, mod_row, :]
    prow, pcol = prow_ref[l], pcol_ref[l]
    gka, gkb = prow[P_GKA:P_GKA + 1, 0:KV_WIDTH_A], prow[P_GKB:P_GKB + 1, :]
    gqa, gqb, gsub = pcol[:, C_GQA:C_GQA + 1], pcol[0:DIFF_DIM, C_GQB:C_GQB + 1], pcol[:, C_GSUB:C_GSUB + 1]

    @pl.when((l == 0) & (j == 0))
    def _():
        xres_s[...] = x_ref[...].reshape(grp, D_MODEL)
        for g in range(N_V_GROUPS):
            vt_s[g * V_GROUP_ROWS + HEAD_DIM:(g + 1) * V_GROUP_ROWS, :] = jnp.ones(
                (ONES_ROWS, grp + past), _BF16)

    def rope_k(k, cos_ref, sp_ref, sm_ref, rows, m):
        out = []
        for c0 in range(0, k.shape[1], LANES):
            kc = k[:, c0:c0 + LANES]
            out.append(kc * cos_ref[rows, :] + pltpu.roll(kc, m, 1) * sp_ref[rows, :]
                       + pltpu.roll(kc, LANES - m, 1) * sm_ref[rows, :])
        return _cat(out, 1)

    def note_key_norms(i, ka, kb, first):
        for k, bd_ref, dim, kn2_s in ((ka, bd64_ref, HEAD_DIM, kn2a_s), (kb, bd32_ref, DIFF_DIM, kn2b_s)):
            n2 = jnp.max(_dot((k * k).astype(_BF16), bd_ref[...]), axis=0, keepdims=True) * float(dim)
            kn2_s[i:i + 1, :] = n2 if first else jnp.maximum(kn2_s[i:i + 1, :], n2)

    def put_values(vt, cols):
        for g in range(N_V_GROUPS):
            vt_s[g * V_GROUP_ROWS:g * V_GROUP_ROWS + HEAD_DIM, cols] = vt[g * HEAD_DIM:(g + 1) * HEAD_DIM]

    @pl.when(j == 0)
    def _prep():
        for c in range(n_blk):
            rows = slice(c * TOKEN_BLOCK, (c + 1) * TOKEN_BLOCK)
            xc = xres_s[rows, :]
            ms = jnp.mean(xc * xc, axis=-1, keepdims=True)
            hc = (xc * lax.rsqrt(ms + RMS_EPS) * ng_ref[l] * (1.0 + scale) + shift).astype(_BF16)
            h_s[rows, :] = hc
            pt = _dot(hc, wt_ref[l])
            ka = pt[:, T_KA:T_KA + KV_WIDTH_A]
            va = pt[:, T_VA:T_VA + KV_WIDTH_A]
            kb = pt[:, T_KB:T_KB + WIDTH_B]
            vb = pt[:, T_VB:T_VB + WIDTH_B]
            vat, vbt = va.T, vb.T
            ka = ka * lax.rsqrt(_dot((ka * ka).astype(_BF16), bd64_ref[...]) + RMS_EPS) * gka
            kb = kb * lax.rsqrt(_dot((kb * kb).astype(_BF16), bd32_ref[...]) + RMS_EPS) * gkb
            if latent:
                ka = rope_k(ka, cka_ref, skpa_ref, skma_ref, rows, HEAD_DIM // 4)
                kb = rope_k(kb, ckb_ref, skpb_ref, skmb_ref, rows, DIFF_DIM // 4)
            else:
                kat, kbt = ka.T, kb.T
                for i in range(nb):
                    r = slice(i * seq, (i + 1) * seq)
                    newa_ref[i, 0, 0], newa_ref[i, 0, 1] = kat[:, r], vat[:, r]
                    newb_ref[i, 0, 0], newb_ref[i, 0, 1] = kbt[:, r], vbt[:, r]
            ka_s[rows, :] = ka.astype(_BF16)
            kb_s[rows, :] = kb.astype(_BF16)
            if not exact:
                for i in range(nb):
                    r = slice(i * seq, (i + 1) * seq) if nb > 1 else slice(None)
                    note_key_norms(i, ka[r], kb[r], first=(c == 0))
            put_values(jnp.concatenate([vat, vbt], axis=0).astype(_BF16), rows)
            uct = _dot_nt(wft_ref[l, F_UC:F_ROWS, :], hc).astype(_BF16)
            t1_s[:, rows] = _dot(bc_ref[...], uct).astype(_BF16)
            t2_s[:, rows] = _dot(bs_ref[...], uct).astype(_BF16)
        if latent:
            ka_ctx, kb_ctx = ctxa_ref[0, 0, 0].T, ctxb_ref[0, 0, 0].T
            ka_s[grp:grp + past, :] = ka_ctx.astype(_BF16)
            kb_s[grp:grp + past, :] = kb_ctx.astype(_BF16)
            if not exact:
                note_key_norms(0, ka_ctx, kb_ctx, first=False)
            put_values(jnp.concatenate([ctxa_ref[0, 0, 1], ctxb_ref[0, 0, 1]], axis=0).astype(_BF16),
                       slice(grp, grp + past))

    blk_rows = pl.ds(pl.multiple_of(j * TOKEN_BLOCK, TOKEN_BLOCK), TOKEN_BLOCK)

    pf_chunks = {}

    def pf_chunk(c):
        if c not in pf_chunks:
            pf_chunks[c] = _dot_nt(wft_ref[l, c * PF_CHUNK:(c + 1) * PF_CHUNK, :], h_s[blk_rows, :])
        return pf_chunks[c]

    def pf_rows(r0, n, cols):
        c, off = divmod(r0, PF_CHUNK)
        assert off + n <= PF_CHUNK
        return pf_chunk(c)[off:off + n, cols]

    lv = prow[P_LAM:P_LAM + 1, :]
    lq1, lk1, lq2, lk2 = (lv[:, i * DIFF_DIM:(i + 1) * DIFF_DIM] for i in range(4))
    lam_init = jnp.where(l == 0, LAM_INIT[0], LAM_INIT[1])
    lam = (jnp.exp(jnp.sum(lq1 * lk1, axis=-1, keepdims=True))
           - jnp.exp(jnp.sum(lq2 * lk2, axis=-1, keepdims=True)) + lam_init)

    def q_tile(r0, dim, gain, cos_ref, sin_ref, cols, pad_start, pad_total, kn2_s):
        q = pf_rows(r0, dim, cols)
        q = q * _rms_rows(q) * gain
        if latent:
            q = q * cos_ref[...] + _swap_halves(q, dim // 4) * sin_ref[...]
        bound = None
        if not exact:
            i = cols.start // tb if nb > 1 else 0
            kn2 = kn2_s[i:i + 1, pad_start:pad_start + 1]
            bound = jnp.sqrt(jnp.sum(q * q, axis=0, keepdims=True) * kn2) * BOUND_SLACK
        return _pad_rows(q.astype(_BF16), pad_start, pad_total), bound

    def q_a(h, cols):
        g = h // (N_HEADS_A // N_KV_A)
        return q_tile(F_QA + h * HEAD_DIM, HEAD_DIM, gqa, cqa_ref if latent else None,
                      sqa_ref if latent else None, cols, g * HEAD_DIM, KV_WIDTH_A, kn2a_s)

    def q_b(h, comp, cols):
        r0 = h * HEAD_DIM + comp * DIFF_DIM
        return q_tile(F_QB + r0, DIFF_DIM, gqb, cqb_ref if latent else None,
                      sqb_ref if latent else None, cols, r0, WIDTH_B, kn2b_s)

    def put_mix(gated_t, cols, c0):
        mix_s[cols, c0:c0 + gated_t.shape[0]] = gated_t.T.astype(_BF16)

    half = {}

    def put_head(slot, cols, gated_t):
        if slot % 2 == 0:
            half[cols.start] = gated_t
        else:
            put_mix(jnp.concatenate([half.pop(cols.start), gated_t], axis=0), cols, (slot - 1) * HEAD_DIM)

    def store_a(h, cols, o):
        ga = pf_rows(F_GA + h * HEAD_DIM, HEAD_DIM, cols)
        put_head(h, cols, o * _silu(ga))

    def store_b(h, cols, o1, o2):
        ob = o1 - lam * o2
        ob = ob * _rms_rows(ob) * gsub
        gb = pf_rows(F_GB + h * HEAD_DIM, HEAD_DIM, cols)
        put_head(N_HEADS_A + h, cols, ob * _silu(gb))

    def v_group(g):
        return slice(g * V_GROUP_ROWS, (g + 1) * V_GROUP_ROWS)

    units = []
    all_cols = slice(0, TOKEN_BLOCK)
    if nb == 1:
        keys = slice(0, grp + past)
        for h in range(N_HEADS_A):
            g = h // (N_HEADS_A // N_KV_A)
            units.append(("a", keys, v_group(g),
                          [functools.partial(q_a, h, all_cols)],
                          functools.partial(store_a, h, all_cols)))
        for h in range(N_HEADS_B):
            v_rows = v_group(N_KV_A + h)
            held = {}
            for comp in range(2):
                def fin(o, h=h, comp=comp, held=held):
                    held[comp] = o
                    if comp == 1:
                        store_b(h, all_cols, held[0], held[1])
                units.append(("b", keys, v_rows, [functools.partial(q_b, h, comp, all_cols)], fin))
    else:
        seqs = [(slice(i * seq, (i + 1) * seq), slice(i * tb, (i + 1) * tb)) for i in range(nb)]
        for u in range(N_HEADS_A // 2):
            g = (2 * u) // (N_HEADS_A // N_KV_A)
            for keys, cols in seqs:
                def fin(o, u=u, cols=cols):
                    store_a(2 * u, cols, o[:, 0:tb])
                    store_a(2 * u + 1, cols, o[:, tb:2 * tb])
                units.append(("a", keys, v_group(g),
                              [functools.partial(q_a, 2 * u, cols), functools.partial(q_a, 2 * u + 1, cols)], fin))
        for h in range(N_HEADS_B):
            for keys, cols in seqs:
                def fin(o, h=h, cols=cols):
                    store_b(h, cols, o[:, 0:tb], o[:, tb:2 * tb])
                units.append(("b", keys, v_group(N_KV_A + h),
                              [functools.partial(q_b, h, 0, cols), functools.partial(q_b, h, 1, cols)], fin))

    def scores(unit):
        kind, keys, _, tiles, _ = unit
        k_ref = ka_s if kind == "a" else kb_s
        qs, shifts = zip(*[t() for t in tiles])
        shift = None if exact else _cat(shifts, 1)
        return _dot(k_ref[keys, :], _cat(qs, 1)), shift

    def attend(unit, e):
        o = _dot(vt_s[unit[2], unit[1]], e)
        unit[4](o[0:HEAD_DIM] / o[HEAD_DIM:HEAD_DIM + 1])

    def stream_units():
        chunks = [(u, c) for u, unit in enumerate(units)
                  for c in range(unit[1].start, unit[1].stop, KEY_CHUNK)]
        queries = {}

        def chunk_scores(u, c):
            kind, _, _, tiles, _ = units[u]
            if u not in queries:
                qs, shifts = zip(*[t() for t in tiles])
                queries[u] = (_cat(qs, 1), _cat(shifts, 1))
            k_ref = ka_s if kind == "a" else kb_s
            return _dot(k_ref[c:c + KEY_CHUNK, :], queries[u][0])

        acc = None
        ahead = chunk_scores(*chunks[0])
        for n, (u, c) in enumerate(chunks):
            cur = ahead
            if n + 1 < len(chunks):
                ahead = chunk_scores(*chunks[n + 1])
            first, last = c == units[u][1].start, c + KEY_CHUNK >= units[u][1].stop
            if first:
                for work in fillers.get(u, ()):
                    work()
            part = _dot(vt_s[units[u][2], c:c + KEY_CHUNK], _softmax_t(cur, queries[u][1]))
            acc = part if first else acc + part
            if last:
                units[u][4](acc[0:HEAD_DIM] / acc[HEAD_DIM:HEAD_DIM + 1])
                del queries[u]

    def mixer_c():
        if nb == 1:
            ft = _dot(t1_s[...], cs_ref[j]) + _dot(t2_s[...], nss_ref[j])
        else:
            ft = _cat([_dot(t1_s[:, i * seq:(i + 1) * seq], cs_ref[0])
                       + _dot(t2_s[:, i * seq:(i + 1) * seq], nss_ref[0]) for i in range(nb)], 1)
        oc = _dot(wct_ref[l], ft.astype(_BF16))
        put_mix(oc * _silu(pf_rows(F_GC, WIDTH_C, all_cols)), all_cols, WIDTH_A + WIDTH_B)

    out_parts = []

    def out_chunk(k):
        cols = slice(k * OUT_CHUNK, (k + 1) * OUT_CHUNK)
        out_parts.append(_dot(mix_s[:, cols], wout_ref[l, cols, :]))

    fillers = {
        0: [functools.partial(pf_chunk, F_GA // PF_CHUNK)],
        1: [functools.partial(pf_chunk, F_QA // PF_CHUNK + 1)],
        2: [functools.partial(pf_chunk, F_GA // PF_CHUNK + 1)],
        3: [functools.partial(pf_chunk, F_QB // PF_CHUNK)],
        4: [functools.partial(pf_chunk, F_GB // PF_CHUNK)],
        5: [mixer_c],
        6: [functools.partial(out_chunk, 3), functools.partial(out_chunk, 0)],
        9: [functools.partial(out_chunk, 1)],
    }
    if exact:
        ahead = scores(units[0])
        for u, unit in enumerate(units):
            cur = ahead
            if u + 1 < len(units):
                ahead = scores(units[u + 1])
            for work in fillers.get(u, ()):
                work()
            attend(unit, _softmax_t(*cur))
    else:
        stream_units()
    out_chunk(2)

    y = xres_s[blk_rows, :] + gate * functools.reduce(lambda a, b: a + b, out_parts)
    xres_s[blk_rows, :] = y

    @pl.when(l == DEPTH - 1)
    def _():
        y_ref[...] = y.reshape(y_ref.shape)


def _const_spec(shape):
    nd = len(shape)
    return pl.BlockSpec(shape, lambda b, l, j: (0,) * nd, pipeline_mode=pl.Buffered(1))


def _mixer(x, mod, wts, consts, nb, exact, ctx=None):
    latent = ctx is not None
    n_batch, seq, _ = x.shape
    grp = nb * seq
    n_blk = grp // TOKEN_BLOCK
    assert grp % TOKEN_BLOCK == 0 and (nb == 1 or grp == TOKEN_BLOCK) and n_batch % nb == 0 and nb <= SEQ_ROWS
    past = ctx[0].shape[-1] if latent else 0
    last = DEPTH - 1

    args = [x, mod]
    specs = [
        pl.BlockSpec((nb, seq, D_MODEL), lambda b, l, j: (b, 0, 0),
                     pipeline_mode=pl.Buffered(1) if latent else None),
        pl.BlockSpec((1, 3, MOD_ROWS, D_MODEL), lambda b, l, j: (l, 0, 0, 0)),
    ]
    for name in ("norm_g", "wft", "wt", "wct", "wout", "prow", "pcol"):
        args.append(wts[name])
        specs.append(_const_spec(wts[name].shape))
    for name in ("bd64", "bd32", "bc", "bs", "cs", "nss"):
        args.append(consts[name])
        specs.append(_const_spec(consts[name].shape))
    if latent:
        args += [ctx[0], ctx[1]]
        specs += [
            pl.BlockSpec((1, 1, 2, KV_WIDTH_A, past), lambda b, l, j: (b, l, 0, 0, 0), pipeline_mode=pl.Buffered(1)),
            pl.BlockSpec((1, 1, 2, WIDTH_B, past), lambda b, l, j: (b, l, 0, 0, 0), pipeline_mode=pl.Buffered(1)),
        ]
        for name, rows in (("cqa", HEAD_DIM), ("sqa", HEAD_DIM), ("cqb", DIFF_DIM), ("sqb", DIFF_DIM)):
            args.append(consts[name])
            specs.append(pl.BlockSpec((rows, TOKEN_BLOCK), lambda b, l, j: (0, j)))
        for name in ("cka", "skpa", "skma", "ckb", "skpb", "skmb"):
            args.append(consts[name])
            specs.append(_const_spec(consts[name].shape))

    y_shape = jax.ShapeDtypeStruct(x.shape, _F32)
    if nb == 1:
        y_spec = pl.BlockSpec((1, TOKEN_BLOCK, D_MODEL), lambda b, l, j: (b, jnp.where(l == last, j, 0), 0))
    else:
        y_spec = pl.BlockSpec((nb, seq, D_MODEL), lambda b, l, j: (b, 0, 0))
    if latent:
        out_shape, out_specs = y_shape, y_spec
    else:
        out_shape = (y_shape,
                     jax.ShapeDtypeStruct((n_batch, DEPTH, 2, KV_WIDTH_A, seq), _F32),
                     jax.ShapeDtypeStruct((n_batch, DEPTH, 2, WIDTH_B, seq), _F32))
        out_specs = (y_spec,
                     pl.BlockSpec((nb, 1, 2, KV_WIDTH_A, seq), lambda b, l, j: (b, l, 0, 0, 0)),
                     pl.BlockSpec((nb, 1, 2, WIDTH_B, seq), lambda b, l, j: (b, l, 0, 0, 0)))

    n_keys = grp + past
    scratch = [
        pltpu.VMEM((grp, D_MODEL), _F32),
        pltpu.VMEM((grp, D_MODEL), _BF16),
        pltpu.VMEM((n_keys, KV_WIDTH_A), _BF16),
        pltpu.VMEM((n_keys, WIDTH_B), _BF16),
        pltpu.VMEM((N_V_GROUPS * V_GROUP_ROWS, n_keys), _BF16),
        pltpu.VMEM((WIDTH_C, grp), _BF16),
        pltpu.VMEM((WIDTH_C, grp), _BF16),
        pltpu.VMEM((TOKEN_BLOCK, D_MODEL), _BF16),
        pltpu.VMEM((SEQ_ROWS, KV_WIDTH_A), _F32),
        pltpu.VMEM((SEQ_ROWS, WIDTH_B), _F32),
    ]
    kern = functools.partial(_mixer_kernel, latent=latent, seq=seq, nb=nb, past=past, exact=exact)
    return pl.pallas_call(
        kern,
        grid=(n_batch // nb, DEPTH, n_blk),
        in_specs=specs,
        out_specs=out_specs,
        out_shape=out_shape,
        scratch_shapes=scratch,
        compiler_params=pltpu.CompilerParams(
            dimension_semantics=("arbitrary", "arbitrary", "arbitrary"),
            vmem_limit_bytes=V7X_VMEM_LIMIT_BYTES),
        name=("latent_mixer" if latent else "context_mixer") + ("_exact" if exact else ""),
    )(*args)


def _rope_tables(n_tok, dim):
    m = dim // 4
    t = np.arange(n_tok)
    inv = 1.0 / (ROPE_BASE ** (np.arange(m, dtype=np.float64) / m))
    ar = (t // GRID_W)[:, None] * inv
    ac = (t % GRID_W)[:, None] * inv
    cos = np.concatenate([np.cos(ar), np.cos(ar), np.cos(ac), np.cos(ac)], axis=1)
    sin = np.concatenate([-np.sin(ar), np.sin(ar), -np.sin(ac), np.sin(ac)], axis=1)
    first = np.tile(np.concatenate([np.ones(m), np.zeros(m)]), 2)[None, :]
    return cos.astype(np.float32), sin.astype(np.float32), first.astype(np.float32)


def _block_diag(block, n):
    out = np.zeros((block.shape[0] * n, block.shape[1] * n), np.float64)
    for i in range(n):
        out[i * block.shape[0]:(i + 1) * block.shape[0], i * block.shape[1]:(i + 1) * block.shape[1]] = block
    return out


def _dft_cos_sin(n):
    k = np.arange(n)
    ang = 2.0 * np.pi * ((k[:, None] * k[None, :]) % n) / n
    return np.cos(ang), np.sin(ang)


def _mixer_consts(seq, latent):
    c64, s64 = _dft_cos_sin(GROUP_C)
    norm = 1.0 / math.sqrt(GROUP_C * seq)
    cs, ss = _dft_cos_sin(seq)
    f32c = lambda a: jnp.asarray(np.asarray(a, np.float32))
    width = min(seq, TOKEN_BLOCK)
    col_blocks = lambda a: a.reshape(seq, seq // width, width).transpose(1, 0, 2)
    consts = {
        "bd64": f32c(_block_diag(np.full((HEAD_DIM, HEAD_DIM), 1.0 / HEAD_DIM), N_KV_A)).astype(_BF16),
        "bd32": f32c(_block_diag(np.full((DIFF_DIM, DIFF_DIM), 1.0 / DIFF_DIM), 2 * N_HEADS_B)).astype(_BF16),
        "bc": f32c(_block_diag(c64, N_GROUPS_C) * norm).astype(_BF16),
        "bs": f32c(_block_diag(s64, N_GROUPS_C) * norm).astype(_BF16),
        "cs": f32c(col_blocks(cs)).astype(_BF16),
        "nss": f32c(col_blocks(-ss)).astype(_BF16),
    }
    if latent:
        cos_a, sin_a, first_a = _rope_tables(seq, HEAD_DIM)
        cos_b, sin_b, first_b = _rope_tables(seq, DIFF_DIM)
        rep_a, rep_b = LANES // HEAD_DIM, LANES // DIFF_DIM
        consts.update({
            "cqa": f32c(cos_a.T), "sqa": f32c(sin_a.T), "cqb": f32c(cos_b.T), "sqb": f32c(sin_b.T),
            "cka": f32c(np.tile(cos_a, (1, rep_a))),
            "skpa": f32c(np.tile(sin_a * (1.0 - first_a), (1, rep_a))),
            "skma": f32c(np.tile(sin_a * first_a, (1, rep_a))),
            "ckb": f32c(np.tile(cos_b, (1, rep_b))),
            "skpb": f32c(np.tile(sin_b * (1.0 - first_b), (1, rep_b))),
            "skmb": f32c(np.tile(sin_b * first_b, (1, rep_b))),
        })
    return consts


def _mixer_weights(norm_g, w_in, q_norm_a, k_norm_a, q_norm_b, k_norm_b,
                   lambda_q1, lambda_k1, lambda_q2, lambda_k2, subln_g, w_fourier, w_out):
    keep = jnp.asarray([1.0 - v for v in LAM_INIT], _F32)
    gqa = q_norm_a * (HEAD_DIM ** -0.5 * LOG2_E)
    gqb = q_norm_b * (DIFF_DIM ** -0.5 * LOG2_E)
    row = lambda v: jnp.pad(v, ((0, 0), (0, P_LANES - v.shape[1])))
    prow = jnp.stack([row(jnp.tile(k_norm_a, (1, N_KV_A))), row(jnp.tile(k_norm_b, (1, 2 * N_HEADS_B))),
                      row(jnp.concatenate([lambda_q1, lambda_k1, lambda_q2, lambda_k2], axis=1))]
                     + [jnp.zeros((DEPTH, P_LANES), _F32)] * (P_ROWS - 3), axis=1)
    col = lambda v: jnp.pad(v, ((0, 0), (0, HEAD_DIM - v.shape[1])))
    pcol = jnp.stack([col(gqa), col(gqb), col(subln_g * keep[:, None])]
                     + [jnp.zeros((DEPTH, HEAD_DIM), _F32)] * (C_COLS - 3), axis=2)
    return {
        "norm_g": norm_g.reshape(DEPTH, 1, D_MODEL),
        "wft": _relayout(w_in, FEATURE_TILES, WEIGHT_TILE, True, "w_in_feature_major"),
        "wt": _relayout(w_in, TOKEN_TILES, WEIGHT_TILE, False, "w_in_token_major"),
        "wct": _relayout(w_fourier, ((0,),), WIDTH_C, True, "w_fourier_t"),
        "wout": _relayout(w_out, ((0,),), D_MODEL, False, "w_out_bf16"),
        "prow": prow,
        "pcol": pcol,
    }


def kernel(x_prompt, x_sample, cache_attn_a, cache_attn_b, c, c_ctx, norm_g, w_mod, b_mod, w_in,
           q_norm_a, k_norm_a, q_norm_b, k_norm_b, lambda_q1, lambda_k1, lambda_q2, lambda_k2,
           subln_g, w_fourier, w_out):
    n_ctx_batch, ctx_seq, _ = x_prompt.shape
    n_lat_batch, lat_seq, _ = x_sample.shape
    past = cache_attn_a.shape[2]
    assert n_lat_batch + 1 <= MOD_ROWS and w_in.shape == (DEPTH, D_MODEL, D_IN)

    cond = jnp.concatenate(
        [c_ctx[None, :], c, jnp.zeros((MOD_ROWS - 1 - n_lat_batch, D_MODEL), _F32)], axis=0)
    mod = _modulation(cond, w_mod, b_mod)

    weights = _mixer_weights(norm_g, w_in, q_norm_a, k_norm_a, q_norm_b, k_norm_b,
                             lambda_q1, lambda_k1, lambda_q2, lambda_k2, subln_g, w_fourier, w_out)

    def feature_major(cache):
        n_heads, dim = cache.shape[-2:]
        return jnp.transpose(cache, (0, 1, 3, 4, 5, 2)).reshape(n_lat_batch, DEPTH, 2, n_heads * dim, past)

    def token_major(new, n_heads):
        seq = new.shape[-1]
        return jnp.transpose(new.reshape(n_ctx_batch, DEPTH, 2, n_heads, HEAD_DIM, seq), (0, 1, 5, 2, 3, 4))

    ctx_a, ctx_b = feature_major(cache_attn_a), feature_major(cache_attn_b)

    def max_norm(gain, dim):
        return math.sqrt(dim) * jnp.max(jnp.abs(gain), axis=-1)

    q_a = max_norm(q_norm_a * (HEAD_DIM ** -0.5 * LOG2_E), HEAD_DIM)
    q_b = max_norm(q_norm_b * (DIFF_DIM ** -0.5 * LOG2_E), DIFF_DIM)
    k_a, k_b = max_norm(k_norm_a, HEAD_DIM), max_norm(k_norm_b, DIFF_DIM)
    cached_a, cached_b = _cached_key_norms(ctx_a, ctx_b)
    ctx_bound = GUARD_SLACK * jnp.maximum(q_a * k_a, q_b * k_b)
    lat_bound = GUARD_SLACK * jnp.maximum(q_a * jnp.maximum(k_a, cached_a), q_b * jnp.maximum(k_b, cached_b))

    ctx_consts = _mixer_consts(ctx_seq, latent=False)
    lat_consts = _mixer_consts(lat_seq, latent=True)

    def context(exact):
        return _mixer(x_prompt, mod, weights, ctx_consts, nb=TOKEN_BLOCK // ctx_seq, exact=exact)

    def latent(exact):
        return _mixer(x_sample, mod, weights, lat_consts, nb=1, exact=exact, ctx=(ctx_a, ctx_b))

    y_prompt, new_a, new_b = lax.cond(jnp.all(ctx_bound <= MAX_SCORE_BOUND),
                                      functools.partial(context, False), functools.partial(context, True))
    y_sample = lax.cond(jnp.all(lat_bound <= MAX_SCORE_BOUND),
                        functools.partial(latent, False), functools.partial(latent, True))
    return (y_prompt, y_sample, token_major(new_a, N_KV_A), token_major(new_b, N_HEADS_B))
```

```python
import functools
import math

import jax
import jax.numpy as jnp
import numpy as np
from jax import lax
from jax.experimental import pallas as pl
from jax.experimental.pallas import tpu as pltpu

D_MODEL = 1024
DEPTH = 2
GRID_W = 64
HEAD_DIM = 64
N_HEADS_A = 8
N_KV_A = 2
N_HEADS_B = 4
DIFF_DIM = 32
GROUP_C = 64
N_GROUPS_C = 4
WIDTH_A = N_HEADS_A * HEAD_DIM
WIDTH_B = N_HEADS_B * HEAD_DIM
WIDTH_C = N_GROUPS_C * GROUP_C
KV_WIDTH_A = N_KV_A * HEAD_DIM
D_IN = 2 * WIDTH_A + 2 * KV_WIDTH_A + 4 * WIDTH_B + 2 * WIDTH_C
RMS_EPS = 1e-6
ROPE_BASE = 10000.0
LOG2_E = math.log2(math.e)
LAM_INIT = tuple(0.8 - 0.6 * math.exp(-0.3 * l) for l in range(DEPTH))

LANES = 128
TOKEN_BLOCK = 512
MOD_ROWS = 8
WEIGHT_TILE = 256
V7X_VMEM_LIMIT_BYTES = 58 * 1024 * 1024
BOUND_SLACK = 1.02
MAX_SCORE_BOUND = 40.0
GUARD_SLACK = 1.05
SEQ_ROWS = 8
P_ROWS, P_LANES = 8, 256
P_GKA, P_GKB, P_LAM = 0, 1, 2
C_COLS = 8
C_GQA, C_GQB, C_GSUB = 0, 1, 2

FEATURE_TILES = ((0, 1), (3, 4), (5, 8), (10, 9))
TOKEN_TILES = ((2, 6, 7),)
F_QA, F_GA, F_QB, F_GB, F_GC, F_UC = 0, 512, 1024, 1280, 1536, 1792
PF_CHUNK = 256
OUT_CHUNK = 256
F_ROWS = 2048
T_KA, T_VA, T_KB, T_VB = 0, 128, 256, 512
T_COLS = 768
ONES_ROWS = 16
V_GROUP_ROWS = HEAD_DIM + ONES_ROWS
N_V_GROUPS = N_KV_A + N_HEADS_B

_BF16 = jnp.bfloat16
_F32 = jnp.float32


def _silu(x):
    return x * (1.0 / (1.0 + jnp.exp(-x)))


def _dot(a, b):
    return jnp.dot(a, b, preferred_element_type=_F32)


def _dot_nt(a, b):
    return lax.dot_general(a, b, (((1,), (1,)), ((), ())), preferred_element_type=_F32)


def _mod_kernel(c_ref, w_ref, b_ref, o_ref):
    c = c_ref[...]
    o_ref[0, 0] = _dot(_silu(c).astype(_BF16), w_ref[0].astype(_BF16)) + b_ref[0]


def _modulation(cond, w_mod, b_mod):
    n_tile = D_MODEL
    return pl.pallas_call(
        _mod_kernel,
        grid=(DEPTH, 3 * D_MODEL // n_tile),
        in_specs=[
            pl.BlockSpec((MOD_ROWS, D_MODEL), lambda l, n: (0, 0)),
            pl.BlockSpec((1, D_MODEL, n_tile), lambda l, n: (l, 0, n)),
            pl.BlockSpec((1, 1, n_tile), lambda l, n: (l, 0, n)),
        ],
        out_specs=pl.BlockSpec((1, 1, MOD_ROWS, n_tile), lambda l, n: (l, n, 0, 0)),
        out_shape=jax.ShapeDtypeStruct((DEPTH, 3, MOD_ROWS, D_MODEL), _F32),
        name="modulation",
    )(cond, w_mod, b_mod.reshape(DEPTH, 1, 3 * D_MODEL))


def _relayout_kernel(tiles_ref, *refs, transpose):
    del tiles_ref
    *w_refs, o_ref = refs
    for k, w_ref in enumerate(w_refs):
        w = w_ref[0]
        width = w.shape[1]
        if transpose:
            o_ref[0, k * width:(k + 1) * width, :] = w.T.astype(_BF16)
        else:
            o_ref[0, :, k * width:(k + 1) * width] = w.astype(_BF16)


def _relayout(w, groups, width, transpose, name):
    depth, rows, _ = w.shape
    n, g = len(groups), len(groups[0])
    assert all(len(grp) == g for grp in groups)
    if transpose:
        out_shape = (depth, n * g * width, rows)
        out_spec = pl.BlockSpec((1, g * width, rows), lambda l, i, t: (l, i, 0))
    else:
        out_shape = (depth, rows, n * g * width)
        out_spec = pl.BlockSpec((1, rows, g * width), lambda l, i, t: (l, 0, i))

    def tile_spec(k):
        return pl.BlockSpec((1, rows, width), lambda l, i, t: (l, 0, t[i * g + k]))

    return pl.pallas_call(
        functools.partial(_relayout_kernel, transpose=transpose),
        grid_spec=pltpu.PrefetchScalarGridSpec(
            num_scalar_prefetch=1,
            grid=(depth, n),
            in_specs=[tile_spec(k) for k in range(g)],
            out_specs=out_spec),
        out_shape=jax.ShapeDtypeStruct(out_shape, _BF16),
        name=name,
    )(jnp.asarray(groups, jnp.int32).reshape(-1), *([w] * g))


def _key_norm_kernel(ka_ref, kb_ref, o_ref):
    def max_norm2(k, dim):
        k2 = k * k
        per_head = [jnp.sum(k2[r:r + dim], axis=0, keepdims=True) for r in range(0, k.shape[0], dim)]
        return jnp.max(functools.reduce(jnp.maximum, per_head), axis=1, keepdims=True)

    o_ref[0, 0, 0:1, :] = jnp.broadcast_to(max_norm2(ka_ref[0, 0, 0], HEAD_DIM), (1, LANES))
    o_ref[0, 0, 1:2, :] = jnp.broadcast_to(max_norm2(kb_ref[0, 0, 0], DIFF_DIM), (1, LANES))
    o_ref[0, 0, 2:, :] = jnp.zeros((SEQ_ROWS - 2, LANES), _F32)


def _cached_key_norms(ctx_a, ctx_b):
    n_batch, _, _, _, past = ctx_a.shape
    n2 = pl.pallas_call(
        _key_norm_kernel,
        grid=(n_batch, DEPTH),
        in_specs=[pl.BlockSpec((1, 1, 1, KV_WIDTH_A, past), lambda b, l: (b, l, 0, 0, 0)),
                  pl.BlockSpec((1, 1, 1, WIDTH_B, past), lambda b, l: (b, l, 0, 0, 0))],
        out_specs=pl.BlockSpec((1, 1, SEQ_ROWS, LANES), lambda b, l: (b, l, 0, 0)),
        out_shape=jax.ShapeDtypeStruct((n_batch, DEPTH, SEQ_ROWS, LANES), _F32),
        name="cached_key_norms",
    )(ctx_a, ctx_b)
    return jnp.sqrt(jnp.max(n2[:, :, 0, 0], axis=0)), jnp.sqrt(jnp.max(n2[:, :, 1, 0], axis=0))


def _softmax_t(s_t, shift=None):
    if shift is None:
        shift = jnp.max(s_t, axis=0, keepdims=True)
    return jnp.exp2(s_t - shift).astype(_BF16)


def _rms_rows(x):
    return lax.rsqrt(jnp.mean(x * x, axis=0, keepdims=True) + RMS_EPS)


def _swap_halves(x, m):
    return jnp.concatenate([x[m:2 * m], x[0:m], x[3 * m:4 * m], x[2 * m:3 * m]], axis=0)


def _pad_rows(x, start, total):
    parts = []
    if start:
        parts.append(jnp.zeros((start, x.shape[1]), x.dtype))
    parts.append(x)
    rest = total - start - x.shape[0]
    if rest:
        parts.append(jnp.zeros((rest, x.shape[1]), x.dtype))
    return jnp.concatenate(parts, axis=0)


def _cat(parts, axis):
    return parts[0] if len(parts) == 1 else jnp.concatenate(parts, axis=axis)


def _mixer_kernel(*refs, latent, seq, nb, past, exact):
    grp = nb * seq
    n_blk = grp // TOKEN_BLOCK
    tb = TOKEN_BLOCK // nb
    it = iter(refs)
    x_ref, mod_ref, ng_ref = next(it), next(it), next(it)
    wft_ref, wt_ref, wct_ref, wout_ref = next(it), next(it), next(it), next(it)
    prow_ref, pcol_ref = next(it), next(it)
    bd64_ref, bd32_ref, bc_ref, bs_ref, cs_ref, nss_ref = (next(it) for _ in range(6))
    if latent:
        ctxa_ref, ctxb_ref = next(it), next(it)
        cqa_ref, sqa_ref, cqb_ref, sqb_ref = (next(it) for _ in range(4))
        cka_ref, skpa_ref, skma_ref, ckb_ref, skpb_ref, skmb_ref = (next(it) for _ in range(6))
        y_ref = next(it)
    else:
        y_ref, newa_ref, newb_ref = next(it), next(it), next(it)
    xres_s, h_s, ka_s, kb_s, vt_s, t1_s, t2_s, mix_s, kn2a_s, kn2b_s = (next(it) for _ in range(10))

    l = pl.program_id(1)
    j = pl.program_id(2)
    mod_row = pl.ds(pl.program_id(0) + 1 if latent else 0, 1)
    shift = mod_ref[0, 0, mod_row, :]
    scale = mod_ref[0, 1, mod_row, :]
    gate = mod_ref[0, 2, mod_row, :]
    prow, pcol = prow_ref[l], pcol_ref[l]
    gka, gkb = prow[P_GKA:P_GKA + 1, 0:KV_WIDTH_A], prow[P_GKB:P_GKB + 1, :]
    gqa, gqb, gsub = pcol[:, C_GQA:C_GQA + 1], pcol[0:DIFF_DIM, C_GQB:C_GQB + 1], pcol[:, C_GSUB:C_GSUB + 1]

    @pl.when((l == 0) & (j == 0))
    def _():
        xres_s[...] = x_ref[...].reshape(grp, D_MODEL)
        for g in range(N_V_GROUPS):
            vt_s[g * V_GROUP_ROWS + HEAD_DIM:(g + 1) * V_GROUP_ROWS, :] = jnp.ones(
                (ONES_ROWS, grp + past), _BF16)

    def rope_k(k, cos_ref, sp_ref, sm_ref, rows, m):
        out = []
        for c0 in range(0, k.shape[1], LANES):
            kc = k[:, c0:c0 + LANES]
            out.append(kc * cos_ref[rows, :] + pltpu.roll(kc, m, 1) * sp_ref[rows, :]
                       + pltpu.roll(kc, LANES - m, 1) * sm_ref[rows, :])
        return _cat(out, 1)

    def note_key_norms(i, ka, kb, first):
        for k, bd_ref, dim, kn2_s in ((ka, bd64_ref, HEAD_DIM, kn2a_s), (kb, bd32_ref, DIFF_DIM, kn2b_s)):
            n2 = jnp.max(_dot((k * k).astype(_BF16), bd_ref[...]), axis=0, keepdims=True) * float(dim)
            kn2_s[i:i + 1, :] = n2 if first else jnp.maximum(kn2_s[i:i + 1, :], n2)

    def put_values(vt, cols):
        for g in range(N_V_GROUPS):
            vt_s[g * V_GROUP_ROWS:g * V_GROUP_ROWS + HEAD_DIM, cols] = vt[g * HEAD_DIM:(g + 1) * HEAD_DIM]

    @pl.when(j == 0)
    def _prep():
        for c in range(n_blk):
            rows = slice(c * TOKEN_BLOCK, (c + 1) * TOKEN_BLOCK)
            xc = xres_s[rows, :]
            ms = jnp.mean(xc * xc, axis=-1, keepdims=True)
            hc = (xc * lax.rsqrt(ms + RMS_EPS) * ng_ref[l] * (1.0 + scale) + shift).astype(_BF16)
            h_s[rows, :] = hc
            pt = _dot(hc, wt_ref[l])
            ka = pt[:, T_KA:T_KA + KV_WIDTH_A]
            va = pt[:, T_VA:T_VA + KV_WIDTH_A]
            kb = pt[:, T_KB:T_KB + WIDTH_B]
            vb = pt[:, T_VB:T_VB + WIDTH_B]
            vat, vbt = va.T, vb.T
            ka = ka * lax.rsqrt(_dot((ka * ka).astype(_BF16), bd64_ref[...]) + RMS_EPS) * gka
            kb = kb * lax.rsqrt(_dot((kb * kb).astype(_BF16), bd32_ref[...]) + RMS_EPS) * gkb
            if latent:
                ka = rope_k(ka, cka_ref, skpa_ref, skma_ref, rows, HEAD_DIM // 4)
                kb = rope_k(kb, ckb_ref, skpb_ref, skmb_ref, rows, DIFF_DIM // 4)
            else:
                kat, kbt = ka.T, kb.T
                for i in range(nb):
                    r = slice(i * seq, (i + 1) * seq)
                    newa_ref[i, 0, 0], newa_ref[i, 0, 1] = kat[:, r], vat[:, r]
                    newb_ref[i, 0, 0], newb_ref[i, 0, 1] = kbt[:, r], vbt[:, r]
            ka_s[rows, :] = ka.astype(_BF16)
            kb_s[rows, :] = kb.astype(_BF16)
            if not exact:
                for i in range(nb):
                    r = slice(i * seq, (i + 1) * seq) if nb > 1 else slice(None)
                    note_key_norms(i, ka[r], kb[r], first=(c == 0))
            put_values(jnp.concatenate([vat, vbt], axis=0).astype(_BF16), rows)
            uct = _dot_nt(wft_ref[l, F_UC:F_ROWS, :], hc).astype(_BF16)
            t1_s[:, rows] = _dot(bc_ref[...], uct).astype(_BF16)
            t2_s[:, rows] = _dot(bs_ref[...], uct).astype(_BF16)
        if latent:
            ka_ctx, kb_ctx = ctxa_ref[0, 0, 0].T, ctxb_ref[0, 0, 0].T
            ka_s[grp:grp + past, :] = ka_ctx.astype(_BF16)
            kb_s[grp:grp + past, :] = kb_ctx.astype(_BF16)
            if not exact:
                note_key_norms(0, ka_ctx, kb_ctx, first=False)
            put_values(jnp.concatenate([ctxa_ref[0, 0, 1], ctxb_ref[0, 0, 1]], axis=0).astype(_BF16),
                       slice(grp, grp + past))

    blk_rows = pl.ds(pl.multiple_of(j * TOKEN_BLOCK, TOKEN_BLOCK), TOKEN_BLOCK)

    pf_chunks = {}

    def pf_chunk(c):
        if c not in pf_chunks:
            pf_chunks[c] = _dot_nt(wft_ref[l, c * PF_CHUNK:(c + 1) * PF_CHUNK, :], h_s[blk_rows, :])
        return pf_chunks[c]

    def pf_rows(r0, n, cols):
        c, off = divmod(r0, PF_CHUNK)
        assert off + n <= PF_CHUNK
        return pf_chunk(c)[off:off + n, cols]

    lv = prow[P_LAM:P_LAM + 1, :]
    lq1, lk1, lq2, lk2 = (lv[:, i * DIFF_DIM:(i + 1) * DIFF_DIM] for i in range(4))
    lam_init = jnp.where(l == 0, LAM_INIT[0], LAM_INIT[1])
    lam = (jnp.exp(jnp.sum(lq1 * lk1, axis=-1, keepdims=True))
           - jnp.exp(jnp.sum(lq2 * lk2, axis=-1, keepdims=True)) + lam_init)

    def q_tile(r0, dim, gain, cos_ref, sin_ref, cols, pad_start, pad_total, kn2_s):
        q = pf_rows(r0, dim, cols)
        q = q * _rms_rows(q) * gain
        if latent:
            q = q * cos_ref[...] + _swap_halves(q, dim // 4) * sin_ref[...]
        bound = None
        if not exact:
            i = cols.start // tb if nb > 1 else 0
            kn2 = kn2_s[i:i + 1, pad_start:pad_start + 1]
            bound = jnp.sqrt(jnp.sum(q * q, axis=0, keepdims=True) * kn2) * BOUND_SLACK
        return _pad_rows(q.astype(_BF16), pad_start, pad_total), bound

    def q_a(h, cols):
        g = h // (N_HEADS_A // N_KV_A)
        return q_tile(F_QA + h * HEAD_DIM, HEAD_DIM, gqa, cqa_ref if latent else None,
                      sqa_ref if latent else None, cols, g * HEAD_DIM, KV_WIDTH_A, kn2a_s)

    def q_b(h, comp, cols):
        r0 = h * HEAD_DIM + comp * DIFF_DIM
        return q_tile(F_QB + r0, DIFF_DIM, gqb, cqb_ref if latent else None,
                      sqb_ref if latent else None, cols, r0, WIDTH_B, kn2b_s)

    def put_mix(gated_t, cols, c0):
        mix_s[cols, c0:c0 + gated_t.shape[0]] = gated_t.T.astype(_BF16)

    half = {}

    def put_head(slot, cols, gated_t):
        if slot % 2 == 0:
            half[cols.start] = gated_t
        else:
            put_mix(jnp.concatenate([half.pop(cols.start), gated_t], axis=0), cols, (slot - 1) * HEAD_DIM)

    def store_a(h, cols, o):
        ga = pf_rows(F_GA + h * HEAD_DIM, HEAD_DIM, cols)
        put_head(h, cols, o * _silu(ga))

    def store_b(h, cols, o1, o2):
        ob = o1 - lam * o2
        ob = ob * _rms_rows(ob) * gsub
        gb = pf_rows(F_GB + h * HEAD_DIM, HEAD_DIM, cols)
        put_head(N_HEADS_A + h, cols, ob * _silu(gb))

    def v_group(g):
        return slice(g * V_GROUP_ROWS, (g + 1) * V_GROUP_ROWS)

    units = []
    all_cols = slice(0, TOKEN_BLOCK)
    if nb == 1:
        keys = slice(0, grp + past)
        for h in range(N_HEADS_A):
            g = h // (N_HEADS_A // N_KV_A)
            units.append(("a", keys, v_group(g),
                          [functools.partial(q_a, h, all_cols)],
                          functools.partial(store_a, h, all_cols)))
        for h in range(N_HEADS_B):
            v_rows = v_group(N_KV_A + h)
            held = {}
            for comp in range(2):
                def fin(o, h=h, comp=comp, held=held):
                    held[comp] = o
                    if comp == 1:
                        store_b(h, all_cols, held[0], held[1])
                units.append(("b", keys, v_rows, [functools.partial(q_b, h, comp, all_cols)], fin))
    else:
        seqs = [(slice(i * seq, (i + 1) * seq), slice(i * tb, (i + 1) * tb)) for i in range(nb)]
        for u in range(N_HEADS_A // 2):
            g = (2 * u) // (N_HEADS_A // N_KV_A)
            for keys, cols in seqs:
                def fin(o, u=u, cols=cols):
                    store_a(2 * u, cols, o[:, 0:tb])
                    store_a(2 * u + 1, cols, o[:, tb:2 * tb])
                units.append(("a", keys, v_group(g),
                              [functools.partial(q_a, 2 * u, cols), functools.partial(q_a, 2 * u + 1, cols)], fin))
        for h in range(N_HEADS_B):
            for keys, cols in seqs:
                def fin(o, h=h, cols=cols):
                    store_b(h, cols, o[:, 0:tb], o[:, tb:2 * tb])
                units.append(("b", keys, v_group(N_KV_A + h),
                              [functools.partial(q_b, h, 0, cols), functools.partial(q_b, h, 1, cols)], fin))

    def scores(unit):
        kind, keys, _, tiles, _ = unit
        k_ref = ka_s if kind == "a" else kb_s
        qs, shifts = zip(*[t() for t in tiles])
        shift = None if exact else _cat(shifts, 1)
        return _dot(k_ref[keys, :], _cat(qs, 1)), shift

    def attend(unit, s_t, shift):
        o = _dot(vt_s[unit[2], unit[1]], _softmax_t(s_t, shift))
        unit[4](o[0:HEAD_DIM] / o[HEAD_DIM:HEAD_DIM + 1])

    def mixer_c():
        if nb == 1:
            ft = _dot(t1_s[...], cs_ref[j]) + _dot(t2_s[...], nss_ref[j])
        else:
            ft = _cat([_dot(t1_s[:, i * seq:(i + 1) * seq], cs_ref[0])
                       + _dot(t2_s[:, i * seq:(i + 1) * seq], nss_ref[0]) for i in range(nb)], 1)
        oc = _dot(wct_ref[l], ft.astype(_BF16))
        put_mix(oc * _silu(pf_rows(F_GC, WIDTH_C, all_cols)), all_cols, WIDTH_A + WIDTH_B)

    out_parts = []

    def out_chunk(k):
        cols = slice(k * OUT_CHUNK, (k + 1) * OUT_CHUNK)
        out_parts.append(_dot(mix_s[:, cols], wout_ref[l, cols, :]))

    fillers = {
        0: [functools.partial(pf_chunk, F_GA // PF_CHUNK)],
        1: [functools.partial(pf_chunk, F_QA // PF_CHUNK + 1)],
        2: [functools.partial(pf_chunk, F_GA // PF_CHUNK + 1)],
        3: [functools.partial(pf_chunk, F_QB // PF_CHUNK)],
        4: [functools.partial(pf_chunk, F_GB // PF_CHUNK)],
        5: [mixer_c],
        6: [functools.partial(out_chunk, 3), functools.partial(out_chunk, 0)],
        9: [functools.partial(out_chunk, 1)],
    }
    s_next = scores(units[0])
    for u, unit in enumerate(units):
        s_cur, shift = s_next
        if u + 1 < len(units):
            s_next = scores(units[u + 1])
        for work in fillers.get(u, ()):
            work()
        attend(unit, s_cur, shift)
    out_chunk(2)

    y = xres_s[blk_rows, :] + gate * functools.reduce(lambda a, b: a + b, out_parts)
    xres_s[blk_rows, :] = y

    @pl.when(l == DEPTH - 1)
    def _():
        y_ref[...] = y.reshape(y_ref.shape)


def _const_spec(shape):
    nd = len(shape)
    return pl.BlockSpec(shape, lambda b, l, j: (0,) * nd, pipeline_mode=pl.Buffered(1))


def _mixer(x, mod, wts, consts, nb, exact, ctx=None):
    latent = ctx is not None
    n_batch, seq, _ = x.shape
    grp = nb * seq
    n_blk = grp // TOKEN_BLOCK
    assert grp % TOKEN_BLOCK == 0 and (nb == 1 or grp == TOKEN_BLOCK) and n_batch % nb == 0 and nb <= SEQ_ROWS
    past = ctx[0].shape[-1] if latent else 0
    last = DEPTH - 1

    args = [x, mod]
    specs = [
        pl.BlockSpec((nb, seq, D_MODEL), lambda b, l, j: (b, 0, 0),
                     pipeline_mode=pl.Buffered(1) if latent else None),
        pl.BlockSpec((1, 3, MOD_ROWS, D_MODEL), lambda b, l, j: (l, 0, 0, 0)),
    ]
    for name in ("norm_g", "wft", "wt", "wct", "wout", "prow", "pcol"):
        args.append(wts[name])
        specs.append(_const_spec(wts[name].shape))
    for name in ("bd64", "bd32", "bc", "bs", "cs", "nss"):
        args.append(consts[name])
        specs.append(_const_spec(consts[name].shape))
    if latent:
        args += [ctx[0], ctx[1]]
        specs += [
            pl.BlockSpec((1, 1, 2, KV_WIDTH_A, past), lambda b, l, j: (b, l, 0, 0, 0), pipeline_mode=pl.Buffered(1)),
            pl.BlockSpec((1, 1, 2, WIDTH_B, past), lambda b, l, j: (b, l, 0, 0, 0), pipeline_mode=pl.Buffered(1)),
        ]
        for name, rows in (("cqa", HEAD_DIM), ("sqa", HEAD_DIM), ("cqb", DIFF_DIM), ("sqb", DIFF_DIM)):
            args.append(consts[name])
            specs.append(pl.BlockSpec((rows, TOKEN_BLOCK), lambda b, l, j: (0, j)))
        for name in ("cka", "skpa", "skma", "ckb", "skpb", "skmb"):
            args.append(consts[name])
            specs.append(_const_spec(consts[name].shape))

    y_shape = jax.ShapeDtypeStruct(x.shape, _F32)
    if nb == 1:
        y_spec = pl.BlockSpec((1, TOKEN_BLOCK, D_MODEL), lambda b, l, j: (b, jnp.where(l == last, j, 0), 0))
    else:
        y_spec = pl.BlockSpec((nb, seq, D_MODEL), lambda b, l, j: (b, 0, 0))
    if latent:
        out_shape, out_specs = y_shape, y_spec
    else:
        out_shape = (y_shape,
                     jax.ShapeDtypeStruct((n_batch, DEPTH, 2, KV_WIDTH_A, seq), _F32),
                     jax.ShapeDtypeStruct((n_batch, DEPTH, 2, WIDTH_B, seq), _F32))
        out_specs = (y_spec,
                     pl.BlockSpec((nb, 1, 2, KV_WIDTH_A, seq), lambda b, l, j: (b, l, 0, 0, 0)),
                     pl.BlockSpec((nb, 1, 2, WIDTH_B, seq), lambda b, l, j: (b, l, 0, 0, 0)))

    n_keys = grp + past
    scratch = [
        pltpu.VMEM((grp, D_MODEL), _F32),
        pltpu.VMEM((grp, D_MODEL), _BF16),
        pltpu.VMEM((n_keys, KV_WIDTH_A), _BF16),
        pltpu.VMEM((n_keys, WIDTH_B), _BF16),
        pltpu.VMEM((N_V_GROUPS * V_GROUP_ROWS, n_keys), _BF16),
        pltpu.VMEM((WIDTH_C, grp), _BF16),
        pltpu.VMEM((WIDTH_C, grp), _BF16),
        pltpu.VMEM((TOKEN_BLOCK, D_MODEL), _BF16),
        pltpu.VMEM((SEQ_ROWS, KV_WIDTH_A), _F32),
        pltpu.VMEM((SEQ_ROWS, WIDTH_B), _F32),
    ]
    kern = functools.partial(_mixer_kernel, latent=latent, seq=seq, nb=nb, past=past, exact=exact)
    return pl.pallas_call(
        kern,
        grid=(n_batch // nb, DEPTH, n_blk),
        in_specs=specs,
        out_specs=out_specs,
        out_shape=out_shape,
        scratch_shapes=scratch,
        compiler_params=pltpu.CompilerParams(
            dimension_semantics=("arbitrary", "arbitrary", "arbitrary"),
            vmem_limit_bytes=V7X_VMEM_LIMIT_BYTES),
        name=("latent_mixer" if latent else "context_mixer") + ("_exact" if exact else ""),
    )(*args)


def _rope_tables(n_tok, dim):
    m = dim // 4
    t = np.arange(n_tok)
    inv = 1.0 / (ROPE_BASE ** (np.arange(m, dtype=np.float64) / m))
    ar = (t // GRID_W)[:, None] * inv
    ac = (t % GRID_W)[:, None] * inv
    cos = np.concatenate([np.cos(ar), np.cos(ar), np.cos(ac), np.cos(ac)], axis=1)
    sin = np.concatenate([-np.sin(ar), np.sin(ar), -np.sin(ac), np.sin(ac)], axis=1)
    first = np.tile(np.concatenate([np.ones(m), np.zeros(m)]), 2)[None, :]
    return cos.astype(np.float32), sin.astype(np.float32), first.astype(np.float32)


def _block_diag(block, n):
    out = np.zeros((block.shape[0] * n, block.shape[1] * n), np.float64)
    for i in range(n):
        out[i * block.shape[0]:(i + 1) * block.shape[0], i * block.shape[1]:(i + 1) * block.shape[1]] = block
    return out


def _dft_cos_sin(n):
    k = np.arange(n)
    ang = 2.0 * np.pi * ((k[:, None] * k[None, :]) % n) / n
    return np.cos(ang), np.sin(ang)


def _mixer_consts(seq, latent):
    c64, s64 = _dft_cos_sin(GROUP_C)
    norm = 1.0 / math.sqrt(GROUP_C * seq)
    cs, ss = _dft_cos_sin(seq)
    f32c = lambda a: jnp.asarray(np.asarray(a, np.float32))
    width = min(seq, TOKEN_BLOCK)
    col_blocks = lambda a: a.reshape(seq, seq // width, width).transpose(1, 0, 2)
    consts = {
        "bd64": f32c(_block_diag(np.full((HEAD_DIM, HEAD_DIM), 1.0 / HEAD_DIM), N_KV_A)).astype(_BF16),
        "bd32": f32c(_block_diag(np.full((DIFF_DIM, DIFF_DIM), 1.0 / DIFF_DIM), 2 * N_HEADS_B)).astype(_BF16),
        "bc": f32c(_block_diag(c64, N_GROUPS_C) * norm).astype(_BF16),
        "bs": f32c(_block_diag(s64, N_GROUPS_C) * norm).astype(_BF16),
        "cs": f32c(col_blocks(cs)).astype(_BF16),
        "nss": f32c(col_blocks(-ss)).astype(_BF16),
    }
    if latent:
        cos_a, sin_a, first_a = _rope_tables(seq, HEAD_DIM)
        cos_b, sin_b, first_b = _rope_tables(seq, DIFF_DIM)
        rep_a, rep_b = LANES // HEAD_DIM, LANES // DIFF_DIM
        consts.update({
            "cqa": f32c(cos_a.T), "sqa": f32c(sin_a.T), "cqb": f32c(cos_b.T), "sqb": f32c(sin_b.T),
            "cka": f32c(np.tile(cos_a, (1, rep_a))),
            "skpa": f32c(np.tile(sin_a * (1.0 - first_a), (1, rep_a))),
            "skma": f32c(np.tile(sin_a * first_a, (1, rep_a))),
            "ckb": f32c(np.tile(cos_b, (1, rep_b))),
            "skpb": f32c(np.tile(sin_b * (1.0 - first_b), (1, rep_b))),
            "skmb": f32c(np.tile(sin_b * first_b, (1, rep_b))),
        })
    return consts


def _mixer_weights(norm_g, w_in, q_norm_a, k_norm_a, q_norm_b, k_norm_b,
                   lambda_q1, lambda_k1, lambda_q2, lambda_k2, subln_g, w_fourier, w_out):
    keep = jnp.asarray([1.0 - v for v in LAM_INIT], _F32)
    gqa = q_norm_a * (HEAD_DIM ** -0.5 * LOG2_E)
    gqb = q_norm_b * (DIFF_DIM ** -0.5 * LOG2_E)
    row = lambda v: jnp.pad(v, ((0, 0), (0, P_LANES - v.shape[1])))
    prow = jnp.stack([row(jnp.tile(k_norm_a, (1, N_KV_A))), row(jnp.tile(k_norm_b, (1, 2 * N_HEADS_B))),
                      row(jnp.concatenate([lambda_q1, lambda_k1, lambda_q2, lambda_k2], axis=1))]
                     + [jnp.zeros((DEPTH, P_LANES), _F32)] * (P_ROWS - 3), axis=1)
    col = lambda v: jnp.pad(v, ((0, 0), (0, HEAD_DIM - v.shape[1])))
    pcol = jnp.stack([col(gqa), col(gqb), col(subln_g * keep[:, None])]
                     + [jnp.zeros((DEPTH, HEAD_DIM), _F32)] * (C_COLS - 3), axis=2)
    return {
        "norm_g": norm_g.reshape(DEPTH, 1, D_MODEL),
        "wft": _relayout(w_in, FEATURE_TILES, WEIGHT_TILE, True, "w_in_feature_major"),
        "wt": _relayout(w_in, TOKEN_TILES, WEIGHT_TILE, False, "w_in_token_major"),
        "wct": _relayout(w_fourier, ((0,),), WIDTH_C, True, "w_fourier_t"),
        "wout": _relayout(w_out, ((0,),), D_MODEL, False, "w_out_bf16"),
        "prow": prow,
        "pcol": pcol,
    }


def kernel(x_prompt, x_sample, cache_attn_a, cache_attn_b, c, c_ctx, norm_g, w_mod, b_mod, w_in,
           q_norm_a, k_norm_a, q_norm_b, k_norm_b, lambda_q1, lambda_k1, lambda_q2, lambda_k2,
           subln_g, w_fourier, w_out):
    n_ctx_batch, ctx_seq, _ = x_prompt.shape
    n_lat_batch, lat_seq, _ = x_sample.shape
    past = cache_attn_a.shape[2]
    assert n_lat_batch + 1 <= MOD_ROWS and w_in.shape == (DEPTH, D_MODEL, D_IN)

    cond = jnp.concatenate(
        [c_ctx[None, :], c, jnp.zeros((MOD_ROWS - 1 - n_lat_batch, D_MODEL), _F32)], axis=0)
    mod = _modulation(cond, w_mod, b_mod)

    weights = _mixer_weights(norm_g, w_in, q_norm_a, k_norm_a, q_norm_b, k_norm_b,
                             lambda_q1, lambda_k1, lambda_q2, lambda_k2, subln_g, w_fourier, w_out)

    def feature_major(cache):
        n_heads, dim = cache.shape[-2:]
        return jnp.transpose(cache, (0, 1, 3, 4, 5, 2)).reshape(n_lat_batch, DEPTH, 2, n_heads * dim, past)

    def token_major(new, n_heads):
        seq = new.shape[-1]
        return jnp.transpose(new.reshape(n_ctx_batch, DEPTH, 2, n_heads, HEAD_DIM, seq), (0, 1, 5, 2, 3, 4))

    ctx_a, ctx_b = feature_major(cache_attn_a), feature_major(cache_attn_b)

    def max_norm(gain, dim):
        return math.sqrt(dim) * jnp.max(jnp.abs(gain), axis=-1)

    q_a = max_norm(q_norm_a * (HEAD_DIM ** -0.5 * LOG2_E), HEAD_DIM)
    q_b = max_norm(q_norm_b * (DIFF_DIM ** -0.5 * LOG2_E), DIFF_DIM)
    k_a, k_b = max_norm(k_norm_a, HEAD_DIM), max_norm(k_norm_b, DIFF_DIM)
    cached_a, cached_b = _cached_key_norms(ctx_a, ctx_b)
    ctx_bound = GUARD_SLACK * jnp.maximum(q_a * k_a, q_b * k_b)
    lat_bound = GUARD_SLACK * jnp.maximum(q_a * jnp.maximum(k_a, cached_a), q_b * jnp.maximum(k_b, cached_b))

    ctx_consts = _mixer_consts(ctx_seq, latent=False)
    lat_consts = _mixer_consts(lat_seq, latent=True)

    def context(exact):
        return _mixer(x_prompt, mod, weights, ctx_consts, nb=TOKEN_BLOCK // ctx_seq, exact=exact)

    def latent(exact):
        return _mixer(x_sample, mod, weights, lat_consts, nb=1, exact=exact, ctx=(ctx_a, ctx_b))

    def both(exact):
        return context(exact), latent(exact)

    bounded = jnp.all(ctx_bound <= MAX_SCORE_BOUND) & jnp.all(lat_bound <= MAX_SCORE_BOUND)
    (y_prompt, new_a, new_b), y_sample = lax.cond(bounded, functools.partial(both, False),
                                                  functools.partial(both, True))
    return (y_prompt, y_sample, token_major(new_a, N_KV_A), token_major(new_b, N_HEADS_B))
```

```python
import functools
import math

import jax
import jax.numpy as jnp
import numpy as np
from jax import lax
from jax.experimental import pallas as pl
from jax.experimental.pallas import tpu as pltpu

D_MODEL = 1024
DEPTH = 2
GRID_W = 64
HEAD_DIM = 64
N_HEADS_A = 8
N_KV_A = 2
N_HEADS_B = 4
DIFF_DIM = 32
GROUP_C = 64
N_GROUPS_C = 4
WIDTH_A = N_HEADS_A * HEAD_DIM
WIDTH_B = N_HEADS_B * HEAD_DIM
WIDTH_C = N_GROUPS_C * GROUP_C
KV_WIDTH_A = N_KV_A * HEAD_DIM
D_IN = 2 * WIDTH_A + 2 * KV_WIDTH_A + 4 * WIDTH_B + 2 * WIDTH_C
RMS_EPS = 1e-6
ROPE_BASE = 10000.0
LOG2_E = math.log2(math.e)
LAM_INIT = tuple(0.8 - 0.6 * math.exp(-0.3 * l) for l in range(DEPTH))

LANES = 128
TOKEN_BLOCK = 512
MOD_ROWS = 8
WEIGHT_TILE = 256
V7X_VMEM_LIMIT_BYTES = 58 * 1024 * 1024
BOUND_SLACK = 1.02
MAX_SCORE_BOUND = 40.0
GUARD_SLACK = 1.05
SEQ_ROWS = 8
P_ROWS, P_LANES = 8, 256
P_GKA, P_GKB, P_LAM = 0, 1, 2
C_COLS = 8
C_GQA, C_GQB, C_GSUB = 0, 1, 2

FEATURE_TILES = ((0, 1), (3, 4), (5, 8), (10, 9))
TOKEN_TILES = ((2, 6, 7),)
F_QA, F_GA, F_QB, F_GB, F_GC, F_UC = 0, 512, 1024, 1280, 1536, 1792
PF_CHUNK = 256
OUT_CHUNK = 256
F_ROWS = 2048
T_KA, T_VA, T_KB, T_VB = 0, 128, 256, 512
T_COLS = 768
ONES_ROWS = 16
V_GROUP_ROWS = HEAD_DIM + ONES_ROWS
N_V_GROUPS = N_KV_A + N_HEADS_B

_BF16 = jnp.bfloat16
_F32 = jnp.float32


def _silu(x):
    return x * (1.0 / (1.0 + jnp.exp(-x)))


def _dot(a, b):
    return jnp.dot(a, b, preferred_element_type=_F32)


def _dot_nt(a, b):
    return lax.dot_general(a, b, (((1,), (1,)), ((), ())), preferred_element_type=_F32)


def _mod_kernel(c_ref, w_ref, b_ref, o_ref):
    c = c_ref[...]
    o_ref[0, 0] = _dot(_silu(c).astype(_BF16), w_ref[0].astype(_BF16)) + b_ref[0]


def _modulation(cond, w_mod, b_mod):
    n_tile = D_MODEL
    return pl.pallas_call(
        _mod_kernel,
        grid=(DEPTH, 3 * D_MODEL // n_tile),
        in_specs=[
            pl.BlockSpec((MOD_ROWS, D_MODEL), lambda l, n: (0, 0)),
            pl.BlockSpec((1, D_MODEL, n_tile), lambda l, n: (l, 0, n)),
            pl.BlockSpec((1, 1, n_tile), lambda l, n: (l, 0, n)),
        ],
        out_specs=pl.BlockSpec((1, 1, MOD_ROWS, n_tile), lambda l, n: (l, n, 0, 0)),
        out_shape=jax.ShapeDtypeStruct((DEPTH, 3, MOD_ROWS, D_MODEL), _F32),
        name="modulation",
    )(cond, w_mod, b_mod.reshape(DEPTH, 1, 3 * D_MODEL))


def _relayout_kernel(tiles_ref, *refs, transpose):
    del tiles_ref
    *w_refs, o_ref = refs
    for k, w_ref in enumerate(w_refs):
        w = w_ref[0]
        width = w.shape[1]
        if transpose:
            o_ref[0, k * width:(k + 1) * width, :] = w.T.astype(_BF16)
        else:
            o_ref[0, :, k * width:(k + 1) * width] = w.astype(_BF16)


def _relayout(w, groups, width, transpose, name):
    depth, rows, _ = w.shape
    n, g = len(groups), len(groups[0])
    assert all(len(grp) == g for grp in groups)
    if transpose:
        out_shape = (depth, n * g * width, rows)
        out_spec = pl.BlockSpec((1, g * width, rows), lambda l, i, t: (l, i, 0))
    else:
        out_shape = (depth, rows, n * g * width)
        out_spec = pl.BlockSpec((1, rows, g * width), lambda l, i, t: (l, 0, i))

    def tile_spec(k):
        return pl.BlockSpec((1, rows, width), lambda l, i, t: (l, 0, t[i * g + k]))

    return pl.pallas_call(
        functools.partial(_relayout_kernel, transpose=transpose),
        grid_spec=pltpu.PrefetchScalarGridSpec(
            num_scalar_prefetch=1,
            grid=(depth, n),
            in_specs=[tile_spec(k) for k in range(g)],
            out_specs=out_spec),
        out_shape=jax.ShapeDtypeStruct(out_shape, _BF16),
        name=name,
    )(jnp.asarray(groups, jnp.int32).reshape(-1), *([w] * g))


def _key_norm_kernel(ka_ref, kb_ref, o_ref):
    def max_norm2(k, dim):
        k2 = k * k
        per_head = [jnp.sum(k2[r:r + dim], axis=0, keepdims=True) for r in range(0, k.shape[0], dim)]
        return jnp.max(functools.reduce(jnp.maximum, per_head), axis=1, keepdims=True)

    o_ref[0, 0, 0:1, :] = jnp.broadcast_to(max_norm2(ka_ref[0, 0, 0], HEAD_DIM), (1, LANES))
    o_ref[0, 0, 1:2, :] = jnp.broadcast_to(max_norm2(kb_ref[0, 0, 0], DIFF_DIM), (1, LANES))
    o_ref[0, 0, 2:, :] = jnp.zeros((SEQ_ROWS - 2, LANES), _F32)


def _cached_key_norms(ctx_a, ctx_b):
    n_batch, _, _, _, past = ctx_a.shape
    n2 = pl.pallas_call(
        _key_norm_kernel,
        grid=(n_batch, DEPTH),
        in_specs=[pl.BlockSpec((1, 1, 1, KV_WIDTH_A, past), lambda b, l: (b, l, 0, 0, 0)),
                  pl.BlockSpec((1, 1, 1, WIDTH_B, past), lambda b, l: (b, l, 0, 0, 0))],
        out_specs=pl.BlockSpec((1, 1, SEQ_ROWS, LANES), lambda b, l: (b, l, 0, 0)),
        out_shape=jax.ShapeDtypeStruct((n_batch, DEPTH, SEQ_ROWS, LANES), _F32),
        name="cached_key_norms",
    )(ctx_a, ctx_b)
    return jnp.sqrt(jnp.max(n2[:, :, 0, 0], axis=0)), jnp.sqrt(jnp.max(n2[:, :, 1, 0], axis=0))


def _softmax_t(s_t, shift=None):
    if shift is None:
        shift = jnp.max(s_t, axis=0, keepdims=True)
    return jnp.exp2(s_t - shift).astype(_BF16)


def _rms_rows(x):
    return lax.rsqrt(jnp.mean(x * x, axis=0, keepdims=True) + RMS_EPS)


def _swap_halves(x, m):
    return jnp.concatenate([x[m:2 * m], x[0:m], x[3 * m:4 * m], x[2 * m:3 * m]], axis=0)


def _pad_rows(x, start, total):
    parts = []
    if start:
        parts.append(jnp.zeros((start, x.shape[1]), x.dtype))
    parts.append(x)
    rest = total - start - x.shape[0]
    if rest:
        parts.append(jnp.zeros((rest, x.shape[1]), x.dtype))
    return jnp.concatenate(parts, axis=0)


def _cat(parts, axis):
    return parts[0] if len(parts) == 1 else jnp.concatenate(parts, axis=axis)


def _mixer_kernel(*refs, latent, seq, nb, past, exact):
    grp = nb * seq
    n_blk = grp // TOKEN_BLOCK
    tb = TOKEN_BLOCK // nb
    it = iter(refs)
    x_ref, mod_ref, ng_ref = next(it), next(it), next(it)
    wft_ref, wt_ref, wct_ref, wout_ref = next(it), next(it), next(it), next(it)
    prow_ref, pcol_ref = next(it), next(it)
    bd64_ref, bd32_ref, bc_ref, bs_ref, cs_ref, nss_ref = (next(it) for _ in range(6))
    if latent:
        ctxa_ref, ctxb_ref = next(it), next(it)
        cqa_ref, sqa_ref, cqb_ref, sqb_ref = (next(it) for _ in range(4))
        cka_ref, skpa_ref, skma_ref, ckb_ref, skpb_ref, skmb_ref = (next(it) for _ in range(6))
        y_ref = next(it)
    else:
        y_ref, newa_ref, newb_ref = next(it), next(it), next(it)
    xres_s, h_s, ka_s, kb_s, vt_s, t1_s, t2_s, mix_s, kn2a_s, kn2b_s = (next(it) for _ in range(10))

    l = pl.program_id(1)
    j = pl.program_id(2)
    mod_row = pl.ds(pl.program_id(0) + 1 if latent else 0, 1)
    shift = mod_ref[0, 0, mod_row, :]
    scale = mod_ref[0, 1, mod_row, :]
    gate = mod_ref[0, 2, mod_row, :]
    prow, pcol = prow_ref[l], pcol_ref[l]
    gka, gkb = prow[P_GKA:P_GKA + 1, 0:KV_WIDTH_A], prow[P_GKB:P_GKB + 1, :]
    gqa, gqb, gsub = pcol[:, C_GQA:C_GQA + 1], pcol[0:DIFF_DIM, C_GQB:C_GQB + 1], pcol[:, C_GSUB:C_GSUB + 1]

    @pl.when((l == 0) & (j == 0))
    def _():
        xres_s[...] = x_ref[...].reshape(grp, D_MODEL)
        for g in range(N_V_GROUPS):
            vt_s[g * V_GROUP_ROWS + HEAD_DIM:(g + 1) * V_GROUP_ROWS, :] = jnp.ones(
                (ONES_ROWS, grp + past), _BF16)

    def rope_k(k, cos_ref, sp_ref, sm_ref, rows, m):
        out = []
        for c0 in range(0, k.shape[1], LANES):
            kc = k[:, c0:c0 + LANES]
            out.append(kc * cos_ref[rows, :] + pltpu.roll(kc, m, 1) * sp_ref[rows, :]
                       + pltpu.roll(kc, LANES - m, 1) * sm_ref[rows, :])
        return _cat(out, 1)

    def note_key_norms(i, ka, kb, first):
        for k, bd_ref, dim, kn2_s in ((ka, bd64_ref, HEAD_DIM, kn2a_s), (kb, bd32_ref, DIFF_DIM, kn2b_s)):
            n2 = jnp.max(_dot((k * k).astype(_BF16), bd_ref[...]), axis=0, keepdims=True) * float(dim)
            kn2_s[i:i + 1, :] = n2 if first else jnp.maximum(kn2_s[i:i + 1, :], n2)

    def put_values(vt, cols):
        for g in range(N_V_GROUPS):
            vt_s[g * V_GROUP_ROWS:g * V_GROUP_ROWS + HEAD_DIM, cols] = vt[g * HEAD_DIM:(g + 1) * HEAD_DIM]

    @pl.when(j == 0)
    def _prep():
        for c in range(n_blk):
            rows = slice(c * TOKEN_BLOCK, (c + 1) * TOKEN_BLOCK)
            xc = xres_s[rows, :]
            ms = jnp.mean(xc * xc, axis=-1, keepdims=True)
            hc = (xc * lax.rsqrt(ms + RMS_EPS) * ng_ref[l] * (1.0 + scale) + shift).astype(_BF16)
            h_s[rows, :] = hc
            pt = _dot(hc, wt_ref[l])
            ka = pt[:, T_KA:T_KA + KV_WIDTH_A]
            va = pt[:, T_VA:T_VA + KV_WIDTH_A]
            kb = pt[:, T_KB:T_KB + WIDTH_B]
            vb = pt[:, T_VB:T_VB + WIDTH_B]
            vat, vbt = va.T, vb.T
            ka = ka * lax.rsqrt(_dot((ka * ka).astype(_BF16), bd64_ref[...]) + RMS_EPS) * gka
            kb = kb * lax.rsqrt(_dot((kb * kb).astype(_BF16), bd32_ref[...]) + RMS_EPS) * gkb
            if latent:
                ka = rope_k(ka, cka_ref, skpa_ref, skma_ref, rows, HEAD_DIM // 4)
                kb = rope_k(kb, ckb_ref, skpb_ref, skmb_ref, rows, DIFF_DIM // 4)
            else:
                kat, kbt = ka.T, kb.T
                for i in range(nb):
                    r = slice(i * seq, (i + 1) * seq)
                    newa_ref[i, 0, 0], newa_ref[i, 0, 1] = kat[:, r], vat[:, r]
                    newb_ref[i, 0, 0], newb_ref[i, 0, 1] = kbt[:, r], vbt[:, r]
            ka_s[rows, :] = ka.astype(_BF16)
            kb_s[rows, :] = kb.astype(_BF16)
            if not exact:
                for i in range(nb):
                    r = slice(i * seq, (i + 1) * seq) if nb > 1 else slice(None)
                    note_key_norms(i, ka[r], kb[r], first=(c == 0))
            put_values(jnp.concatenate([vat, vbt], axis=0).astype(_BF16), rows)
            uct = _dot_nt(wft_ref[l, F_UC:F_ROWS, :], hc).astype(_BF16)
            t1_s[:, rows] = _dot(bc_ref[...], uct).astype(_BF16)
            t2_s[:, rows] = _dot(bs_ref[...], uct).astype(_BF16)
        if latent:
            ka_ctx, kb_ctx = ctxa_ref[0, 0, 0].T, ctxb_ref[0, 0, 0].T
            ka_s[grp:grp + past, :] = ka_ctx.astype(_BF16)
            kb_s[grp:grp + past, :] = kb_ctx.astype(_BF16)
            if not exact:
                note_key_norms(0, ka_ctx, kb_ctx, first=False)
            put_values(jnp.concatenate([ctxa_ref[0, 0, 1], ctxb_ref[0, 0, 1]], axis=0).astype(_BF16),
                       slice(grp, grp + past))

    blk_rows = pl.ds(pl.multiple_of(j * TOKEN_BLOCK, TOKEN_BLOCK), TOKEN_BLOCK)

    pf_chunks = {}

    def pf_chunk(c):
        if c not in pf_chunks:
            pf_chunks[c] = _dot_nt(wft_ref[l, c * PF_CHUNK:(c + 1) * PF_CHUNK, :], h_s[blk_rows, :])
        return pf_chunks[c]

    def pf_rows(r0, n, cols):
        c, off = divmod(r0, PF_CHUNK)
        assert off + n <= PF_CHUNK
        return pf_chunk(c)[off:off + n, cols]

    lv = prow[P_LAM:P_LAM + 1, :]
    lq1, lk1, lq2, lk2 = (lv[:, i * DIFF_DIM:(i + 1) * DIFF_DIM] for i in range(4))
    lam_init = jnp.where(l == 0, LAM_INIT[0], LAM_INIT[1])
    lam = (jnp.exp(jnp.sum(lq1 * lk1, axis=-1, keepdims=True))
           - jnp.exp(jnp.sum(lq2 * lk2, axis=-1, keepdims=True)) + lam_init)

    def q_tile(r0, dim, gain, cos_ref, sin_ref, cols, pad_start, pad_total, kn2_s):
        q = pf_rows(r0, dim, cols)
        q = q * _rms_rows(q) * gain
        if latent:
            q = q * cos_ref[...] + _swap_halves(q, dim // 4) * sin_ref[...]
        bound = None
        if not exact:
            i = cols.start // tb if nb > 1 else 0
            kn2 = kn2_s[i:i + 1, pad_start:pad_start + 1]
            bound = jnp.sqrt(jnp.sum(q * q, axis=0, keepdims=True) * kn2) * BOUND_SLACK
        return _pad_rows(q.astype(_BF16), pad_start, pad_total), bound

    def q_a(h, cols):
        g = h // (N_HEADS_A // N_KV_A)
        return q_tile(F_QA + h * HEAD_DIM, HEAD_DIM, gqa, cqa_ref if latent else None,
                      sqa_ref if latent else None, cols, g * HEAD_DIM, KV_WIDTH_A, kn2a_s)

    def q_b(h, comp, cols):
        r0 = h * HEAD_DIM + comp * DIFF_DIM
        return q_tile(F_QB + r0, DIFF_DIM, gqb, cqb_ref if latent else None,
                      sqb_ref if latent else None, cols, r0, WIDTH_B, kn2b_s)

    def put_mix(gated_t, cols, c0):
        mix_s[cols, c0:c0 + gated_t.shape[0]] = gated_t.T.astype(_BF16)

    half = {}

    def put_head(slot, cols, gated_t):
        if slot % 2 == 0:
            half[cols.start] = gated_t
        else:
            put_mix(jnp.concatenate([half.pop(cols.start), gated_t], axis=0), cols, (slot - 1) * HEAD_DIM)

    def store_a(h, cols, o):
        ga = pf_rows(F_GA + h * HEAD_DIM, HEAD_DIM, cols)
        put_head(h, cols, o * _silu(ga))

    def store_b(h, cols, o1, o2):
        ob = o1 - lam * o2
        ob = ob * _rms_rows(ob) * gsub
        gb = pf_rows(F_GB + h * HEAD_DIM, HEAD_DIM, cols)
        put_head(N_HEADS_A + h, cols, ob * _silu(gb))

    def v_group(g):
        return slice(g * V_GROUP_ROWS, (g + 1) * V_GROUP_ROWS)

    units = []
    all_cols = slice(0, TOKEN_BLOCK)
    if nb == 1:
        keys = slice(0, grp + past)
        for h in range(N_HEADS_A):
            g = h // (N_HEADS_A // N_KV_A)
            units.append(("a", keys, v_group(g),
                          [functools.partial(q_a, h, all_cols)],
                          functools.partial(store_a, h, all_cols)))
        for h in range(N_HEADS_B):
            v_rows = v_group(N_KV_A + h)
            held = {}
            for comp in range(2):
                def fin(o, h=h, comp=comp, held=held):
                    held[comp] = o
                    if comp == 1:
                        store_b(h, all_cols, held[0], held[1])
                units.append(("b", keys, v_rows, [functools.partial(q_b, h, comp, all_cols)], fin))
    else:
        seqs = [(slice(i * seq, (i + 1) * seq), slice(i * tb, (i + 1) * tb)) for i in range(nb)]
        for u in range(N_HEADS_A // 2):
            g = (2 * u) // (N_HEADS_A // N_KV_A)
            for keys, cols in seqs:
                def fin(o, u=u, cols=cols):
                    store_a(2 * u, cols, o[:, 0:tb])
                    store_a(2 * u + 1, cols, o[:, tb:2 * tb])
                units.append(("a", keys, v_group(g),
                              [functools.partial(q_a, 2 * u, cols), functools.partial(q_a, 2 * u + 1, cols)], fin))
        for h in range(N_HEADS_B):
            for keys, cols in seqs:
                def fin(o, h=h, cols=cols):
                    store_b(h, cols, o[:, 0:tb], o[:, tb:2 * tb])
                units.append(("b", keys, v_group(N_KV_A + h),
                              [functools.partial(q_b, h, 0, cols), functools.partial(q_b, h, 1, cols)], fin))

    def scores(unit):
        kind, keys, _, tiles, _ = unit
        k_ref = ka_s if kind == "a" else kb_s
        qs, shifts = zip(*[t() for t in tiles])
        shift = None if exact else _cat(shifts, 1)
        return _dot(k_ref[keys, :], _cat(qs, 1)), shift

    def attend(unit, s_t, shift):
        o = _dot(vt_s[unit[2], unit[1]], _softmax_t(s_t, shift))
        unit[4](o[0:HEAD_DIM] / o[HEAD_DIM:HEAD_DIM + 1])

    def mixer_c():
        if nb == 1:
            ft = _dot(t1_s[...], cs_ref[j]) + _dot(t2_s[...], nss_ref[j])
        else:
            ft = _cat([_dot(t1_s[:, i * seq:(i + 1) * seq], cs_ref[0])
                       + _dot(t2_s[:, i * seq:(i + 1) * seq], nss_ref[0]) for i in range(nb)], 1)
        oc = _dot(wct_ref[l], ft.astype(_BF16))
        put_mix(oc * _silu(pf_rows(F_GC, WIDTH_C, all_cols)), all_cols, WIDTH_A + WIDTH_B)

    out_parts = []

    def out_chunk(k):
        cols = slice(k * OUT_CHUNK, (k + 1) * OUT_CHUNK)
        out_parts.append(_dot(mix_s[:, cols], wout_ref[l, cols, :]))

    fillers = {
        0: [functools.partial(pf_chunk, F_GA // PF_CHUNK)],
        1: [functools.partial(pf_chunk, F_QA // PF_CHUNK + 1)],
        2: [functools.partial(pf_chunk, F_GA // PF_CHUNK + 1)],
        3: [functools.partial(pf_chunk, F_QB // PF_CHUNK)],
        4: [functools.partial(pf_chunk, F_GB // PF_CHUNK)],
        5: [mixer_c],
        6: [functools.partial(out_chunk, 3), functools.partial(out_chunk, 0)],
        9: [functools.partial(out_chunk, 1)],
    }
    s_next = scores(units[0])
    for u, unit in enumerate(units):
        s_cur, shift = s_next
        if u + 1 < len(units):
            s_next = scores(units[u + 1])
        for work in fillers.get(u, ()):
            work()
        attend(unit, s_cur, shift)
    out_chunk(2)

    y = xres_s[blk_rows, :] + gate * functools.reduce(lambda a, b: a + b, out_parts)
    xres_s[blk_rows, :] = y

    @pl.when(l == DEPTH - 1)
    def _():
        y_ref[...] = y.reshape(y_ref.shape)


def _const_spec(shape):
    nd = len(shape)
    return pl.BlockSpec(shape, lambda b, l, j: (0,) * nd, pipeline_mode=pl.Buffered(1))


def _mixer(x, mod, wts, consts, nb, exact, ctx=None):
    latent = ctx is not None
    n_batch, seq, _ = x.shape
    grp = nb * seq
    n_blk = grp // TOKEN_BLOCK
    assert grp % TOKEN_BLOCK == 0 and (nb == 1 or grp == TOKEN_BLOCK) and n_batch % nb == 0 and nb <= SEQ_ROWS
    past = ctx[0].shape[-1] if latent else 0
    last = DEPTH - 1

    args = [x, mod]
    specs = [
        pl.BlockSpec((nb, seq, D_MODEL), lambda b, l, j: (b, 0, 0),
                     pipeline_mode=pl.Buffered(1) if latent else None),
        pl.BlockSpec((1, 3, MOD_ROWS, D_MODEL), lambda b, l, j: (l, 0, 0, 0)),
    ]
    for name in ("norm_g", "wft", "wt", "wct", "wout", "prow", "pcol"):
        args.append(wts[name])
        specs.append(_const_spec(wts[name].shape))
    for name in ("bd64", "bd32", "bc", "bs", "cs", "nss"):
        args.append(consts[name])
        specs.append(_const_spec(consts[name].shape))
    if latent:
        args += [ctx[0], ctx[1]]
        specs += [
            pl.BlockSpec((1, 1, 2, KV_WIDTH_A, past), lambda b, l, j: (b, l, 0, 0, 0), pipeline_mode=pl.Buffered(1)),
            pl.BlockSpec((1, 1, 2, WIDTH_B, past), lambda b, l, j: (b, l, 0, 0, 0), pipeline_mode=pl.Buffered(1)),
        ]
        for name, rows in (("cqa", HEAD_DIM), ("sqa", HEAD_DIM), ("cqb", DIFF_DIM), ("sqb", DIFF_DIM)):
            args.append(consts[name])
            specs.append(pl.BlockSpec((rows, TOKEN_BLOCK), lambda b, l, j: (0, j)))
        for name in ("cka", "skpa", "skma", "ckb", "skpb", "skmb"):
            args.append(consts[name])
            specs.append(_const_spec(consts[name].shape))

    y_shape = jax.ShapeDtypeStruct(x.shape, _F32)
    if nb == 1:
        y_spec = pl.BlockSpec((1, TOKEN_BLOCK, D_MODEL), lambda b, l, j: (b, jnp.where(l == last, j, 0), 0))
    else:
        y_spec = pl.BlockSpec((nb, seq, D_MODEL), lambda b, l, j: (b, 0, 0))
    if latent:
        out_shape, out_specs = y_shape, y_spec
    else:
        out_shape = (y_shape,
                     jax.ShapeDtypeStruct((n_batch, DEPTH, 2, KV_WIDTH_A, seq), _F32),
                     jax.ShapeDtypeStruct((n_batch, DEPTH, 2, WIDTH_B, seq), _F32))
        out_specs = (y_spec,
                     pl.BlockSpec((nb, 1, 2, KV_WIDTH_A, seq), lambda b, l, j: (b, l, 0, 0, 0)),
                     pl.BlockSpec((nb, 1, 2, WIDTH_B, seq), lambda b, l, j: (b, l, 0, 0, 0)))

    n_keys = grp + past
    scratch = [
        pltpu.VMEM((grp, D_MODEL), _F32),
        pltpu.VMEM((grp, D_MODEL), _BF16),
        pltpu.VMEM((n_keys, KV_WIDTH_A), _BF16),
        pltpu.VMEM((n_keys, WIDTH_B), _BF16),
        pltpu.VMEM((N_V_GROUPS * V_GROUP_ROWS, n_keys), _BF16),
        pltpu.VMEM((WIDTH_C, grp), _BF16),
        pltpu.VMEM((WIDTH_C, grp), _BF16),
        pltpu.VMEM((TOKEN_BLOCK, D_MODEL), _BF16),
        pltpu.VMEM((SEQ_ROWS, KV_WIDTH_A), _F32),
        pltpu.VMEM((SEQ_ROWS, WIDTH_B), _F32),
    ]
    kern = functools.partial(_mixer_kernel, latent=latent, seq=seq, nb=nb, past=past, exact=exact)
    return pl.pallas_call(
        kern,
        grid=(n_batch // nb, DEPTH, n_blk),
        in_specs=specs,
        out_specs=out_specs,
        out_shape=out_shape,
        scratch_shapes=scratch,
        compiler_params=pltpu.CompilerParams(
            dimension_semantics=("arbitrary", "arbitrary", "arbitrary"),
            vmem_limit_bytes=V7X_VMEM_LIMIT_BYTES),
        name=("latent_mixer" if latent else "context_mixer") + ("_exact" if exact else ""),
    )(*args)


def _rope_tables(n_tok, dim):
    m = dim // 4
    t = np.arange(n_tok)
    inv = 1.0 / (ROPE_BASE ** (np.arange(m, dtype=np.float64) / m))
    ar = (t // GRID_W)[:, None] * inv
    ac = (t % GRID_W)[:, None] * inv
    cos = np.concatenate([np.cos(ar), np.cos(ar), np.cos(ac), np.cos(ac)], axis=1)
    sin = np.concatenate([-np.sin(ar), np.sin(ar), -np.sin(ac), np.sin(ac)], axis=1)
    first = np.tile(np.concatenate([np.ones(m), np.zeros(m)]), 2)[None, :]
    return cos.astype(np.float32), sin.astype(np.float32), first.astype(np.float32)


def _block_diag(block, n):
    out = np.zeros((block.shape[0] * n, block.shape[1] * n), np.float64)
    for i in range(n):
        out[i * block.shape[0]:(i + 1) * block.shape[0], i * block.shape[1]:(i + 1) * block.shape[1]] = block
    return out


def _dft_cos_sin(n):
    k = np.arange(n)
    ang = 2.0 * np.pi * ((k[:, None] * k[None, :]) % n) / n
    return np.cos(ang), np.sin(ang)


def _mixer_consts(seq, latent):
    c64, s64 = _dft_cos_sin(GROUP_C)
    norm = 1.0 / math.sqrt(GROUP_C * seq)
    cs, ss = _dft_cos_sin(seq)
    f32c = lambda a: jnp.asarray(np.asarray(a, np.float32))
    width = min(seq, TOKEN_BLOCK)
    col_blocks = lambda a: a.reshape(seq, seq // width, width).transpose(1, 0, 2)
    consts = {
        "bd64": f32c(_block_diag(np.full((HEAD_DIM, HEAD_DIM), 1.0 / HEAD_DIM), N_KV_A)).astype(_BF16),
        "bd32": f32c(_block_diag(np.full((DIFF_DIM, DIFF_DIM), 1.0 / DIFF_DIM), 2 * N_HEADS_B)).astype(_BF16),
        "bc": f32c(_block_diag(c64, N_GROUPS_C) * norm).astype(_BF16),
        "bs": f32c(_block_diag(s64, N_GROUPS_C) * norm).astype(_BF16),
        "cs": f32c(col_blocks(cs)).astype(_BF16),
        "nss": f32c(col_blocks(-ss)).astype(_BF16),
    }
    if latent:
        cos_a, sin_a, first_a = _rope_tables(seq, HEAD_DIM)
        cos_b, sin_b, first_b = _rope_tables(seq, DIFF_DIM)
        rep_a, rep_b = LANES // HEAD_DIM, LANES // DIFF_DIM
        consts.update({
            "cqa": f32c(cos_a.T), "sqa": f32c(sin_a.T), "cqb": f32c(cos_b.T), "sqb": f32c(sin_b.T),
            "cka": f32c(np.tile(cos_a, (1, rep_a))),
            "skpa": f32c(np.tile(sin_a * (1.0 - first_a), (1, rep_a))),
            "skma": f32c(np.tile(sin_a * first_a, (1, rep_a))),
            "ckb": f32c(np.tile(cos_b, (1, rep_b))),
            "skpb": f32c(np.tile(sin_b * (1.0 - first_b), (1, rep_b))),
            "skmb": f32c(np.tile(sin_b * first_b, (1, rep_b))),
        })
    return consts


def _mixer_weights(norm_g, w_in, q_norm_a, k_norm_a, q_norm_b, k_norm_b,
                   lambda_q1, lambda_k1, lambda_q2, lambda_k2, subln_g, w_fourier, w_out):
    keep = jnp.asarray([1.0 - v for v in LAM_INIT], _F32)
    gqa = q_norm_a * (HEAD_DIM ** -0.5 * LOG2_E)
    gqb = q_norm_b * (DIFF_DIM ** -0.5 * LOG2_E)
    row = lambda v: jnp.pad(v, ((0, 0), (0, P_LANES - v.shape[1])))
    prow = jnp.stack([row(jnp.tile(k_norm_a, (1, N_KV_A))), row(jnp.tile(k_norm_b, (1, 2 * N_HEADS_B))),
                      row(jnp.concatenate([lambda_q1, lambda_k1, lambda_q2, lambda_k2], axis=1))]
                     + [jnp.zeros((DEPTH, P_LANES), _F32)] * (P_ROWS - 3), axis=1)
    col = lambda v: jnp.pad(v, ((0, 0), (0, HEAD_DIM - v.shape[1])))
    pcol = jnp.stack([col(gqa), col(gqb), col(subln_g * keep[:, None])]
                     + [jnp.zeros((DEPTH, HEAD_DIM), _F32)] * (C_COLS - 3), axis=2)
    return {
        "norm_g": norm_g.reshape(DEPTH, 1, D_MODEL),
        "wft": _relayout(w_in, FEATURE_TILES, WEIGHT_TILE, True, "w_in_feature_major"),
        "wt": _relayout(w_in, TOKEN_TILES, WEIGHT_TILE, False, "w_in_token_major"),
        "wct": _relayout(w_fourier, ((0,),), WIDTH_C, True, "w_fourier_t"),
        "wout": _relayout(w_out, ((0,),), D_MODEL, False, "w_out_bf16"),
        "prow": prow,
        "pcol": pcol,
    }


def kernel(x_prompt, x_sample, cache_attn_a, cache_attn_b, c, c_ctx, norm_g, w_mod, b_mod, w_in,
           q_norm_a, k_norm_a, q_norm_b, k_norm_b, lambda_q1, lambda_k1, lambda_q2, lambda_k2,
           subln_g, w_fourier, w_out):
    n_ctx_batch, ctx_seq, _ = x_prompt.shape
    n_lat_batch, lat_seq, _ = x_sample.shape
    past = cache_attn_a.shape[2]
    assert n_lat_batch + 1 <= MOD_ROWS and w_in.shape == (DEPTH, D_MODEL, D_IN)

    cond = jnp.concatenate(
        [c_ctx[None, :], c, jnp.zeros((MOD_ROWS - 1 - n_lat_batch, D_MODEL), _F32)], axis=0)
    mod = _modulation(cond, w_mod, b_mod)

    weights = _mixer_weights(norm_g, w_in, q_norm_a, k_norm_a, q_norm_b, k_norm_b,
                             lambda_q1, lambda_k1, lambda_q2, lambda_k2, subln_g, w_fourier, w_out)

    def feature_major(cache):
        n_heads, dim = cache.shape[-2:]
        return jnp.transpose(cache, (0, 1, 3, 4, 5, 2)).reshape(n_lat_batch, DEPTH, 2, n_heads * dim, past)

    def token_major(new, n_heads):
        seq = new.shape[-1]
        return jnp.transpose(new.reshape(n_ctx_batch, DEPTH, 2, n_heads, HEAD_DIM, seq), (0, 1, 5, 2, 3, 4))

    ctx_a, ctx_b = feature_major(cache_attn_a), feature_major(cache_attn_b)

    def max_norm(gain, dim):
        return math.sqrt(dim) * jnp.max(jnp.abs(gain), axis=-1)

    q_a = max_norm(q_norm_a * (HEAD_DIM ** -0.5 * LOG2_E), HEAD_DIM)
    q_b = max_norm(q_norm_b * (DIFF_DIM ** -0.5 * LOG2_E), DIFF_DIM)
    k_a, k_b = max_norm(k_norm_a, HEAD_DIM), max_norm(k_norm_b, DIFF_DIM)
    cached_a, cached_b = _cached_key_norms(ctx_a, ctx_b)
    ctx_bound = GUARD_SLACK * jnp.maximum(q_a * k_a, q_b * k_b)
    lat_bound = GUARD_SLACK * jnp.maximum(q_a * jnp.maximum(k_a, cached_a), q_b * jnp.maximum(k_b, cached_b))

    ctx_consts = _mixer_consts(ctx_seq, latent=False)
    lat_consts = _mixer_consts(lat_seq, latent=True)

    def context(exact):
        return _mixer(x_prompt, mod, weights, ctx_consts, nb=TOKEN_BLOCK // ctx_seq, exact=exact)

    def latent(exact, x):
        return _mixer(x, mod, weights, lat_consts, nb=1, exact=exact, ctx=(ctx_a, ctx_b))

    def both(exact):
        ctx_out = context(exact)
        _, x = lax.optimization_barrier((ctx_out[0], x_sample))
        return ctx_out, latent(exact, x)

    bounded = jnp.all(ctx_bound <= MAX_SCORE_BOUND) & jnp.all(lat_bound <= MAX_SCORE_BOUND)
    (y_prompt, new_a, new_b), y_sample = lax.cond(bounded, functools.partial(both, False),
                                                  functools.partial(both, True))
    return (y_prompt, y_sample, token_major(new_a, N_KV_A), token_major(new_b, N_HEADS_B))
```

```python
import functools
import math

import jax
import jax.numpy as jnp
import numpy as np
from jax import lax
from jax.experimental import pallas as pl
from jax.experimental.pallas import tpu as pltpu

D_MODEL = 1024
DEPTH = 2
GRID_W = 64
HEAD_DIM = 64
N_HEADS_A = 8
N_KV_A = 2
N_HEADS_B = 4
DIFF_DIM = 32
GROUP_C = 64
N_GROUPS_C = 4
WIDTH_A = N_HEADS_A * HEAD_DIM
WIDTH_B = N_HEADS_B * HEAD_DIM
WIDTH_C = N_GROUPS_C * GROUP_C
KV_WIDTH_A = N_KV_A * HEAD_DIM
D_IN = 2 * WIDTH_A + 2 * KV_WIDTH_A + 4 * WIDTH_B + 2 * WIDTH_C
RMS_EPS = 1e-6
ROPE_BASE = 10000.0
LOG2_E = math.log2(math.e)
LAM_INIT = tuple(0.8 - 0.6 * math.exp(-0.3 * l) for l in range(DEPTH))

LANES = 128
TOKEN_BLOCK = 512
MOD_ROWS = 8
WEIGHT_TILE = 256
V7X_VMEM_LIMIT_BYTES = 58 * 1024 * 1024
BOUND_SLACK = 1.02
MAX_SCORE_BOUND = 40.0
GUARD_SLACK = 1.05
SEQ_ROWS = 8
P_ROWS, P_LANES = 8, 256
P_GKA, P_GKB, P_LAM = 0, 1, 2
C_COLS = 8
C_GQA, C_GQB, C_GSUB = 0, 1, 2

FEATURE_TILES = ((0, 1), (3, 4), (5, 8), (10, 9))
TOKEN_TILES = ((2, 6, 7),)
F_QA, F_GA, F_QB, F_GB, F_GC, F_UC = 0, 512, 1024, 1280, 1536, 1792
PF_CHUNK = 256
OUT_CHUNK = 256
F_ROWS = 2048
T_KA, T_VA, T_KB, T_VB = 0, 128, 256, 512
T_COLS = 768
ONES_ROWS = 16
V_GROUP_ROWS = HEAD_DIM + ONES_ROWS
N_V_GROUPS = N_KV_A + N_HEADS_B

_BF16 = jnp.bfloat16
_F32 = jnp.float32


def _silu(x):
    return x * (1.0 / (1.0 + jnp.exp(-x)))


def _dot(a, b):
    return jnp.dot(a, b, preferred_element_type=_F32)


def _dot_nt(a, b):
    return lax.dot_general(a, b, (((1,), (1,)), ((), ())), preferred_element_type=_F32)


def _mod_kernel(c_ref, w_ref, b_ref, o_ref):
    c = c_ref[...]
    o_ref[0, 0] = _dot(_silu(c).astype(_BF16), w_ref[0].astype(_BF16)) + b_ref[0]


def _modulation(cond, w_mod, b_mod):
    n_tile = D_MODEL
    return pl.pallas_call(
        _mod_kernel,
        grid=(DEPTH, 3 * D_MODEL // n_tile),
        in_specs=[
            pl.BlockSpec((MOD_ROWS, D_MODEL), lambda l, n: (0, 0)),
            pl.BlockSpec((1, D_MODEL, n_tile), lambda l, n: (l, 0, n)),
            pl.BlockSpec((1, 1, n_tile), lambda l, n: (l, 0, n)),
        ],
        out_specs=pl.BlockSpec((1, 1, MOD_ROWS, n_tile), lambda l, n: (l, n, 0, 0)),
        out_shape=jax.ShapeDtypeStruct((DEPTH, 3, MOD_ROWS, D_MODEL), _F32),
        name="modulation",
    )(cond, w_mod, b_mod.reshape(DEPTH, 1, 3 * D_MODEL))


def _relayout_kernel(tiles_ref, *refs, transpose):
    del tiles_ref
    *w_refs, o_ref = refs
    for k, w_ref in enumerate(w_refs):
        w = w_ref[0]
        width = w.shape[1]
        if transpose:
            o_ref[0, k * width:(k + 1) * width, :] = w.T.astype(_BF16)
        else:
            o_ref[0, :, k * width:(k + 1) * width] = w.astype(_BF16)


def _relayout(w, groups, width, transpose, name):
    depth, rows, _ = w.shape
    n, g = len(groups), len(groups[0])
    assert all(len(grp) == g for grp in groups)
    if transpose:
        out_shape = (depth, n * g * width, rows)
        out_spec = pl.BlockSpec((1, g * width, rows), lambda l, i, t: (l, i, 0))
    else:
        out_shape = (depth, rows, n * g * width)
        out_spec = pl.BlockSpec((1, rows, g * width), lambda l, i, t: (l, 0, i))

    def tile_spec(k):
        return pl.BlockSpec((1, rows, width), lambda l, i, t: (l, 0, t[i * g + k]))

    return pl.pallas_call(
        functools.partial(_relayout_kernel, transpose=transpose),
        grid_spec=pltpu.PrefetchScalarGridSpec(
            num_scalar_prefetch=1,
            grid=(depth, n),
            in_specs=[tile_spec(k) for k in range(g)],
            out_specs=out_spec),
        out_shape=jax.ShapeDtypeStruct(out_shape, _BF16),
        name=name,
    )(jnp.asarray(groups, jnp.int32).reshape(-1), *([w] * g))


def _key_norm_kernel(ka_ref, kb_ref, o_ref):
    def max_norm2(k, dim):
        k2 = k * k
        per_head = [jnp.sum(k2[r:r + dim], axis=0, keepdims=True) for r in range(0, k.shape[0], dim)]
        return jnp.max(functools.reduce(jnp.maximum, per_head), axis=1, keepdims=True)

    o_ref[0, 0, 0:1, :] = jnp.broadcast_to(max_norm2(ka_ref[0, 0, 0], HEAD_DIM), (1, LANES))
    o_ref[0, 0, 1:2, :] = jnp.broadcast_to(max_norm2(kb_ref[0, 0, 0], DIFF_DIM), (1, LANES))
    o_ref[0, 0, 2:, :] = jnp.zeros((SEQ_ROWS - 2, LANES), _F32)


def _cached_key_norms(ctx_a, ctx_b):
    n_batch, _, _, _, past = ctx_a.shape
    n2 = pl.pallas_call(
        _key_norm_kernel,
        grid=(n_batch, DEPTH),
        in_specs=[pl.BlockSpec((1, 1, 1, KV_WIDTH_A, past), lambda b, l: (b, l, 0, 0, 0)),
                  pl.BlockSpec((1, 1, 1, WIDTH_B, past), lambda b, l: (b, l, 0, 0, 0))],
        out_specs=pl.BlockSpec((1, 1, SEQ_ROWS, LANES), lambda b, l: (b, l, 0, 0)),
        out_shape=jax.ShapeDtypeStruct((n_batch, DEPTH, SEQ_ROWS, LANES), _F32),
        name="cached_key_norms",
    )(ctx_a, ctx_b)
    return jnp.sqrt(jnp.max(n2[:, :, 0, 0], axis=0)), jnp.sqrt(jnp.max(n2[:, :, 1, 0], axis=0))


def _softmax_t(s_t, shift=None):
    if shift is None:
        shift = jnp.max(s_t, axis=0, keepdims=True)
    return jnp.exp2(s_t - shift).astype(_BF16)


def _rms_rows(x):
    return lax.rsqrt(jnp.mean(x * x, axis=0, keepdims=True) + RMS_EPS)


def _swap_halves(x, m):
    return jnp.concatenate([x[m:2 * m], x[0:m], x[3 * m:4 * m], x[2 * m:3 * m]], axis=0)


def _pad_rows(x, start, total):
    parts = []
    if start:
        parts.append(jnp.zeros((start, x.shape[1]), x.dtype))
    parts.append(x)
    rest = total - start - x.shape[0]
    if rest:
        parts.append(jnp.zeros((rest, x.shape[1]), x.dtype))
    return jnp.concatenate(parts, axis=0)


def _cat(parts, axis):
    return parts[0] if len(parts) == 1 else jnp.concatenate(parts, axis=axis)


def _mixer_kernel(*refs, latent, seq, nb, past, exact):
    grp = nb * seq
    n_blk = grp // TOKEN_BLOCK
    tb = TOKEN_BLOCK // nb
    it = iter(refs)
    x_ref, mod_ref, ng_ref = next(it), next(it), next(it)
    wft_ref, wt_ref, wct_ref, wout_ref = next(it), next(it), next(it), next(it)
    prow_ref, pcol_ref = next(it), next(it)
    bd64_ref, bd32_ref, bc_ref, bs_ref, cs_ref, nss_ref = (next(it) for _ in range(6))
    if latent:
        ctxa_ref, ctxb_ref = next(it), next(it)
        cqa_ref, sqa_ref, cqb_ref, sqb_ref = (next(it) for _ in range(4))
        cka_ref, skpa_ref, skma_ref, ckb_ref, skpb_ref, skmb_ref = (next(it) for _ in range(6))
        y_ref = next(it)
    else:
        y_ref, newa_ref, newb_ref = next(it), next(it), next(it)
    xres_s, h_s, ka_s, kb_s, vt_s, t1_s, t2_s, mix_s, kn2a_s, kn2b_s = (next(it) for _ in range(10))

    l = pl.program_id(1)
    j = pl.program_id(2)
    mod_row = pl.ds(pl.program_id(0) + 1 if latent else 0, 1)
    shift = mod_ref[0, 0, mod_row, :]
    scale = mod_ref[0, 1, mod_row, :]
    gate = mod_ref[0, 2, mod_row, :]
    prow, pcol = prow_ref[l], pcol_ref[l]
    gka, gkb = prow[P_GKA:P_GKA + 1, 0:KV_WIDTH_A], prow[P_GKB:P_GKB + 1, :]
    gqa, gqb, gsub = pcol[:, C_GQA:C_GQA + 1], pcol[0:DIFF_DIM, C_GQB:C_GQB + 1], pcol[:, C_GSUB:C_GSUB + 1]

    @pl.when((l == 0) & (j == 0))
    def _():
        xres_s[...] = x_ref[...].reshape(grp, D_MODEL)
        for g in range(N_V_GROUPS):
            vt_s[g * V_GROUP_ROWS + HEAD_DIM:(g + 1) * V_GROUP_ROWS, :] = jnp.ones(
                (ONES_ROWS, grp + past), _BF16)

    def rope_k(k, cos_ref, sp_ref, sm_ref, rows, m):
        out = []
        for c0 in range(0, k.shape[1], LANES):
            kc = k[:, c0:c0 + LANES]
            out.append(kc * cos_ref[rows, :] + pltpu.roll(kc, m, 1) * sp_ref[rows, :]
                       + pltpu.roll(kc, LANES - m, 1) * sm_ref[rows, :])
        return _cat(out, 1)

    def note_key_norms(i, ka, kb, first):
        for k, bd_ref, dim, kn2_s in ((ka, bd64_ref, HEAD_DIM, kn2a_s), (kb, bd32_ref, DIFF_DIM, kn2b_s)):
            n2 = jnp.max(_dot((k * k).astype(_BF16), bd_ref[...]), axis=0, keepdims=True) * float(dim)
            kn2_s[i:i + 1, :] = n2 if first else jnp.maximum(kn2_s[i:i + 1, :], n2)

    def put_values(vt, cols):
        for g in range(N_V_GROUPS):
            vt_s[g * V_GROUP_ROWS:g * V_GROUP_ROWS + HEAD_DIM, cols] = vt[g * HEAD_DIM:(g + 1) * HEAD_DIM]

    @pl.when(j == 0)
    def _prep():
        for c in range(n_blk):
            rows = slice(c * TOKEN_BLOCK, (c + 1) * TOKEN_BLOCK)
            xc = xres_s[rows, :]
            ms = jnp.mean(xc * xc, axis=-1, keepdims=True)
            hc = (xc * lax.rsqrt(ms + RMS_EPS) * ng_ref[l] * (1.0 + scale) + shift).astype(_BF16)
            h_s[rows, :] = hc
            pt = _dot(hc, wt_ref[l])
            ka = pt[:, T_KA:T_KA + KV_WIDTH_A]
            va = pt[:, T_VA:T_VA + KV_WIDTH_A]
            kb = pt[:, T_KB:T_KB + WIDTH_B]
            vb = pt[:, T_VB:T_VB + WIDTH_B]
            vat, vbt = va.T, vb.T
            ka = ka * lax.rsqrt(_dot((ka * ka).astype(_BF16), bd64_ref[...]) + RMS_EPS) * gka
            kb = kb * lax.rsqrt(_dot((kb * kb).astype(_BF16), bd32_ref[...]) + RMS_EPS) * gkb
            if latent:
                ka = rope_k(ka, cka_ref, skpa_ref, skma_ref, rows, HEAD_DIM // 4)
                kb = rope_k(kb, ckb_ref, skpb_ref, skmb_ref, rows, DIFF_DIM // 4)
            else:
                kat, kbt = ka.T, kb.T
                for i in range(nb):
                    r = slice(i * seq, (i + 1) * seq)
                    newa_ref[i, 0, 0], newa_ref[i, 0, 1] = kat[:, r], vat[:, r]
                    newb_ref[i, 0, 0], newb_ref[i, 0, 1] = kbt[:, r], vbt[:, r]
            ka_s[rows, :] = ka.astype(_BF16)
            kb_s[rows, :] = kb.astype(_BF16)
            if not exact:
                for i in range(nb):
                    r = slice(i * seq, (i + 1) * seq) if nb > 1 else slice(None)
                    note_key_norms(i, ka[r], kb[r], first=(c == 0))
            put_values(jnp.concatenate([vat, vbt], axis=0).astype(_BF16), rows)
            uct = _dot_nt(wft_ref[l, F_UC:F_ROWS, :], hc).astype(_BF16)
            t1_s[:, rows] = _dot(bc_ref[...], uct).astype(_BF16)
            t2_s[:, rows] = _dot(bs_ref[...], uct).astype(_BF16)
        if latent:
            ka_ctx, kb_ctx = ctxa_ref[0, 0, 0].T, ctxb_ref[0, 0, 0].T
            ka_s[grp:grp + past, :] = ka_ctx.astype(_BF16)
            kb_s[grp:grp + past, :] = kb_ctx.astype(_BF16)
            if not exact:
                note_key_norms(0, ka_ctx, kb_ctx, first=False)
            put_values(jnp.concatenate([ctxa_ref[0, 0, 1], ctxb_ref[0, 0, 1]], axis=0).astype(_BF16),
                       slice(grp, grp + past))

    blk_rows = pl.ds(pl.multiple_of(j * TOKEN_BLOCK, TOKEN_BLOCK), TOKEN_BLOCK)

    pf_chunks = {}

    def pf_chunk(c):
        if c not in pf_chunks:
            pf_chunks[c] = _dot_nt(wft_ref[l, c * PF_CHUNK:(c + 1) * PF_CHUNK, :], h_s[blk_rows, :])
        return pf_chunks[c]

    def pf_rows(r0, n, cols):
        c, off = divmod(r0, PF_CHUNK)
        assert off + n <= PF_CHUNK
        return pf_chunk(c)[off:off + n, cols]

    lv = prow[P_LAM:P_LAM + 1, :]
    lq1, lk1, lq2, lk2 = (lv[:, i * DIFF_DIM:(i + 1) * DIFF_DIM] for i in range(4))
    lam_init = jnp.where(l == 0, LAM_INIT[0], LAM_INIT[1])
    lam = (jnp.exp(jnp.sum(lq1 * lk1, axis=-1, keepdims=True))
           - jnp.exp(jnp.sum(lq2 * lk2, axis=-1, keepdims=True)) + lam_init)

    def q_tile(r0, dim, gain, cos_ref, sin_ref, cols, pad_start, pad_total, kn2_s):
        q = pf_rows(r0, dim, cols)
        q = q * _rms_rows(q) * gain
        if latent:
            q = q * cos_ref[...] + _swap_halves(q, dim // 4) * sin_ref[...]
        bound = None
        if not exact:
            i = cols.start // tb if nb > 1 else 0
            kn2 = kn2_s[i:i + 1, pad_start:pad_start + 1]
            bound = jnp.sqrt(jnp.sum(q * q, axis=0, keepdims=True) * kn2) * BOUND_SLACK
        return _pad_rows(q.astype(_BF16), pad_start, pad_total), bound

    def q_a(h, cols):
        g = h // (N_HEADS_A // N_KV_A)
        return q_tile(F_QA + h * HEAD_DIM, HEAD_DIM, gqa, cqa_ref if latent else None,
                      sqa_ref if latent else None, cols, g * HEAD_DIM, KV_WIDTH_A, kn2a_s)

    def q_b(h, comp, cols):
        r0 = h * HEAD_DIM + comp * DIFF_DIM
        return q_tile(F_QB + r0, DIFF_DIM, gqb, cqb_ref if latent else None,
                      sqb_ref if latent else None, cols, r0, WIDTH_B, kn2b_s)

    def put_mix(gated_t, cols, c0):
        mix_s[cols, c0:c0 + gated_t.shape[0]] = gated_t.T.astype(_BF16)

    half = {}

    def put_head(slot, cols, gated_t):
        if slot % 2 == 0:
            half[cols.start] = gated_t
        else:
            put_mix(jnp.concatenate([half.pop(cols.start), gated_t], axis=0), cols, (slot - 1) * HEAD_DIM)

    def store_a(h, cols, o):
        ga = pf_rows(F_GA + h * HEAD_DIM, HEAD_DIM, cols)
        put_head(h, cols, o * _silu(ga))

    def store_b(h, cols, o1, o2):
        ob = o1 - lam * o2
        ob = ob * _rms_rows(ob) * gsub
        gb = pf_rows(F_GB + h * HEAD_DIM, HEAD_DIM, cols)
        put_head(N_HEADS_A + h, cols, ob * _silu(gb))

    def v_group(g):
        return slice(g * V_GROUP_ROWS, (g + 1) * V_GROUP_ROWS)

    units = []
    all_cols = slice(0, TOKEN_BLOCK)
    if nb == 1:
        keys = slice(0, grp + past)
        for h in range(N_HEADS_A):
            g = h // (N_HEADS_A // N_KV_A)
            units.append(("a", keys, v_group(g),
                          [functools.partial(q_a, h, all_cols)],
                          functools.partial(store_a, h, all_cols)))
        for h in range(N_HEADS_B):
            v_rows = v_group(N_KV_A + h)
            held = {}
            for comp in range(2):
                def fin(o, h=h, comp=comp, held=held):
                    held[comp] = o
                    if comp == 1:
                        store_b(h, all_cols, held[0], held[1])
                units.append(("b", keys, v_rows, [functools.partial(q_b, h, comp, all_cols)], fin))
    else:
        seqs = [(slice(i * seq, (i + 1) * seq), slice(i * tb, (i + 1) * tb)) for i in range(nb)]
        for u in range(N_HEADS_A // 2):
            g = (2 * u) // (N_HEADS_A // N_KV_A)
            for keys, cols in seqs:
                def fin(o, u=u, cols=cols):
                    store_a(2 * u, cols, o[:, 0:tb])
                    store_a(2 * u + 1, cols, o[:, tb:2 * tb])
                units.append(("a", keys, v_group(g),
                              [functools.partial(q_a, 2 * u, cols), functools.partial(q_a, 2 * u + 1, cols)], fin))
        for h in range(N_HEADS_B):
            for keys, cols in seqs:
                def fin(o, h=h, cols=cols):
                    store_b(h, cols, o[:, 0:tb], o[:, tb:2 * tb])
                units.append(("b", keys, v_group(N_KV_A + h),
                              [functools.partial(q_b, h, 0, cols), functools.partial(q_b, h, 1, cols)], fin))

    def scores(unit):
        kind, keys, _, tiles, _ = unit
        k_ref = ka_s if kind == "a" else kb_s
        qs, shifts = zip(*[t() for t in tiles])
        shift = None if exact else _cat(shifts, 1)
        return _dot(k_ref[keys, :], _cat(qs, 1)), shift

    def attend(unit, s_t, shift):
        o = _dot(vt_s[unit[2], unit[1]], _softmax_t(s_t, shift))
        unit[4](o[0:HEAD_DIM] / o[HEAD_DIM:HEAD_DIM + 1])

    def mixer_c():
        if nb == 1:
            ft = _dot(t1_s[...], cs_ref[j]) + _dot(t2_s[...], nss_ref[j])
        else:
            ft = _cat([_dot(t1_s[:, i * seq:(i + 1) * seq], cs_ref[0])
                       + _dot(t2_s[:, i * seq:(i + 1) * seq], nss_ref[0]) for i in range(nb)], 1)
        oc = _dot(wct_ref[l], ft.astype(_BF16))
        put_mix(oc * _silu(pf_rows(F_GC, WIDTH_C, all_cols)), all_cols, WIDTH_A + WIDTH_B)

    out_parts = []

    def out_chunk(k):
        cols = slice(k * OUT_CHUNK, (k + 1) * OUT_CHUNK)
        part = _dot(mix_s[:, cols], wout_ref[l, cols, :])
        out_parts[:] = [out_parts[0] + part] if out_parts else [part]

    fillers = {
        0: [functools.partial(pf_chunk, F_GA // PF_CHUNK)],
        1: [functools.partial(pf_chunk, F_QA // PF_CHUNK + 1)],
        2: [functools.partial(pf_chunk, F_GA // PF_CHUNK + 1)],
        3: [functools.partial(pf_chunk, F_QB // PF_CHUNK)],
        4: [functools.partial(pf_chunk, F_GB // PF_CHUNK)],
        5: [mixer_c],
        6: [functools.partial(out_chunk, 3), functools.partial(out_chunk, 0)],
        9: [functools.partial(out_chunk, 1)],
    }
    s_next = scores(units[0])
    for u, unit in enumerate(units):
        s_cur, shift = s_next
        if u + 1 < len(units):
            s_next = scores(units[u + 1])
        for work in fillers.get(u, ()):
            work()
        attend(unit, s_cur, shift)
    out_chunk(2)

    y = xres_s[blk_rows, :] + gate * out_parts[0]
    xres_s[blk_rows, :] = y

    @pl.when(l == DEPTH - 1)
    def _():
        y_ref[...] = y.reshape(y_ref.shape)


def _const_spec(shape):
    nd = len(shape)
    return pl.BlockSpec(shape, lambda b, l, j: (0,) * nd, pipeline_mode=pl.Buffered(1))


def _mixer(x, mod, wts, consts, nb, exact, ctx=None):
    latent = ctx is not None
    n_batch, seq, _ = x.shape
    grp = nb * seq
    n_blk = grp // TOKEN_BLOCK
    assert grp % TOKEN_BLOCK == 0 and (nb == 1 or grp == TOKEN_BLOCK) and n_batch % nb == 0 and nb <= SEQ_ROWS
    past = ctx[0].shape[-1] if latent else 0
    last = DEPTH - 1

    args = [x, mod]
    specs = [
        pl.BlockSpec((nb, seq, D_MODEL), lambda b, l, j: (b, 0, 0),
                     pipeline_mode=pl.Buffered(1) if latent else None),
        pl.BlockSpec((1, 3, MOD_ROWS, D_MODEL), lambda b, l, j: (l, 0, 0, 0)),
    ]
    for name in ("norm_g", "wft", "wt", "wct", "wout", "prow", "pcol"):
        args.append(wts[name])
        specs.append(_const_spec(wts[name].shape))
    for name in ("bd64", "bd32", "bc", "bs", "cs", "nss"):
        args.append(consts[name])
        specs.append(_const_spec(consts[name].shape))
    if latent:
        args += [ctx[0], ctx[1]]
        specs += [
            pl.BlockSpec((1, 1, 2, KV_WIDTH_A, past), lambda b, l, j: (b, l, 0, 0, 0), pipeline_mode=pl.Buffered(1)),
            pl.BlockSpec((1, 1, 2, WIDTH_B, past), lambda b, l, j: (b, l, 0, 0, 0), pipeline_mode=pl.Buffered(1)),
        ]
        for name, rows in (("cqa", HEAD_DIM), ("sqa", HEAD_DIM), ("cqb", DIFF_DIM), ("sqb", DIFF_DIM)):
            args.append(consts[name])
            specs.append(pl.BlockSpec((rows, TOKEN_BLOCK), lambda b, l, j: (0, j)))
        for name in ("cka", "skpa", "skma", "ckb", "skpb", "skmb"):
            args.append(consts[name])
            specs.append(_const_spec(consts[name].shape))

    y_shape = jax.ShapeDtypeStruct(x.shape, _F32)
    if nb == 1:
        y_spec = pl.BlockSpec((1, TOKEN_BLOCK, D_MODEL), lambda b, l, j: (b, jnp.where(l == last, j, 0), 0))
    else:
        y_spec = pl.BlockSpec((nb, seq, D_MODEL), lambda b, l, j: (b, 0, 0))
    if latent:
        out_shape, out_specs = y_shape, y_spec
    else:
        out_shape = (y_shape,
                     jax.ShapeDtypeStruct((n_batch, DEPTH, 2, KV_WIDTH_A, seq), _F32),
                     jax.ShapeDtypeStruct((n_batch, DEPTH, 2, WIDTH_B, seq), _F32))
        out_specs = (y_spec,
                     pl.BlockSpec((nb, 1, 2, KV_WIDTH_A, seq), lambda b, l, j: (b, l, 0, 0, 0)),
                     pl.BlockSpec((nb, 1, 2, WIDTH_B, seq), lambda b, l, j: (b, l, 0, 0, 0)))

    n_keys = grp + past
    scratch = [
        pltpu.VMEM((grp, D_MODEL), _F32),
        pltpu.VMEM((grp, D_MODEL), _BF16),
        pltpu.VMEM((n_keys, KV_WIDTH_A), _BF16),
        pltpu.VMEM((n_keys, WIDTH_B), _BF16),
        pltpu.VMEM((N_V_GROUPS * V_GROUP_ROWS, n_keys), _BF16),
        pltpu.VMEM((WIDTH_C, grp), _BF16),
        pltpu.VMEM((WIDTH_C, grp), _BF16),
        pltpu.VMEM((TOKEN_BLOCK, D_MODEL), _BF16),
        pltpu.VMEM((SEQ_ROWS, KV_WIDTH_A), _F32),
        pltpu.VMEM((SEQ_ROWS, WIDTH_B), _F32),
    ]
    kern = functools.partial(_mixer_kernel, latent=latent, seq=seq, nb=nb, past=past, exact=exact)
    return pl.pallas_call(
        kern,
        grid=(n_batch // nb, DEPTH, n_blk),
        in_specs=specs,
        out_specs=out_specs,
        out_shape=out_shape,
        scratch_shapes=scratch,
        compiler_params=pltpu.CompilerParams(
            dimension_semantics=("arbitrary", "arbitrary", "arbitrary"),
            vmem_limit_bytes=V7X_VMEM_LIMIT_BYTES),
        name=("latent_mixer" if latent else "context_mixer") + ("_exact" if exact else ""),
    )(*args)


def _rope_tables(n_tok, dim):
    m = dim // 4
    t = np.arange(n_tok)
    inv = 1.0 / (ROPE_BASE ** (np.arange(m, dtype=np.float64) / m))
    ar = (t // GRID_W)[:, None] * inv
    ac = (t % GRID_W)[:, None] * inv
    cos = np.concatenate([np.cos(ar), np.cos(ar), np.cos(ac), np.cos(ac)], axis=1)
    sin = np.concatenate([-np.sin(ar), np.sin(ar), -np.sin(ac), np.sin(ac)], axis=1)
    first = np.tile(np.concatenate([np.ones(m), np.zeros(m)]), 2)[None, :]
    return cos.astype(np.float32), sin.astype(np.float32), first.astype(np.float32)


def _block_diag(block, n):
    out = np.zeros((block.shape[0] * n, block.shape[1] * n), np.float64)
    for i in range(n):
        out[i * block.shape[0]:(i + 1) * block.shape[0], i * block.shape[1]:(i + 1) * block.shape[1]] = block
    return out


def _dft_cos_sin(n):
    k = np.arange(n)
    ang = 2.0 * np.pi * ((k[:, None] * k[None, :]) % n) / n
    return np.cos(ang), np.sin(ang)


def _mixer_consts(seq, latent):
    c64, s64 = _dft_cos_sin(GROUP_C)
    norm = 1.0 / math.sqrt(GROUP_C * seq)
    cs, ss = _dft_cos_sin(seq)
    f32c = lambda a: jnp.asarray(np.asarray(a, np.float32))
    width = min(seq, TOKEN_BLOCK)
    col_blocks = lambda a: a.reshape(seq, seq // width, width).transpose(1, 0, 2)
    consts = {
        "bd64": f32c(_block_diag(np.full((HEAD_DIM, HEAD_DIM), 1.0 / HEAD_DIM), N_KV_A)).astype(_BF16),
        "bd32": f32c(_block_diag(np.full((DIFF_DIM, DIFF_DIM), 1.0 / DIFF_DIM), 2 * N_HEADS_B)).astype(_BF16),
        "bc": f32c(_block_diag(c64, N_GROUPS_C) * norm).astype(_BF16),
        "bs": f32c(_block_diag(s64, N_GROUPS_C) * norm).astype(_BF16),
        "cs": f32c(col_blocks(cs)).astype(_BF16),
        "nss": f32c(col_blocks(-ss)).astype(_BF16),
    }
    if latent:
        cos_a, sin_a, first_a = _rope_tables(seq, HEAD_DIM)
        cos_b, sin_b, first_b = _rope_tables(seq, DIFF_DIM)
        rep_a, rep_b = LANES // HEAD_DIM, LANES // DIFF_DIM
        consts.update({
            "cqa": f32c(cos_a.T), "sqa": f32c(sin_a.T), "cqb": f32c(cos_b.T), "sqb": f32c(sin_b.T),
            "cka": f32c(np.tile(cos_a, (1, rep_a))),
            "skpa": f32c(np.tile(sin_a * (1.0 - first_a), (1, rep_a))),
            "skma": f32c(np.tile(sin_a * first_a, (1, rep_a))),
            "ckb": f32c(np.tile(cos_b, (1, rep_b))),
            "skpb": f32c(np.tile(sin_b * (1.0 - first_b), (1, rep_b))),
            "skmb": f32c(np.tile(sin_b * first_b, (1, rep_b))),
        })
    return consts


def _mixer_weights(norm_g, w_in, q_norm_a, k_norm_a, q_norm_b, k_norm_b,
                   lambda_q1, lambda_k1, lambda_q2, lambda_k2, subln_g, w_fourier, w_out):
    keep = jnp.asarray([1.0 - v for v in LAM_INIT], _F32)
    gqa = q_norm_a * (HEAD_DIM ** -0.5 * LOG2_E)
    gqb = q_norm_b * (DIFF_DIM ** -0.5 * LOG2_E)
    def packed(rows, n_rows, width):
        parts = []
        for v in rows:
            parts += [v, jnp.zeros((DEPTH, width - v.shape[1]), _F32)]
        parts.append(jnp.zeros((DEPTH, (n_rows - len(rows)) * width), _F32))
        return jnp.concatenate(parts, axis=1).reshape(DEPTH, n_rows, width)

    prow = packed([jnp.tile(k_norm_a, (1, N_KV_A)), jnp.tile(k_norm_b, (1, 2 * N_HEADS_B)),
                   jnp.concatenate([lambda_q1, lambda_k1, lambda_q2, lambda_k2], axis=1)], P_ROWS, P_LANES)
    pcol = jnp.transpose(packed([gqa, gqb, subln_g * keep[:, None]], C_COLS, HEAD_DIM), (0, 2, 1))
    return {
        "norm_g": norm_g.reshape(DEPTH, 1, D_MODEL),
        "wft": _relayout(w_in, FEATURE_TILES, WEIGHT_TILE, True, "w_in_feature_major"),
        "wt": _relayout(w_in, TOKEN_TILES, WEIGHT_TILE, False, "w_in_token_major"),
        "wct": _relayout(w_fourier, ((0,),), WIDTH_C, True, "w_fourier_t"),
        "wout": _relayout(w_out, ((0,),), D_MODEL, False, "w_out_bf16"),
        "prow": prow,
        "pcol": pcol,
    }


def kernel(x_prompt, x_sample, cache_attn_a, cache_attn_b, c, c_ctx, norm_g, w_mod, b_mod, w_in,
           q_norm_a, k_norm_a, q_norm_b, k_norm_b, lambda_q1, lambda_k1, lambda_q2, lambda_k2,
           subln_g, w_fourier, w_out):
    n_ctx_batch, ctx_seq, _ = x_prompt.shape
    n_lat_batch, lat_seq, _ = x_sample.shape
    past = cache_attn_a.shape[2]
    assert n_lat_batch + 1 <= MOD_ROWS and w_in.shape == (DEPTH, D_MODEL, D_IN)

    cond = jnp.concatenate(
        [c_ctx[None, :], c, jnp.zeros((MOD_ROWS - 1 - n_lat_batch, D_MODEL), _F32)], axis=0)
    mod = _modulation(cond, w_mod, b_mod)

    weights = _mixer_weights(norm_g, w_in, q_norm_a, k_norm_a, q_norm_b, k_norm_b,
                             lambda_q1, lambda_k1, lambda_q2, lambda_k2, subln_g, w_fourier, w_out)

    def feature_major(cache):
        n_heads, dim = cache.shape[-2:]
        return jnp.transpose(cache, (0, 1, 3, 4, 5, 2)).reshape(n_lat_batch, DEPTH, 2, n_heads * dim, past)

    def token_major(new, n_heads):
        seq = new.shape[-1]
        return jnp.transpose(new.reshape(n_ctx_batch, DEPTH, 2, n_heads, HEAD_DIM, seq), (0, 1, 5, 2, 3, 4))

    ctx_a, ctx_b = feature_major(cache_attn_a), feature_major(cache_attn_b)

    def max_norm(gain, dim):
        return math.sqrt(dim) * jnp.max(jnp.abs(gain), axis=-1)

    q_a = max_norm(q_norm_a * (HEAD_DIM ** -0.5 * LOG2_E), HEAD_DIM)
    q_b = max_norm(q_norm_b * (DIFF_DIM ** -0.5 * LOG2_E), DIFF_DIM)
    k_a, k_b = max_norm(k_norm_a, HEAD_DIM), max_norm(k_norm_b, DIFF_DIM)
    cached_a, cached_b = _cached_key_norms(ctx_a, ctx_b)
    ctx_bound = GUARD_SLACK * jnp.maximum(q_a * k_a, q_b * k_b)
    lat_bound = GUARD_SLACK * jnp.maximum(q_a * jnp.maximum(k_a, cached_a), q_b * jnp.maximum(k_b, cached_b))

    ctx_consts = _mixer_consts(ctx_seq, latent=False)
    lat_consts = _mixer_consts(lat_seq, latent=True)

    def context(exact):
        return _mixer(x_prompt, mod, weights, ctx_consts, nb=TOKEN_BLOCK // ctx_seq, exact=exact)

    def latent(exact, x):
        return _mixer(x, mod, weights, lat_consts, nb=1, exact=exact, ctx=(ctx_a, ctx_b))

    def both(exact):
        ctx_out = context(exact)
        _, x = lax.optimization_barrier((ctx_out[0], x_sample))
        return ctx_out, latent(exact, x)

    bounded = jnp.all(ctx_bound <= MAX_SCORE_BOUND) & jnp.all(lat_bound <= MAX_SCORE_BOUND)
    (y_prompt, new_a, new_b), y_sample = lax.cond(bounded, functools.partial(both, False),
                                                  functools.partial(both, True))
    return (y_prompt, y_sample, token_major(new_a, N_KV_A), token_major(new_b, N_HEADS_B))
```
